```python
import math
import jax, jax.numpy as jnp
from jax import lax
import numpy as np


D_MODEL = 1024
BATCH = 8
SEQ = 4096
DEPTH = 2

CHUNK = 64
Q_BLOCK = 128
HEAD_DIM = 64
A_HEADS = 4
A_V_DIM = 2 * HEAD_DIM
B_HEADS = 8
B_KV_HEADS = 2
B_GROUP = B_HEADS // B_KV_HEADS
WINDOW = 128
W_CHUNKS = WINDOW // CHUNK
BAND = (W_CHUNKS + 1) * CHUNK
C_HEADS = 8
IDX_HEADS = 4
IDX_DIM = 32
TOPK_MAX = 256
NUM_BUCKETS = 32
MAX_DISTANCE = 1024
REL_HEADS = A_HEADS + B_HEADS + C_HEADS
N_BRANCH = 3
BRANCH_WIDTH = 512
N_GROUPS = 4
EXPERTS_PER_GROUP = 4
N_EXPERTS = N_GROUPS * EXPERTS_PER_GROUP
TOP_E = 2
EXPERT_FF = 256
EPS = 1e-6
NEG = -1e30

COL_SIZES = (A_HEADS * 2 * HEAD_DIM, A_HEADS * 2 * HEAD_DIM, A_HEADS * A_V_DIM,
             B_HEADS * HEAD_DIM, B_KV_HEADS * HEAD_DIM, B_KV_HEADS * HEAD_DIM,
             C_HEADS * HEAD_DIM, HEAD_DIM, HEAD_DIM,
             IDX_HEADS * IDX_DIM, IDX_DIM, IDX_HEADS)
IN_COLS = sum(COL_SIZES)

kernel_name = 'hybrid_chunk_causal_block'


def rms_norm(x, g):
    xf = x.astype(jnp.float32)
    y = xf * lax.rsqrt(jnp.mean(xf * xf, axis=-1, keepdims=True) + EPS)
    return (y * g.astype(jnp.float32)).astype(x.dtype)


def t5_bucket(rel):
    half = NUM_BUCKETS // 2
    max_exact = half // 2
    n = jnp.abs(rel)
    n_f = jnp.maximum(n, 1).astype(jnp.float32)
    large = max_exact + (jnp.log(n_f / max_exact) / math.log(MAX_DISTANCE / max_exact)
                         * (half - max_exact)).astype(jnp.int32)
    large = jnp.minimum(large, half - 1)
    return jnp.where(rel > 0, half, 0) + jnp.where(n < max_exact, n, large)


def band_chunks(t, n_chunks):
    tc = t.reshape(t.shape[0], n_chunks, CHUNK, *t.shape[2:])
    pad = [(0, 0), (W_CHUNKS, 0)] + [(0, 0)] * (tc.ndim - 2)
    tp = jnp.pad(tc, pad)
    return jnp.concatenate([tp[:, i:i + n_chunks] for i in range(W_CHUNKS + 1)], axis=2)


def diff_attention(q, k, v, bias_tab, q_g, k_g, lam_par, subln_g, lambda_init):
    bsz, seq = q.shape[0], q.shape[1]
    n_blocks = seq // Q_BLOCK
    q = rms_norm(q, q_g) * (HEAD_DIM ** -0.5)
    k = rms_norm(k, k_g)
    lp = lam_par.astype(jnp.float32)
    lam = jnp.exp(jnp.sum(lp[0] * lp[1])) - jnp.exp(jnp.sum(lp[2] * lp[3])) + lambda_init
    q_blocks = jnp.moveaxis(q.reshape(bsz, n_blocks, Q_BLOCK, A_HEADS, 2, HEAD_DIM), 1, 0)
    starts = jnp.arange(n_blocks, dtype=jnp.int32) * Q_BLOCK
    key_pos = jnp.arange(seq, dtype=jnp.int32)

    def one_block(args):
        qb, start = args
        q_pos = start + jnp.arange(Q_BLOCK, dtype=jnp.int32)
        bias = bias_tab[t5_bucket(key_pos[None, :] - q_pos[:, None])]
        allowed = (key_pos[None, :] // CHUNK) <= (q_pos[:, None] // CHUNK)
        s = jnp.einsum('bqhmd,bkhmd->bhmqk', qb, k).astype(jnp.float32)
        s = s + jnp.transpose(bias, (2, 0, 1)).astype(jnp.float32)[None, :, None]
        s = jnp.where(allowed, s, NEG)
        p = jax.nn.softmax(s, axis=-1)
        p = p[:, :, 0] - lam * p[:, :, 1]
        return jnp.einsum('bhqk,bkhe->bqhe', p.astype(v.dtype), v)

    out = lax.map(one_block, (q_blocks, starts))
    out = jnp.moveaxis(out, 0, 1).reshape(bsz, seq, A_HEADS, A_V_DIM)
    out = rms_norm(out, subln_g) * (1.0 - lambda_init)
    return out.reshape(bsz, seq, A_HEADS * A_V_DIM)


def sliding_sink_attention(q, k, v, bias_tab, q_g, k_g, sinks):
    bsz, seq = q.shape[0], q.shape[1]
    n_chunks = seq // CHUNK
    q = rms_norm(q, q_g) * (HEAD_DIM ** -0.5)
    k = rms_norm(k, k_g)
    q = q.reshape(bsz, n_chunks, CHUNK, B_KV_HEADS, B_GROUP, HEAD_DIM)
    k_band = band_chunks(k, n_chunks)
    v_band = band_chunks(v, n_chunks)
    k_off = jnp.arange(BAND, dtype=jnp.int32) - W_CHUNKS * CHUNK
    q_off = jnp.arange(CHUNK, dtype=jnp.int32)
    bias = bias_tab[t5_bucket(k_off[None, :] - q_off[:, None])]
    bias = jnp.transpose(bias.reshape(CHUNK, BAND, B_KV_HEADS, B_GROUP), (2, 3, 0, 1)).astype(jnp.float32)
    valid = (jnp.arange(n_chunks, dtype=jnp.int32)[:, None] * CHUNK + k_off[None, :]) >= 0
    s = jnp.einsum('bcqhgd,bckhd->bchgqk', q, k_band).astype(jnp.float32) + bias
    s = jnp.where(valid[None, :, None, None, None, :], s, NEG)
    sink = sinks.astype(jnp.float32).reshape(B_KV_HEADS, B_GROUP)[None, None, :, :, None, None]
    m = jnp.maximum(jnp.max(s, axis=-1, keepdims=True), sink)
    e = jnp.exp(s - m)
    p = e / (jnp.sum(e, axis=-1, keepdims=True) + jnp.exp(sink - m))
    out = jnp.einsum('bchgqk,bckhd->bcqhgd', p.astype(v.dtype), v_band)
    return out.reshape(bsz, seq, B_HEADS * HEAD_DIM)


def dsa_attention(q, k, v, q_idx, k_idx, w_idx, bias_tab, q_g, k_g):
    bsz, seq = q.shape[0], q.shape[1]
    n_blocks = seq // Q_BLOCK
    top_k = min(TOPK_MAX, seq // 4)
    q = rms_norm(q, q_g) * (HEAD_DIM ** -0.5)
    k = rms_norm(k, k_g)
    w_idx = w_idx.astype(jnp.float32) * (IDX_HEADS ** -0.5 * IDX_DIM ** -0.5)

    def blocks(t):
        return jnp.moveaxis(t.reshape(bsz, n_blocks, Q_BLOCK, *t.shape[2:]), 1, 0)

    starts = jnp.arange(n_blocks, dtype=jnp.int32) * Q_BLOCK
    key_chunk = jnp.arange(seq, dtype=jnp.int32) // CHUNK
    gather = jax.vmap(lambda table, idx: table[idx])

    def one_block(args):
        qb, qib, wib, start = args
        q_pos = start + jnp.arange(Q_BLOCK, dtype=jnp.int32)
        q_chunk = q_pos // CHUNK
        logits = jnp.einsum('bqhd,bkd->bqhk', qib, k_idx).astype(jnp.float32)
        score = jnp.einsum('bqhk,bqh->bqk', jax.nn.relu(logits), wib)
        score = jnp.where(key_chunk[None, None, :] <= q_chunk[None, :, None], score, NEG)
        _, sel = lax.top_k(score, top_k)
        k_sel = gather(k, sel)
        v_sel = gather(v, sel)
        bias = bias_tab[t5_bucket(sel - q_pos[None, :, None])]
        ok = (sel // CHUNK) <= q_chunk[None, :, None]
        s = jnp.einsum('bqhd,bqkd->bqhk', qb, k_sel).astype(jnp.float32)
        s = s + jnp.swapaxes(bias, -1, -2).astype(jnp.float32)
        s = jnp.where(ok[:, :, None, :], s, NEG)
        p = jax.nn.softmax(s, axis=-1)
        return jnp.einsum('bqhk,bqkd->bqhd', p.astype(v.dtype), v_sel)

    out = lax.map(one_block, (blocks(q), blocks(q_idx), blocks(w_idx), starts))
    return jnp.moveaxis(out, 0, 1).reshape(bsz, seq, C_HEADS * HEAD_DIM)


def hybrid_mixer(h, rel_bias, w_in, qk_g, lam_par, subln_g, sinks, w_branch, w_gate, b_gate, w_out, lambda_init):
    bsz, seq = h.shape[0], h.shape[1]
    proj = h @ w_in
    split_points = np.cumsum(COL_SIZES)[:-1].tolist()
    aq, ak, av, bq, bk, bv, cq, ck, cv, iq, ik, iw = jnp.split(proj, split_points, axis=-1)
    oa = diff_attention(aq.reshape(bsz, seq, A_HEADS, 2, HEAD_DIM),
                        ak.reshape(bsz, seq, A_HEADS, 2, HEAD_DIM),
                        av.reshape(bsz, seq, A_HEADS, A_V_DIM),
                        rel_bias[:, :A_HEADS], qk_g[0, 0], qk_g[0, 1], lam_par, subln_g, lambda_init)
    ob = sliding_sink_attention(bq.reshape(bsz, seq, B_HEADS, HEAD_DIM),
                                bk.reshape(bsz, seq, B_KV_HEADS, HEAD_DIM),
                                bv.reshape(bsz, seq, B_KV_HEADS, HEAD_DIM),
                                rel_bias[:, A_HEADS:A_HEADS + B_HEADS], qk_g[1, 0], qk_g[1, 1], sinks)
    oc = dsa_attention(cq.reshape(bsz, seq, C_HEADS, HEAD_DIM), ck, cv,
                       iq.reshape(bsz, seq, IDX_HEADS, IDX_DIM), ik, iw,
                       rel_bias[:, A_HEADS + B_HEADS:], qk_g[2, 0], qk_g[2, 1])
    branches = jnp.stack([oa, ob, oc], axis=2)
    y = jnp.einsum('bsnw,nwd->bsnd', branches, w_branch)
    gates = jax.nn.sigmoid(h @ w_gate + b_gate).reshape(bsz, seq, N_BRANCH, D_MODEL)
    return jnp.sum(gates * y, axis=2) @ w_out


def hier_moe(h, w_rg, b_rg, w_re, b_re, w_g, w_u, w_d):
    bsz, seq = h.shape[0], h.shape[1]
    g_logits = (h @ w_rg + b_rg).astype(jnp.float32)
    g_prob = jax.nn.softmax(g_logits, axis=-1)
    _, g_sel = lax.top_k(g_logits, 1)
    p_group = jnp.take_along_axis(g_prob, g_sel, axis=-1)
    e_logits = (h @ w_re + b_re).astype(jnp.float32).reshape(bsz, seq, N_GROUPS, EXPERTS_PER_GROUP)
    e_in_group = jnp.take_along_axis(e_logits, g_sel[..., None], axis=2)[:, :, 0]
    e_val, e_sel = lax.top_k(e_in_group, TOP_E)
    weights = jax.nn.softmax(e_val, axis=-1) * p_group
    expert_id = g_sel * EXPERTS_PER_GROUP + e_sel
    gate = jnp.sum(jax.nn.one_hot(expert_id, N_EXPERTS, dtype=jnp.float32) * weights[..., None], axis=2)
    hid = jax.nn.silu(jnp.einsum('bsd,edf->bsef', h, w_g)) * jnp.einsum('bsd,edf->bsef', h, w_u)
    hid = hid * gate[..., None].astype(hid.dtype)
    return jnp.einsum('bsef,efd->bsd', hid, w_d)


def setup_inputs(seed: int = 0) -> dict:
    key = jax.random.key(seed)
    ks = jax.random.split(key, 20)

    def nrm(k, shape, scale):
        return jax.random.normal(k, shape, jnp.float32) * scale

    return {
        'x': nrm(ks[0], (BATCH, SEQ, D_MODEL), 1.0),
        'rel_bias': nrm(ks[1], (NUM_BUCKETS, REL_HEADS), 0.5),
        'norm_mix_g': 1.0 + nrm(ks[2], (DEPTH, D_MODEL), 0.05),
        'w_in': nrm(ks[3], (DEPTH, D_MODEL, IN_COLS), D_MODEL ** -0.5),
        'qk_norm_g': 1.0 + nrm(ks[4], (DEPTH, N_BRANCH, 2, HEAD_DIM), 0.05),
        'diff_lambda': nrm(ks[5], (DEPTH, 4, HEAD_DIM), 0.1),
        'diff_subln_g': 1.0 + nrm(ks[6], (DEPTH, A_V_DIM), 0.05),
        'sinks': nrm(ks[7], (DEPTH, B_HEADS), 0.5),
        'w_branch': nrm(ks[8], (DEPTH, N_BRANCH, BRANCH_WIDTH, D_MODEL), BRANCH_WIDTH ** -0.5),
        'w_gate': nrm(ks[9], (DEPTH, D_MODEL, N_BRANCH * D_MODEL), D_MODEL ** -0.5),
        'b_gate': nrm(ks[10], (DEPTH, N_BRANCH * D_MODEL), 0.02),
        'w_out': nrm(ks[11], (DEPTH, D_MODEL, D_MODEL), D_MODEL ** -0.5),
        'norm_ffn_g': 1.0 + nrm(ks[12], (DEPTH, D_MODEL), 0.05),
        'w_router_group': nrm(ks[13], (DEPTH, D_MODEL, N_GROUPS), D_MODEL ** -0.5),
        'b_router_group': nrm(ks[14], (DEPTH, N_GROUPS), 0.01),
        'w_router_expert': nrm(ks[15], (DEPTH, D_MODEL, N_EXPERTS), D_MODEL ** -0.5),
        'b_router_expert': nrm(ks[16], (DEPTH, N_EXPERTS), 0.01),
        'w_ff_gate': nrm(ks[17], (DEPTH, N_EXPERTS, D_MODEL, EXPERT_FF), D_MODEL ** -0.5),
        'w_ff_up': nrm(ks[18], (DEPTH, N_EXPERTS, D_MODEL, EXPERT_FF), D_MODEL ** -0.5),
        'w_ff_down': nrm(ks[19], (DEPTH, N_EXPERTS, EXPERT_FF, D_MODEL), EXPERT_FF ** -0.5),
    }


def reference(x, rel_bias, norm_mix_g, w_in, qk_norm_g, diff_lambda, diff_subln_g, sinks,
              w_branch, w_gate, b_gate, w_out, norm_ffn_g, w_router_group, b_router_group,
              w_router_expert, b_router_expert, w_ff_gate, w_ff_up, w_ff_down):
    for l in range(DEPTH):
        lambda_init = 0.8 - 0.6 * math.exp(-0.3 * l)
        h = rms_norm(x, norm_mix_g[l])
        x = x + hybrid_mixer(h, rel_bias, w_in[l], qk_norm_g[l], diff_lambda[l], diff_subln_g[l],
                             sinks[l], w_branch[l], w_gate[l], b_gate[l], w_out[l], lambda_init)
        h = rms_norm(x, norm_ffn_g[l])
        x = x + hier_moe(h, w_router_group[l], b_router_group[l], w_router_expert[l], b_router_expert[l],
                         w_ff_gate[l], w_ff_up[l], w_ff_down[l])
    return x
```

```python
import functools
import math

import numpy as np
import jax
import jax.numpy as jnp
from jax import lax
from jax.experimental import pallas as pl
from jax.experimental.pallas import tpu as pltpu

F32 = jnp.float32
BF16 = jnp.bfloat16
I32 = jnp.int32

D_MODEL = 1024
DEPTH = 2
CHUNK = 64
HEAD_DIM = 64
A_HEADS = 4
A_V_DIM = 2 * HEAD_DIM
B_HEADS = 8
B_KV_HEADS = 2
B_GROUP = B_HEADS // B_KV_HEADS
W_CHUNKS = 2
C_HEADS = 8
IDX_HEADS = 4
IDX_DIM = 32
TOPK_MAX = 256
NUM_BUCKETS = 32
MAX_DISTANCE = 1024
N_BRANCH = 3
BRANCH_WIDTH = 512
N_GROUPS = 4
EXPERTS_PER_GROUP = 4
N_EXPERTS = N_GROUPS * EXPERTS_PER_GROUP
EXPERT_FF = 256
EPS = 1e-6
NEG = -1e30
INT_MIN = -(2 ** 31)

LANES = 128
SUBLANES = 8
TILE = 256
NEAR_TILES = 4
VMEM_CAP = 60000 * 1024

_C_AQ, _C_AK, _C_AV, _C_BQ, _C_CQ = 0, 512, 1024, 1536, 2048
_C_BK0, _C_BK1, _C_BV, _C_IQ, _C_MISC = 2560, 2688, 2816, 2944, 3072
_W_COLS = 3328
_M_CV, _M_IK, _M_IW = 64, 128, 160


def _dot(a, b):
    return jnp.dot(a, b, preferred_element_type=F32)


def _split_bf16(a):
    hi = a.astype(BF16)
    lo = (a - hi.astype(F32)).astype(BF16)
    return hi, lo


def _sigmoid(x):
    return 1.0 / (1.0 + jnp.exp(-x))


def _rms(x, g):
    return x * lax.rsqrt(jnp.mean(x * x, axis=-1, keepdims=True) + EPS) * g


def _colmax8(s):
    r, c = s.shape
    return jnp.max(s.reshape(r // SUBLANES, SUBLANES, c), axis=0)


def _colsum8(s):
    r, c = s.shape
    return jnp.sum(s.reshape(r // SUBLANES, SUBLANES, c), axis=0)


def _vmem_limit(nbytes):
    return int(min(VMEM_CAP, nbytes))


def _proj_kernel(x_ref, g_ref, w_ref, seg_ref, gaq_ref, gak_ref, gbq_ref, gbk_ref, gcq_ref, gck_ref,
                 aqT_ref, ak_ref, avT_ref, bqT_ref, bk_ref, bvT_ref, cqT_ref, ck_ref, cvT_ref,
                 iqT_ref, ik_ref, iwT_ref, *, tm, iw_scale):
    hb = _rms(x_ref[0], g_ref[...]).astype(BF16)
    seg = seg_ref[...]
    n_sub = tm // TILE

    def grp(a, n):
        return _dot(hb, w_ref[:, a:a + n])

    def segnorm(t, g):
        hi, lo = _split_bf16(t * t)
        ssq = _dot(hi, seg) + _dot(lo, seg)
        return t * lax.rsqrt(ssq * (1.0 / HEAD_DIM) + EPS) * g

    def put_slabs(ref, tT):
        for s in range(n_sub):
            ref[0, s] = tT[:, s * TILE:(s + 1) * TILE].astype(BF16)

    aqT_ref[0] = segnorm(grp(_C_AQ, 512), gaq_ref[...]).T.astype(BF16)
    ak_ref[0] = segnorm(grp(_C_AK, 512), gak_ref[...]).astype(BF16)
    put_slabs(avT_ref, grp(_C_AV, 512).T)
    bqT_ref[0] = segnorm(grp(_C_BQ, 512), gbq_ref[...]).T.astype(BF16)
    cqT_ref[0] = segnorm(grp(_C_CQ, 512), gcq_ref[...]).T.astype(BF16)
    for g, col in enumerate((_C_BK0, _C_BK1)):
        t = grp(col, LANES)
        ssq = jnp.sum(t * t, axis=-1, keepdims=True)
        t = t * lax.rsqrt(ssq * (1.0 / HEAD_DIM) + EPS)
        bk_ref[0, g] = (t[:, :HEAD_DIM] * gbk_ref[...]).astype(BF16)
    put_slabs(bvT_ref, grp(_C_BV, LANES).T)
    iqT_ref[0] = grp(_C_IQ, LANES).T.astype(BF16)
    misc = grp(_C_MISC, 2 * LANES)
    ck = misc[:, :HEAD_DIM]
    ssq = jnp.sum(ck * ck, axis=-1, keepdims=True)
    ck_ref[0] = (ck * lax.rsqrt(ssq * (1.0 / HEAD_DIM) + EPS) * gck_ref[...]).astype(BF16)
    ik_ref[0] = misc[:, _M_IK:_M_IK + IDX_DIM].astype(BF16)
    miscT = misc.T
    put_slabs(cvT_ref, miscT[_M_CV:_M_CV + HEAD_DIM, :])
    iwT_ref[0] = miscT[_M_IW:_M_IW + IDX_HEADS, :] * iw_scale


def _proj(x, g, w, seg, gains, tm):
    B, S, D = x.shape
    nt = S // TILE
    n_sub = tm // TILE
    full = lambda shape: pl.BlockSpec(shape, lambda b, t: (0,) * len(shape))
    out_shape = [
        jax.ShapeDtypeStruct((B, 512, S), BF16),
        jax.ShapeDtypeStruct((B, S, 512), BF16),
        jax.ShapeDtypeStruct((B, nt, 512, TILE), BF16),
        jax.ShapeDtypeStruct((B, 512, S), BF16),
        jax.ShapeDtypeStruct((B, B_KV_HEADS, S, HEAD_DIM), BF16),
        jax.ShapeDtypeStruct((B, nt, LANES, TILE), BF16),
        jax.ShapeDtypeStruct((B, 512, S), BF16),
        jax.ShapeDtypeStruct((B, S, HEAD_DIM), BF16),
        jax.ShapeDtypeStruct((B, nt, HEAD_DIM, TILE), BF16),
        jax.ShapeDtypeStruct((B, LANES, S), BF16),
        jax.ShapeDtypeStruct((B, S, IDX_DIM), BF16),
        jax.ShapeDtypeStruct((B, IDX_HEADS, S), F32),
    ]
    colT = lambda r: pl.BlockSpec((1, r, tm), lambda b, t: (b, 0, t))
    row = lambda c: pl.BlockSpec((1, tm, c), lambda b, t: (b, t, 0))
    slab = lambda r: pl.BlockSpec((1, n_sub, r, TILE), lambda b, t: (b, t, 0, 0))
    out_specs = [colT(512), row(512), slab(512), colT(512),
                 pl.BlockSpec((1, B_KV_HEADS, tm, HEAD_DIM), lambda b, t: (b, 0, t, 0)),
                 slab(LANES), colT(512), row(HEAD_DIM), slab(HEAD_DIM), colT(LANES), row(IDX_DIM),
                 pl.BlockSpec((1, IDX_HEADS, tm), lambda b, t: (b, 0, t))]
    in_specs = [pl.BlockSpec((1, tm, D), lambda b, t: (b, t, 0)), full((1, D)), full((D, _W_COLS)),
                full((512, 512))] + [full(gn.shape) for gn in gains]
    vmem = 2 * (tm * D * 4 + D * _W_COLS * 2 + 512 * 512 * 2 + tm * 3400 * 2) + 24 * tm * 512 * 4
    return pl.pallas_call(
        functools.partial(_proj_kernel, tm=tm, iw_scale=IDX_HEADS ** -0.5 * IDX_DIM ** -0.5),
        grid=(B, S // tm), in_specs=in_specs, out_specs=out_specs, out_shape=out_shape,
        compiler_params=pltpu.CompilerParams(dimension_semantics=("arbitrary", "arbitrary"),
                                             vmem_limit_bytes=_vmem_limit(vmem)),
        name="proj",
    )(x, g, w, seg, *gains)


def _attn_a_kernel(qT_ref, k_ref, vT_ref, bias_ref, lam_ref, sub_ref, o_ref, l_ref, acc_ref, *, lambda_init):
    i = pl.program_id(1)
    n_far = jnp.maximum(i - (NEAR_TILES - 1), 0)
    n_near = jnp.minimum(i, NEAR_TILES - 1) + 1
    lp = lam_ref[...]
    lam = (jnp.exp(jnp.sum(lp[0:1] * lp[1:2], axis=-1, keepdims=True))
           - jnp.exp(jnp.sum(lp[2:3] * lp[3:4], axis=-1, keepdims=True)) + lambda_init)
    row = lax.broadcasted_iota(I32, (2 * HEAD_DIM, TILE), 0)

    for h in range(A_HEADS):
        hs = slice(h * A_V_DIM, (h + 1) * A_V_DIM)
        qh = qT_ref[0, hs, :]
        zero = jnp.zeros_like(qh)
        q2 = (jnp.where(row < HEAD_DIM, qh, zero), jnp.where(row >= HEAD_DIM, qh, zero))

        def scores(j, c, bias):
            kt = k_ref[0, pl.ds(pl.multiple_of(j * TILE, TILE), TILE), hs]
            s = _dot(kt, q2[c])
            return s if bias is None else s + bias

        def max_far(j, m):
            return tuple(jnp.maximum(m[c], _colmax8(scores(j, c, None))) for c in (0, 1))

        def max_near(d, m):
            b = bias_ref[h, d]
            return tuple(jnp.maximum(m[c], _colmax8(scores(i - d, c, b))) for c in (0, 1))

        m0 = jnp.full((SUBLANES, TILE), -jnp.inf, F32)
        m = lax.fori_loop(0, n_far, max_far, (m0, m0))
        m = lax.fori_loop(0, n_near, max_near, m)
        m = tuple(jnp.max(mc, axis=0, keepdims=True) for mc in m)

        l_ref[...] = jnp.zeros_like(l_ref)
        acc_ref[...] = jnp.zeros_like(acc_ref)

        def accumulate(j, bias):
            vt = vT_ref[0, j, hs, :]
            for c in (0, 1):
                e = jnp.exp(scores(j, c, bias) - m[c])
                l_ref[c] += _colsum8(e)
                acc_ref[c] += _dot(vt, e.astype(BF16))

        def acc_far(j, carry):
            accumulate(j, None)
            return carry

        def acc_near(d, carry):
            accumulate(i - d, bias_ref[h, d])
            return carry

        lax.fori_loop(0, n_far, acc_far, 0)
        lax.fori_loop(0, n_near, acc_near, 0)

        r0 = 1.0 / jnp.sum(l_ref[0], axis=0, keepdims=True)
        r1 = 1.0 / jnp.sum(l_ref[1], axis=0, keepdims=True)
        outT = acc_ref[0] * r0 - lam * (acc_ref[1] * r1)
        out = outT.T
        out = _rms(out, sub_ref[...]) * (1.0 - lambda_init)
        o_ref[0, :, hs] = out.astype(BF16)


def _attn_a(aqT, ak, avT, bias, lam_par, subln_g, lambda_init):
    B, _, S = aqT.shape
    nt = S // TILE
    vmem = 2 * (512 * TILE * 2 + 2 * S * 512 * 2 + bias.size * 4 + TILE * 512 * 2) + 16 * TILE * TILE * 4
    return pl.pallas_call(
        functools.partial(_attn_a_kernel, lambda_init=lambda_init),
        grid=(B, nt),
        in_specs=[pl.BlockSpec((1, 512, TILE), lambda b, i: (b, 0, i)),
                  pl.BlockSpec((1, S, 512), lambda b, i: (b, 0, 0)),
                  pl.BlockSpec((1, nt, 512, TILE), lambda b, i: (b, 0, 0, 0)),
                  pl.BlockSpec(bias.shape, lambda b, i: (0, 0, 0, 0)),
                  pl.BlockSpec((4, HEAD_DIM), lambda b, i: (0, 0)),
                  pl.BlockSpec((1, A_V_DIM), lambda b, i: (0, 0))],
        out_specs=pl.BlockSpec((1, TILE, 512), lambda b, i: (b, i, 0)),
        out_shape=jax.ShapeDtypeStruct((B, S, 512), BF16),
        scratch_shapes=[pltpu.VMEM((2, SUBLANES, TILE), F32), pltpu.VMEM((2, A_V_DIM, TILE), F32)],
        compiler_params=pltpu.CompilerParams(dimension_semantics=("arbitrary", "arbitrary"),
                                             vmem_limit_bytes=_vmem_limit(vmem)),
        name="attn_a",
    )(aqT, ak, avT, bias, lam_par, subln_g)


def _attn_b_kernel(sink_ref, qT_ref, k_ref, vT_ref, bias_ref, o_ref, oT_ref):
    i = pl.program_id(1)
    jp = jnp.maximum(i - 1, 0)
    p_idx = jnp.where(i > 0, 1, 2)
    cur = pl.ds(pl.multiple_of(i * TILE, TILE), TILE)
    prev = pl.ds(pl.multiple_of(jp * TILE, TILE), TILE)
    for g in range(B_KV_HEADS):
        kc = k_ref[0, g, cur, :]
        kp = k_ref[0, g, prev, :]
        gs = slice(g * HEAD_DIM, (g + 1) * HEAD_DIM)
        vc = vT_ref[0, i, gs, :]
        vp = vT_ref[0, jp, gs, :]
        for hh in range(B_GROUP):
            h = g * B_GROUP + hh
            hs = slice(h * HEAD_DIM, (h + 1) * HEAD_DIM)
            qh = qT_ref[0, hs, :]
            sc = _dot(kc, qh) + bias_ref[h, 0]
            sp = _dot(kp, qh) + bias_ref[h, p_idx]
            sink = sink_ref[h]
            m = jnp.max(jnp.maximum(_colmax8(sc), _colmax8(sp)), axis=0, keepdims=True)
            m = jnp.maximum(m, sink)
            ec = jnp.exp(sc - m)
            ep = jnp.exp(sp - m)
            den = jnp.sum(_colsum8(ec) + _colsum8(ep), axis=0, keepdims=True) + jnp.exp(sink - m)
            outT = _dot(vc, ec.astype(BF16)) + _dot(vp, ep.astype(BF16))
            oT_ref[hs, :] = outT * (1.0 / den)
    o_ref[0] = oT_ref[...].T.astype(BF16)


def _attn_b(bqT, bk, bvT, bias, sinks):
    B, _, S = bqT.shape
    nt = S // TILE
    vmem = 2 * (512 * TILE * 2 + 2 * S * LANES * 2 + S * LANES * 2 + bias.size * 4 + TILE * 512 * 2) \
        + 24 * TILE * TILE * 4
    return pl.pallas_call(
        _attn_b_kernel,
        grid=(B, nt),
        in_specs=[pl.BlockSpec(memory_space=pltpu.SMEM),
                  pl.BlockSpec((1, 512, TILE), lambda b, i: (b, 0, i)),
                  pl.BlockSpec((1, B_KV_HEADS, S, HEAD_DIM), lambda b, i: (b, 0, 0, 0)),
                  pl.BlockSpec((1, nt, LANES, TILE), lambda b, i: (b, 0, 0, 0)),
                  pl.BlockSpec(bias.shape, lambda b, i: (0, 0, 0, 0))],
        out_specs=pl.BlockSpec((1, TILE, 512), lambda b, i: (b, i, 0)),
        out_shape=jax.ShapeDtypeStruct((B, S, 512), BF16),
        scratch_shapes=[pltpu.VMEM((512, TILE), F32)],
        compiler_params=pltpu.CompilerParams(dimension_semantics=("arbitrary", "arbitrary"),
                                             vmem_limit_bytes=_vmem_limit(vmem)),
        name="attn_b",
    )(sinks, bqT, bk, bvT, bias)


def _attn_c_kernel(qT_ref, k_ref, vT_ref, iqT_ref, ik_ref, iwT_ref, bias_ref, o_ref,
                   keys_ref, mb_ref, l_ref, acc_ref, oT_ref, *, top_k, idx_bits):
    i = pl.program_id(1)
    n_t = i + 1
    n_far = jnp.maximum(i - (NEAR_TILES - 1), 0)
    n_near = jnp.minimum(i, NEAR_TILES - 1) + 1
    krow = lax.broadcasted_iota(I32, (TILE, TILE), 0)
    qcol = lax.broadcasted_iota(I32, (TILE, TILE), 1)
    allowed = (krow // CHUNK) <= (qcol // CHUNK)

    def idx_keys(j, diag):
        ikt = ik_ref[0, pl.ds(pl.multiple_of(j * TILE, TILE), TILE), :]
        sc = jnp.zeros((TILE, TILE), F32)
        for hh in range(IDX_HEADS):
            lg = _dot(ikt, iqT_ref[0, hh * IDX_DIM:(hh + 1) * IDX_DIM, :])
            sc = sc + jnp.maximum(lg, 0.0) * iwT_ref[0, hh:hh + 1, :]
        if diag:
            sc = jnp.where(allowed, sc, NEG)
        bits = lax.bitcast_convert_type(sc, I32)
        keys_ref[j] = bits ^ ((bits >> 31) & 0x7FFFFFFF)

    def fill(j, carry):
        idx_keys(j, False)
        return carry

    lax.fori_loop(0, i, fill, 0)
    idx_keys(i, True)

    def count(pred8):
        def body(j, c8):
            kk = keys_ref[j].reshape(TILE // SUBLANES, SUBLANES, TILE)
            return c8 + jnp.sum(pred8(kk, j).astype(I32), axis=0)
        c8 = lax.fori_loop(0, n_t, body, jnp.zeros((SUBLANES, TILE), I32))
        return jnp.sum(c8, axis=0, keepdims=True)

    def bcast8(v):
        return jnp.broadcast_to(v, (SUBLANES, TILE))

    def bit_step(b, t):
        cand = t + lax.shift_left(jnp.int32(1), 31 - b)
        c8 = bcast8(cand)
        cnt = count(lambda kk, j: kk >= c8[None])
        return jnp.where(cnt >= top_k, cand, t)

    t = lax.fori_loop(0, 32, bit_step, jnp.full((1, TILE), INT_MIN, I32))
    t8 = bcast8(t)
    cnt_ge = count(lambda kk, j: kk >= t8[None])
    need_tb = jnp.max(jnp.where(cnt_ge > top_k, 1.0, 0.0))

    sub3 = (lax.broadcasted_iota(I32, (TILE // SUBLANES, SUBLANES, TILE), 0) * SUBLANES
            + lax.broadcasted_iota(I32, (TILE // SUBLANES, SUBLANES, TILE), 1))

    def tie_break():
        cnt_gt = count(lambda kk, j: kk > t8[None])
        r = top_k - cnt_gt

        def idx_step(b, p):
            cand = p + lax.shift_left(jnp.int32(1), idx_bits - 1 - b)
            c8 = bcast8(cand)
            cnt = count(lambda kk, j: (kk == t8[None]) & ((sub3 + j * TILE) < c8[None]))
            return jnp.where(cnt < r, cand, p)

        return lax.fori_loop(0, idx_bits, idx_step, jnp.zeros((1, TILE), I32))

    p_last = lax.cond(need_tb > 0.5, tie_break, lambda: jnp.full((1, TILE), 2 ** idx_bits, I32))

    def mask_tile(j, diag):
        kk = keys_ref[j]
        sel = (kk > t) | ((kk == t) & ((krow + j * TILE) <= p_last))
        if diag:
            sel = sel & allowed
        mb_ref[j] = jnp.where(sel, 0.0, NEG)

    def mask_far(j, carry):
        mask_tile(j, False)
        return carry

    lax.fori_loop(0, i, mask_far, 0)
    mask_tile(i, True)

    for h in range(C_HEADS):
        hs = slice(h * HEAD_DIM, (h + 1) * HEAD_DIM)
        qh = qT_ref[0, hs, :]

        def scores(j, bias):
            kt = k_ref[0, pl.ds(pl.multiple_of(j * TILE, TILE), TILE), :]
            s = _dot(kt, qh) + mb_ref[j]
            return s if bias is None else s + bias

        m0 = jnp.full((SUBLANES, TILE), -jnp.inf, F32)
        m = lax.fori_loop(0, n_far, lambda j, m: jnp.maximum(m, _colmax8(scores(j, None))), m0)
        m = lax.fori_loop(0, n_near,
                          lambda d, m: jnp.maximum(m, _colmax8(scores(i - d, bias_ref[h, d]))), m)
        m = jnp.max(m, axis=0, keepdims=True)

        l_ref[...] = jnp.zeros_like(l_ref)
        acc_ref[...] = jnp.zeros_like(acc_ref)

        def accumulate(j, bias):
            e = jnp.exp(scores(j, bias) - m)
            l_ref[...] += _colsum8(e)
            acc_ref[...] += _dot(vT_ref[0, j], e.astype(BF16))

        def acc_far(j, carry):
            accumulate(j, None)
            return carry

        def acc_near(d, carry):
            accumulate(i - d, bias_ref[h, d])
            return carry

        lax.fori_loop(0, n_far, acc_far, 0)
        lax.fori_loop(0, n_near, acc_near, 0)
        oT_ref[hs, :] = acc_ref[...] * (1.0 / jnp.sum(l_ref[...], axis=0, keepdims=True))
    o_ref[0] = oT_ref[...].T.astype(BF16)


def _attn_c(cqT, ck, cvT, iqT, ik, iwT, bias, top_k):
    B, _, S = cqT.shape
    nt = S // TILE
    idx_bits = max(1, (S - 1).bit_length())
    vmem = 2 * (512 * TILE * 2 + 2 * S * LANES * 2 + S * HEAD_DIM * 2 + LANES * TILE * 2 + 8 * TILE * 4
                + bias.size * 4 + TILE * 512 * 2) + 2 * S * TILE * 4 + 24 * TILE * TILE * 4
    return pl.pallas_call(
        functools.partial(_attn_c_kernel, top_k=top_k, idx_bits=idx_bits),
        grid=(B, nt),
        in_specs=[pl.BlockSpec((1, 512, TILE), lambda b, i: (b, 0, i)),
                  pl.BlockSpec((1, S, HEAD_DIM), lambda b, i: (b, 0, 0)),
                  pl.BlockSpec((1, nt, HEAD_DIM, TILE), lambda b, i: (b, 0, 0, 0)),
                  pl.BlockSpec((1, LANES, TILE), lambda b, i: (b, 0, i)),
                  pl.BlockSpec((1, S, IDX_DIM), lambda b, i: (b, 0, 0)),
                  pl.BlockSpec((1, IDX_HEADS, TILE), lambda b, i: (b, 0, i)),
                  pl.BlockSpec(bias.shape, lambda b, i: (0, 0, 0, 0))],
        out_specs=pl.BlockSpec((1, TILE, 512), lambda b, i: (b, i, 0)),
        out_shape=jax.ShapeDtypeStruct((B, S, 512), BF16),
        scratch_shapes=[pltpu.VMEM((nt, TILE, TILE), I32), pltpu.VMEM((nt, TILE, TILE), F32),
                        pltpu.VMEM((SUBLANES, TILE), F32), pltpu.VMEM((HEAD_DIM, TILE), F32),
                        pltpu.VMEM((512, TILE), F32)],
        compiler_params=pltpu.CompilerParams(dimension_semantics=("arbitrary", "arbitrary"),
                                             vmem_limit_bytes=_vmem_limit(vmem)),
        name="attn_c",
    )(cqT, ck, cvT, iqT, ik, iwT, bias)


def _merge_kernel(x_ref, oa_ref, ob_ref, oc_ref, gmix_ref, wg_ref, bg_ref, wb_ref, wo_ref, gffn_ref,
                  wrh_ref, wrl_ref, br_ref, x1_ref, h2_ref, gate_ref):
    x = x_ref[...]
    hb = _rms(x, gmix_ref[...]).astype(BF16)
    z = None
    for n, o_ref in enumerate((oa_ref, ob_ref, oc_ref)):
        cs = slice(n * D_MODEL, (n + 1) * D_MODEL)
        gate = _sigmoid(_dot(hb, wg_ref[:, cs]) + bg_ref[:, cs])
        y = _dot(o_ref[...], wb_ref[n])
        z = gate * y if z is None else z + gate * y
    x1 = x + _dot(z.astype(BF16), wo_ref[...])
    x1_ref[...] = x1
    h2 = _rms(x1, gffn_ref[...])
    h2_ref[...] = h2.astype(BF16)

    hi, lo = _split_bf16(h2)
    lg = _dot(hi, wrh_ref[...]) + _dot(lo, wrh_ref[...]) + _dot(hi, wrl_ref[...]) + br_ref[...]
    col = lax.broadcasted_iota(I32, lg.shape, 1).astype(F32)
    big = float(4 * LANES)
    is_g = (col >= N_EXPERTS) & (col < N_EXPERTS + N_GROUPS)
    gl = jnp.where(is_g, lg, -jnp.inf)
    gmax = jnp.max(gl, axis=-1, keepdims=True)
    p_group = 1.0 / jnp.sum(jnp.exp(gl - gmax), axis=-1, keepdims=True)
    g_sel = jnp.min(jnp.where(gl == gmax, col, big), axis=-1, keepdims=True) - N_EXPERTS
    in_g = (col >= g_sel * EXPERTS_PER_GROUP) & (col < (g_sel + 1) * EXPERTS_PER_GROUP)
    el = jnp.where(in_g, lg, -jnp.inf)
    e1 = jnp.max(el, axis=-1, keepdims=True)
    i1 = jnp.min(jnp.where(el == e1, col, big), axis=-1, keepdims=True)
    el2 = jnp.where(col == i1, -jnp.inf, el)
    e2 = jnp.max(el2, axis=-1, keepdims=True)
    i2 = jnp.min(jnp.where(el2 == e2, col, big), axis=-1, keepdims=True)
    t2 = jnp.exp(e2 - e1)
    w1 = p_group / (1.0 + t2)
    w2 = w1 * t2
    gates = jnp.where(col == i1, w1, 0.0) + jnp.where(col == i2, w2, 0.0)
    gate_ref[...] = gates[:, :N_EXPERTS]


def _merge(x2, oa, ob, oc, gmix, wg, bg, wb, wo, gffn, wrh, wrl, br, tm):
    T, D = x2.shape
    full = lambda a: pl.BlockSpec(a.shape, lambda t: (0,) * a.ndim)
    rowb = lambda c: pl.BlockSpec((tm, c), lambda t: (t, 0))
    vmem = 2 * (tm * D * 4 * 2 + 3 * tm * 512 * 2 + tm * D * 2 + wg.size * 2 + wb.size * 2 + wo.size * 2
                + 2 * D * LANES * 2) + 10 * tm * D * 4
    return pl.pallas_call(
        _merge_kernel,
        grid=(T // tm,),
        in_specs=[rowb(D), rowb(512), rowb(512), rowb(512), full(gmix), full(wg), full(bg), full(wb),
                  full(wo), full(gffn), full(wrh), full(wrl), full(br)],
        out_specs=[rowb(D), rowb(D), rowb(N_EXPERTS)],
        out_shape=[jax.ShapeDtypeStruct((T, D), F32), jax.ShapeDtypeStruct((T, D), BF16),
                   jax.ShapeDtypeStruct((T, N_EXPERTS), F32)],
        compiler_params=pltpu.CompilerParams(dimension_semantics=("arbitrary",),
                                             vmem_limit_bytes=_vmem_limit(vmem)),
        name="merge",
    )(x2, oa, ob, oc, gmix, wg, bg, wb, wo, gffn, wrh, wrl, br)


def _moe_kernel(x1_ref, h2_ref, gate_ref, wgu_ref, wd_ref, o_ref):
    e = pl.program_id(1)
    gu = _dot(h2_ref[...], wgu_ref[0])
    g = gu[:, :EXPERT_FF]
    hid = g * _sigmoid(g) * gu[:, EXPERT_FF:] * gate_ref[0]
    y = _dot(hid.astype(BF16), wd_ref[0])

    @pl.when(e == 0)
    def _():
        o_ref[...] = x1_ref[...] + y

    @pl.when(e > 0)
    def _():
        o_ref[...] += y


def _moe(x1, h2, gate_e, wgu, wd, tm):
    T, D = x1.shape
    vmem = 2 * (tm * D * 4 * 2 + tm * D * 2 + tm * LANES * 4 + D * 2 * EXPERT_FF * 2 + EXPERT_FF * D * 2) \
        + 6 * tm * 2 * EXPERT_FF * 4 + 2 * tm * D * 4
    return pl.pallas_call(
        _moe_kernel,
        grid=(T // tm, N_EXPERTS),
        in_specs=[pl.BlockSpec((tm, D), lambda t, e: (t, 0)),
                  pl.BlockSpec((tm, D), lambda t, e: (t, 0)),
                  pl.BlockSpec((1, tm, 1), lambda t, e: (e, t, 0)),
                  pl.BlockSpec((1, D, 2 * EXPERT_FF), lambda t, e: (e, 0, 0)),
                  pl.BlockSpec((1, EXPERT_FF, D), lambda t, e: (e, 0, 0))],
        out_specs=pl.BlockSpec((tm, D), lambda t, e: (t, 0)),
        out_shape=jax.ShapeDtypeStruct((T, D), F32),
        compiler_params=pltpu.CompilerParams(dimension_semantics=("arbitrary", "arbitrary"),
                                             vmem_limit_bytes=_vmem_limit(vmem)),
        name="moe",
    )(x1, h2, gate_e, wgu, wd)


def _t5_bucket_np(rel):
    half = NUM_BUCKETS // 2
    max_exact = half // 2
    n = np.abs(rel)
    n_f = np.maximum(n, 1).astype(np.float32)
    large = max_exact + (np.log(n_f / np.float32(max_exact)) / np.float32(math.log(MAX_DISTANCE / max_exact))
                         * np.float32(half - max_exact)).astype(np.int32)
    large = np.minimum(large, half - 1)
    return np.where(rel > 0, half, 0) + np.where(n < max_exact, n, large)


def _bias_tables(rel_bias):
    kk = np.arange(TILE)[:, None]
    qq = np.arange(TILE)[None, :]
    rel = np.stack([kk - qq - TILE * d for d in range(NEAR_TILES)])
    tab = jnp.transpose(rel_bias[_t5_bucket_np(rel)], (3, 0, 1, 2))
    far = rel_bias[NUM_BUCKETS // 2 - 1]
    qc = qq // CHUNK
    kc = np.stack([(kk // CHUNK) - (TILE // CHUNK) * d for d in range(NEAR_TILES)]) + 0 * qc
    a_ok = jnp.asarray(kc <= qc)
    ta = jnp.where(a_ok[None], tab[:A_HEADS] - far[:A_HEADS, None, None, None], NEG)
    b_ok = jnp.asarray((qc - kc[:2] >= 0) & (qc - kc[:2] <= W_CHUNKS))
    tb = jnp.where(b_ok[None], tab[A_HEADS:A_HEADS + B_HEADS, :2], NEG)
    tb = jnp.concatenate([tb, jnp.full((B_HEADS, 1, TILE, TILE), NEG, F32)], axis=1)
    tc = tab[A_HEADS + B_HEADS:] - far[A_HEADS + B_HEADS:, None, None, None]
    return ta.astype(F32), tb.astype(F32), tc.astype(F32)


def _proj_weight(w_in):
    cols = np.cumsum((0, 512, 512, 512, 512, 128, 128, 512, 64, 64, 128, 32, 4))
    aq, ak, av, bq, bk, bv, cq, ck, cv, iq, ik, iw = [w_in[:, cols[n]:cols[n + 1]] for n in range(12)]
    z = lambda n: jnp.zeros((w_in.shape[0], n), w_in.dtype)
    w = jnp.concatenate([aq, ak, av, bq, cq, bk[:, :64], z(64), bk[:, 64:], z(64), bv, iq,
                         ck, cv, ik, iw, z(2 * LANES - _M_IW - IDX_HEADS)], axis=1)
    return w.astype(BF16)


def kernel(x, rel_bias, norm_mix_g, w_in, qk_norm_g, diff_lambda, diff_subln_g, sinks, w_branch, w_gate, b_gate,
           w_out, norm_ffn_g, w_router_group, b_router_group, w_router_expert, b_router_expert, w_ff_gate,
           w_ff_up, w_ff_down):
    B, S, D = x.shape
    assert D == D_MODEL and S % TILE == 0
    T = B * S
    top_k = min(TOPK_MAX, S // 4)
    tm_proj = 512 if S % 512 == 0 else TILE
    tm_merge = 512 if T % 512 == 0 else TILE
    tm_moe = 1024 if T % 1024 == 0 else TILE

    bias_a, bias_b, bias_c = _bias_tables(rel_bias)
    seg = jnp.asarray(np.kron(np.eye(512 // HEAD_DIM), np.ones((HEAD_DIM, HEAD_DIM))), BF16)
    q_scale = HEAD_DIM ** -0.5

    for l in range(DEPTH):
        lambda_init = 0.8 - 0.6 * math.exp(-0.3 * l)
        qg = qk_norm_g[l]
        tile8 = lambda g: jnp.tile(g, 512 // HEAD_DIM)[None, :]
        gains = (tile8(qg[0, 0]) * q_scale, tile8(qg[0, 1]), tile8(qg[1, 0]) * q_scale, qg[1, 1][None, :],
                 tile8(qg[2, 0]) * q_scale, qg[2, 1][None, :])
        (aqT, ak, avT, bqT, bk, bvT, cqT, ck, cvT, iqT, ik, iwT) = _proj(
            x, norm_mix_g[l][None, :], _proj_weight(w_in[l]), seg, gains, tm_proj)

        oa = _attn_a(aqT, ak, avT, bias_a, diff_lambda[l], diff_subln_g[l][None, :], lambda_init)
        ob = _attn_b(bqT, bk, bvT, bias_b, sinks[l])
        oc = _attn_c(cqT, ck, cvT, iqT, ik, iwT, bias_c, top_k)

        w_r = jnp.concatenate([w_router_expert[l], w_router_group[l],
                               jnp.zeros((D, LANES - N_EXPERTS - N_GROUPS), F32)], axis=1)
        b_r = jnp.concatenate([b_router_expert[l], b_router_group[l],
                               jnp.zeros((LANES - N_EXPERTS - N_GROUPS,), F32)])[None, :]
        wrh = w_r.astype(BF16)
        wrl = (w_r - wrh.astype(F32)).astype(BF16)
        x1, h2, gate = _merge(
            x.reshape(T, D), oa.reshape(T, 512), ob.reshape(T, 512), oc.reshape(T, 512),
            norm_mix_g[l][None, :], w_gate[l].astype(BF16), b_gate[l][None, :], w_branch[l].astype(BF16),
            w_out[l].astype(BF16), norm_ffn_g[l][None, :], wrh, wrl, b_r, tm_merge)

        wgu = jnp.concatenate([w_ff_gate[l], w_ff_up[l]], axis=-1).astype(BF16)
        gate_e = jnp.transpose(gate)[:, :, None]
        x = _moe(x1, h2, gate_e, wgu, w_ff_down[l].astype(BF16), tm_moe).reshape(B, S, D)
    return x
```

```python
import functools
import math

import numpy as np
import jax
import jax.numpy as jnp
from jax import lax
from jax.experimental import pallas as pl
from jax.experimental.pallas import tpu as pltpu

F32 = jnp.float32
BF16 = jnp.bfloat16
I32 = jnp.int32

D_MODEL = 1024
DEPTH = 2
CHUNK = 64
HEAD_DIM = 64
A_HEADS = 4
A_V_DIM = 2 * HEAD_DIM
B_HEADS = 8
B_KV_HEADS = 2
B_GROUP = B_HEADS // B_KV_HEADS
W_CHUNKS = 2
C_HEADS = 8
IDX_HEADS = 4
IDX_DIM = 32
TOPK_MAX = 256
NUM_BUCKETS = 32
MAX_DISTANCE = 1024
N_BRANCH = 3
BRANCH_WIDTH = 512
N_GROUPS = 4
EXPERTS_PER_GROUP = 4
N_EXPERTS = N_GROUPS * EXPERTS_PER_GROUP
EXPERT_FF = 256
EPS = 1e-6
NEG = -1e30
INT_MIN = -(2 ** 31)

LANES = 128
SUBLANES = 8
TILE = 256
NEAR_TILES = 4
BF16_ROWS = 16
A_VT_ROWS = A_V_DIM + BF16_ROWS
C_VT_ROWS = HEAD_DIM + BF16_ROWS
LOG2E = 1.4426950408889634
SKEW = 5
VMEM_CAP = 60000 * 1024

_C_AQ, _C_AK, _C_AV, _C_BQ, _C_CQ = 0, 512, 1024, 1536, 2048
_C_BK0, _C_BK1, _C_BV, _C_IQ, _C_MISC = 2560, 2688, 2816, 2944, 3072
_W_COLS = 3328
_M_CV, _M_IK, _M_IW = 64, 128, 160


def _dot(a, b):
    return jnp.dot(a, b, preferred_element_type=F32)


def _split_bf16(a):
    hi = a.astype(BF16)
    lo = (a - hi.astype(F32)).astype(BF16)
    return hi, lo


def _sigmoid(x):
    return 1.0 / (1.0 + jnp.exp(-x))


def _rms(x, g):
    return x * lax.rsqrt(jnp.mean(x * x, axis=-1, keepdims=True) + EPS) * g


def _colmax8(s):
    r, c = s.shape
    return jnp.max(s.reshape(r // SUBLANES, SUBLANES, c), axis=0)


def _colsum8(s):
    r, c = s.shape
    return jnp.sum(s.reshape(r // SUBLANES, SUBLANES, c), axis=0)


def _vmem_limit(nbytes):
    return int(min(VMEM_CAP, nbytes))


def _proj_kernel(x_ref, g_ref, w_ref, seg_ref, gaq_ref, gak_ref, gbq_ref, gbk_ref, gcq_ref, gck_ref,
                 aqT_ref, ak_ref, avT_ref, bqT_ref, bk_ref, bvT_ref, cqT_ref, ck_ref, cvT_ref,
                 iqT_ref, ik_ref, iwT_ref, *, tm, iw_scale):
    hb = _rms(x_ref[0], g_ref[...]).astype(BF16)
    seg = seg_ref[...]
    n_sub = tm // TILE

    def grp(a, n):
        return _dot(hb, w_ref[:, a:a + n])

    def segnorm(t, g):
        hi, lo = _split_bf16(t * t)
        ssq = _dot(hi, seg) + _dot(lo, seg)
        return t * lax.rsqrt(ssq * (1.0 / HEAD_DIM) + EPS) * g

    ones_rows = (lax.broadcasted_iota(I32, (BF16_ROWS, tm), 0) == 0).astype(F32)

    def put_slabs(ref, tT):
        for s in range(n_sub):
            ref[0, s] = tT[:, s * TILE:(s + 1) * TILE].astype(BF16)

    aqT_ref[0] = segnorm(grp(_C_AQ, 512), gaq_ref[...]).T.astype(BF16)
    ak_ref[0] = segnorm(grp(_C_AK, 512), gak_ref[...]).astype(BF16)
    avT = grp(_C_AV, 512).T
    put_slabs(avT_ref, jnp.concatenate(
        [p for h in range(A_HEADS) for p in (avT[h * A_V_DIM:(h + 1) * A_V_DIM, :], ones_rows)], axis=0))
    bqT_ref[0] = segnorm(grp(_C_BQ, 512), gbq_ref[...]).T.astype(BF16)
    cqT_ref[0] = segnorm(grp(_C_CQ, 512), gcq_ref[...]).T.astype(BF16)
    for g, col in enumerate((_C_BK0, _C_BK1)):
        t = grp(col, LANES)
        ssq = jnp.sum(t * t, axis=-1, keepdims=True)
        t = t * lax.rsqrt(ssq * (1.0 / HEAD_DIM) + EPS)
        bk_ref[0, g] = (t[:, :HEAD_DIM] * gbk_ref[...]).astype(BF16)
    put_slabs(bvT_ref, grp(_C_BV, LANES).T)
    iqT_ref[0] = grp(_C_IQ, LANES).T.astype(BF16)
    misc = grp(_C_MISC, 2 * LANES)
    ck = misc[:, :HEAD_DIM]
    ssq = jnp.sum(ck * ck, axis=-1, keepdims=True)
    ck_ref[0] = (ck * lax.rsqrt(ssq * (1.0 / HEAD_DIM) + EPS) * gck_ref[...]).astype(BF16)
    ik_ref[0] = misc[:, _M_IK:_M_IK + IDX_DIM].astype(BF16)
    miscT = misc.T
    put_slabs(cvT_ref, jnp.concatenate([miscT[_M_CV:_M_CV + HEAD_DIM, :], ones_rows], axis=0))
    iwT_ref[0] = miscT[_M_IW:_M_IW + IDX_HEADS, :] * iw_scale


def _proj(x, g, w, seg, gains, tm):
    B, S, D = x.shape
    nt = S // TILE
    n_sub = tm // TILE
    full = lambda shape: pl.BlockSpec(shape, lambda b, t: (0,) * len(shape))
    out_shape = [
        jax.ShapeDtypeStruct((B, 512, S), BF16),
        jax.ShapeDtypeStruct((B, S, 512), BF16),
        jax.ShapeDtypeStruct((B, nt, A_HEADS * A_VT_ROWS, TILE), BF16),
        jax.ShapeDtypeStruct((B, 512, S), BF16),
        jax.ShapeDtypeStruct((B, B_KV_HEADS, S, HEAD_DIM), BF16),
        jax.ShapeDtypeStruct((B, nt, LANES, TILE), BF16),
        jax.ShapeDtypeStruct((B, 512, S), BF16),
        jax.ShapeDtypeStruct((B, S, HEAD_DIM), BF16),
        jax.ShapeDtypeStruct((B, nt, C_VT_ROWS, TILE), BF16),
        jax.ShapeDtypeStruct((B, LANES, S), BF16),
        jax.ShapeDtypeStruct((B, S, IDX_DIM), BF16),
        jax.ShapeDtypeStruct((B, IDX_HEADS, S), F32),
    ]
    colT = lambda r: pl.BlockSpec((1, r, tm), lambda b, t: (b, 0, t))
    row = lambda c: pl.BlockSpec((1, tm, c), lambda b, t: (b, t, 0))
    slab = lambda r: pl.BlockSpec((1, n_sub, r, TILE), lambda b, t: (b, t, 0, 0))
    out_specs = [colT(512), row(512), slab(A_HEADS * A_VT_ROWS), colT(512),
                 pl.BlockSpec((1, B_KV_HEADS, tm, HEAD_DIM), lambda b, t: (b, 0, t, 0)),
                 slab(LANES), colT(512), row(HEAD_DIM), slab(C_VT_ROWS), colT(LANES), row(IDX_DIM),
                 pl.BlockSpec((1, IDX_HEADS, tm), lambda b, t: (b, 0, t))]
    in_specs = [pl.BlockSpec((1, tm, D), lambda b, t: (b, t, 0)), full((1, D)), full((D, _W_COLS)),
                full((512, 512))] + [full(gn.shape) for gn in gains]
    vmem = 2 * (tm * D * 4 + D * _W_COLS * 2 + 512 * 512 * 2 + tm * 3400 * 2) + 24 * tm * 512 * 4
    return pl.pallas_call(
        functools.partial(_proj_kernel, tm=tm, iw_scale=IDX_HEADS ** -0.5 * IDX_DIM ** -0.5),
        grid=(B, S // tm), in_specs=in_specs, out_specs=out_specs, out_shape=out_shape,
        compiler_params=pltpu.CompilerParams(dimension_semantics=("arbitrary", "arbitrary"),
                                             vmem_limit_bytes=_vmem_limit(vmem)),
        name="proj",
    )(x, g, w, seg, *gains)


def _online_step(s, m_old, vt, acc_ref, ch):
    m_new = jnp.maximum(m_old, jnp.max(_colmax8(s), axis=0, keepdims=True))
    alpha = jnp.exp2(m_old - m_new)
    e = jnp.exp2(s - m_new).astype(BF16)
    acc_ref[ch] = acc_ref[ch] * alpha + _dot(vt, e)
    return m_new


def _skewed_chains(n_chain, scores, vt, ms, acc_ref):
    new = []
    pending = [scores(ch) for ch in range(min(SKEW, n_chain))]
    for ch in range(n_chain):
        s = pending.pop(0)
        if ch + SKEW < n_chain:
            pending.append(scores(ch + SKEW))
        new.append(_online_step(s, ms[ch], vt(ch), acc_ref, ch))
    return tuple(new)


def _attn_a_kernel(qT_ref, k_ref, vT_ref, bias_ref, lam_ref, sub_ref, o_ref, q2_ref, acc_ref, *, lambda_init):
    i = pl.program_id(1)
    n_far = jnp.maximum(i - (NEAR_TILES - 1), 0)
    n_near = jnp.minimum(i, NEAR_TILES - 1) + 1
    lp = lam_ref[...]
    lam = (jnp.exp(jnp.sum(lp[0:1] * lp[1:2], axis=-1, keepdims=True))
           - jnp.exp(jnp.sum(lp[2:3] * lp[3:4], axis=-1, keepdims=True)) + lambda_init)
    row = lax.broadcasted_iota(I32, (2 * HEAD_DIM, TILE), 0)
    n_chain = 2 * A_HEADS

    for h in range(A_HEADS):
        qh = qT_ref[0, h * A_V_DIM:(h + 1) * A_V_DIM, :]
        zero = jnp.zeros_like(qh)
        q2_ref[2 * h] = jnp.where(row < HEAD_DIM, qh, zero)
        q2_ref[2 * h + 1] = jnp.where(row >= HEAD_DIM, qh, zero)
    acc_ref[...] = jnp.zeros_like(acc_ref)

    def tile_step(j, ms, d):
        rows = pl.ds(pl.multiple_of(j * TILE, TILE), TILE)

        def scores(ch):
            h = ch // 2
            s = _dot(k_ref[0, rows, h * A_V_DIM:(h + 1) * A_V_DIM], q2_ref[ch])
            return s if d is None else s + bias_ref[h, d]

        def vt(ch):
            h = ch // 2
            return vT_ref[0, j, h * A_VT_ROWS:(h + 1) * A_VT_ROWS, :]

        return _skewed_chains(n_chain, scores, vt, ms, acc_ref)

    m0 = tuple(jnp.full((1, TILE), -jnp.inf, F32) for _ in range(n_chain))
    ms = lax.fori_loop(0, n_far, lambda j, ms: tile_step(j, ms, None), m0)
    lax.fori_loop(0, n_near, lambda d, ms: tile_step(i - d, ms, d), ms)

    for h in range(A_HEADS):
        a0 = acc_ref[2 * h]
        a1 = acc_ref[2 * h + 1]
        r0 = 1.0 / a0[A_V_DIM:A_V_DIM + 1, :]
        r1 = 1.0 / a1[A_V_DIM:A_V_DIM + 1, :]
        outT = a0[:A_V_DIM, :] * r0 - lam * (a1[:A_V_DIM, :] * r1)
        out = _rms(outT.T, sub_ref[...]) * (1.0 - lambda_init)
        o_ref[0, :, h * A_V_DIM:(h + 1) * A_V_DIM] = out.astype(BF16)


def _attn_a(aqT, ak, avT, bias, lam_par, subln_g, lambda_init):
    B, _, S = aqT.shape
    nt = S // TILE
    vt_rows = A_HEADS * A_VT_ROWS
    vmem = 2 * (512 * TILE * 2 + S * 512 * 2 + S * vt_rows * 2 + bias.size * 4 + TILE * 512 * 2) \
        + 8 * A_V_DIM * TILE * 2 + 8 * A_VT_ROWS * TILE * 4 + 48 * TILE * TILE * 4
    return pl.pallas_call(
        functools.partial(_attn_a_kernel, lambda_init=lambda_init),
        grid=(B, nt),
        in_specs=[pl.BlockSpec((1, 512, TILE), lambda b, i: (b, 0, i)),
                  pl.BlockSpec((1, S, 512), lambda b, i: (b, 0, 0)),
                  pl.BlockSpec((1, nt, vt_rows, TILE), lambda b, i: (b, 0, 0, 0)),
                  pl.BlockSpec(bias.shape, lambda b, i: (0, 0, 0, 0)),
                  pl.BlockSpec((4, HEAD_DIM), lambda b, i: (0, 0)),
                  pl.BlockSpec((1, A_V_DIM), lambda b, i: (0, 0))],
        out_specs=pl.BlockSpec((1, TILE, 512), lambda b, i: (b, i, 0)),
        out_shape=jax.ShapeDtypeStruct((B, S, 512), BF16),
        scratch_shapes=[pltpu.VMEM((2 * A_HEADS, A_V_DIM, TILE), BF16),
                        pltpu.VMEM((2 * A_HEADS, A_VT_ROWS, TILE), F32)],
        compiler_params=pltpu.CompilerParams(dimension_semantics=("arbitrary", "arbitrary"),
                                             vmem_limit_bytes=_vmem_limit(vmem)),
        name="attn_a",
    )(aqT, ak, avT, bias, lam_par, subln_g)


def _attn_b_kernel(sink_ref, qT_ref, k_ref, vT_ref, bias_ref, o_ref, oT_ref):
    i = pl.program_id(1)
    jp = jnp.maximum(i - 1, 0)
    p_idx = jnp.where(i > 0, 1, 2)
    cur = pl.ds(pl.multiple_of(i * TILE, TILE), TILE)
    prev = pl.ds(pl.multiple_of(jp * TILE, TILE), TILE)
    for g in range(B_KV_HEADS):
        kc = k_ref[0, g, cur, :]
        kp = k_ref[0, g, prev, :]
        gs = slice(g * HEAD_DIM, (g + 1) * HEAD_DIM)
        vc = vT_ref[0, i, gs, :]
        vp = vT_ref[0, jp, gs, :]
        for hh in range(B_GROUP):
            h = g * B_GROUP + hh
            hs = slice(h * HEAD_DIM, (h + 1) * HEAD_DIM)
            qh = qT_ref[0, hs, :]
            sc = _dot(kc, qh) + bias_ref[h, 0]
            sp = _dot(kp, qh) + bias_ref[h, p_idx]
            sink = sink_ref[h]
            m = jnp.max(jnp.maximum(_colmax8(sc), _colmax8(sp)), axis=0, keepdims=True)
            m = jnp.maximum(m, sink)
            ec = jnp.exp(sc - m)
            ep = jnp.exp(sp - m)
            den = jnp.sum(_colsum8(ec) + _colsum8(ep), axis=0, keepdims=True) + jnp.exp(sink - m)
            outT = _dot(vc, ec.astype(BF16)) + _dot(vp, ep.astype(BF16))
            oT_ref[hs, :] = outT * (1.0 / den)
    o_ref[0] = oT_ref[...].T.astype(BF16)


def _attn_b(bqT, bk, bvT, bias, sinks):
    B, _, S = bqT.shape
    nt = S // TILE
    vmem = 2 * (512 * TILE * 2 + 2 * S * LANES * 2 + S * LANES * 2 + bias.size * 4 + TILE * 512 * 2) \
        + 24 * TILE * TILE * 4
    return pl.pallas_call(
        _attn_b_kernel,
        grid=(B, nt),
        in_specs=[pl.BlockSpec(memory_space=pltpu.SMEM),
                  pl.BlockSpec((1, 512, TILE), lambda b, i: (b, 0, i)),
                  pl.BlockSpec((1, B_KV_HEADS, S, HEAD_DIM), lambda b, i: (b, 0, 0, 0)),
                  pl.BlockSpec((1, nt, LANES, TILE), lambda b, i: (b, 0, 0, 0)),
                  pl.BlockSpec(bias.shape, lambda b, i: (0, 0, 0, 0))],
        out_specs=pl.BlockSpec((1, TILE, 512), lambda b, i: (b, i, 0)),
        out_shape=jax.ShapeDtypeStruct((B, S, 512), BF16),
        scratch_shapes=[pltpu.VMEM((512, TILE), F32)],
        compiler_params=pltpu.CompilerParams(dimension_semantics=("arbitrary", "arbitrary"),
                                             vmem_limit_bytes=_vmem_limit(vmem)),
        name="attn_b",
    )(sinks, bqT, bk, bvT, bias)


def _attn_c_kernel(qT_ref, k_ref, vT_ref, iqT_ref, ik_ref, iwT_ref, bias_ref, o_ref,
                   keys_ref, mb_ref, acc_ref, oT_ref, *, top_k, idx_bits):
    i = pl.program_id(1)
    n_t = i + 1
    n_far = jnp.maximum(i - (NEAR_TILES - 1), 0)
    n_near = jnp.minimum(i, NEAR_TILES - 1) + 1
    krow = lax.broadcasted_iota(I32, (TILE, TILE), 0)
    qcol = lax.broadcasted_iota(I32, (TILE, TILE), 1)
    allowed = (krow // CHUNK) <= (qcol // CHUNK)

    def idx_keys(j, diag):
        ikt = ik_ref[0, pl.ds(pl.multiple_of(j * TILE, TILE), TILE), :]
        sc = jnp.zeros((TILE, TILE), F32)
        for hh in range(IDX_HEADS):
            lg = _dot(ikt, iqT_ref[0, hh * IDX_DIM:(hh + 1) * IDX_DIM, :])
            sc = sc + jnp.maximum(lg, 0.0) * iwT_ref[0, hh:hh + 1, :]
        if diag:
            sc = jnp.where(allowed, sc, NEG)
        bits = lax.bitcast_convert_type(sc, I32)
        keys_ref[j] = bits ^ ((bits >> 31) & 0x7FFFFFFF)

    def fill(j, carry):
        idx_keys(j, False)
        return carry

    lax.fori_loop(0, i, fill, 0)
    idx_keys(i, True)

    def count(pred8):
        def body(j, c8):
            kk = keys_ref[j].reshape(TILE // SUBLANES, SUBLANES, TILE)
            return c8 + jnp.sum(pred8(kk, j).astype(I32), axis=0)
        c8 = lax.fori_loop(0, n_t, body, jnp.zeros((SUBLANES, TILE), I32))
        return jnp.sum(c8, axis=0, keepdims=True)

    def bcast8(v):
        return jnp.broadcast_to(v, (SUBLANES, TILE))

    def bit_step(b, t):
        cand = t + lax.shift_left(jnp.int32(1), 31 - b)
        c8 = bcast8(cand)
        cnt = count(lambda kk, j: kk >= c8[None])
        return jnp.where(cnt >= top_k, cand, t)

    t = lax.fori_loop(0, 32, bit_step, jnp.full((1, TILE), INT_MIN, I32))
    t8 = bcast8(t)
    cnt_ge = count(lambda kk, j: kk >= t8[None])
    need_tb = jnp.max(jnp.where(cnt_ge > top_k, 1.0, 0.0))

    sub3 = (lax.broadcasted_iota(I32, (TILE // SUBLANES, SUBLANES, TILE), 0) * SUBLANES
            + lax.broadcasted_iota(I32, (TILE // SUBLANES, SUBLANES, TILE), 1))

    def tie_break():
        cnt_gt = count(lambda kk, j: kk > t8[None])
        r = top_k - cnt_gt

        def idx_step(b, p):
            cand = p + lax.shift_left(jnp.int32(1), idx_bits - 1 - b)
            c8 = bcast8(cand)
            cnt = count(lambda kk, j: (kk == t8[None]) & ((sub3 + j * TILE) < c8[None]))
            return jnp.where(cnt < r, cand, p)

        return lax.fori_loop(0, idx_bits, idx_step, jnp.zeros((1, TILE), I32))

    p_last = lax.cond(need_tb > 0.5, tie_break, lambda: jnp.full((1, TILE), 2 ** idx_bits, I32))

    def mask_tile(j, diag):
        kk = keys_ref[j]
        sel = (kk > t) | ((kk == t) & ((krow + j * TILE) <= p_last))
        if diag:
            sel = sel & allowed
        mb_ref[j] = jnp.where(sel, 0.0, NEG)

    def mask_far(j, carry):
        mask_tile(j, False)
        return carry

    lax.fori_loop(0, i, mask_far, 0)
    mask_tile(i, True)

    acc_ref[...] = jnp.zeros_like(acc_ref)

    def tile_step(j, ms, d):
        kt = k_ref[0, pl.ds(pl.multiple_of(j * TILE, TILE), TILE), :]

        def scores(h):
            s = _dot(kt, qT_ref[0, h * HEAD_DIM:(h + 1) * HEAD_DIM, :]) + mb_ref[j]
            return s if d is None else s + bias_ref[h, d]

        return _skewed_chains(C_HEADS, scores, lambda h: vT_ref[0, j], ms, acc_ref)

    m0 = tuple(jnp.full((1, TILE), -jnp.inf, F32) for _ in range(C_HEADS))
    ms = lax.fori_loop(0, n_far, lambda j, ms: tile_step(j, ms, None), m0)
    lax.fori_loop(0, n_near, lambda d, ms: tile_step(i - d, ms, d), ms)

    for h in range(C_HEADS):
        a = acc_ref[h]
        oT_ref[h * HEAD_DIM:(h + 1) * HEAD_DIM, :] = a[:HEAD_DIM, :] * (1.0 / a[HEAD_DIM:HEAD_DIM + 1, :])
    o_ref[0] = oT_ref[...].T.astype(BF16)


def _attn_c(cqT, ck, cvT, iqT, ik, iwT, bias, top_k):
    B, _, S = cqT.shape
    nt = S // TILE
    idx_bits = max(1, (S - 1).bit_length())
    vmem = 2 * (512 * TILE * 2 + 2 * S * LANES * 2 + S * HEAD_DIM * 2 + LANES * TILE * 2 + 8 * TILE * 4
                + bias.size * 4 + TILE * 512 * 2) + 2 * S * TILE * 4 + 48 * TILE * TILE * 4
    return pl.pallas_call(
        functools.partial(_attn_c_kernel, top_k=top_k, idx_bits=idx_bits),
        grid=(B, nt),
        in_specs=[pl.BlockSpec((1, 512, TILE), lambda b, i: (b, 0, i)),
                  pl.BlockSpec((1, S, HEAD_DIM), lambda b, i: (b, 0, 0)),
                  pl.BlockSpec((1, nt, C_VT_ROWS, TILE), lambda b, i: (b, 0, 0, 0)),
                  pl.BlockSpec((1, LANES, TILE), lambda b, i: (b, 0, i)),
                  pl.BlockSpec((1, S, IDX_DIM), lambda b, i: (b, 0, 0)),
                  pl.BlockSpec((1, IDX_HEADS, TILE), lambda b, i: (b, 0, i)),
                  pl.BlockSpec(bias.shape, lambda b, i: (0, 0, 0, 0))],
        out_specs=pl.BlockSpec((1, TILE, 512), lambda b, i: (b, i, 0)),
        out_shape=jax.ShapeDtypeStruct((B, S, 512), BF16),
        scratch_shapes=[pltpu.VMEM((nt, TILE, TILE), I32), pltpu.VMEM((nt, TILE, TILE), F32),
                        pltpu.VMEM((C_HEADS, C_VT_ROWS, TILE), F32), pltpu.VMEM((512, TILE), F32)],
        compiler_params=pltpu.CompilerParams(dimension_semantics=("arbitrary", "arbitrary"),
                                             vmem_limit_bytes=_vmem_limit(vmem)),
        name="attn_c",
    )(cqT, ck, cvT, iqT, ik, iwT, bias)


def _merge_kernel(x_ref, oa_ref, ob_ref, oc_ref, gmix_ref, wg_ref, bg_ref, wb_ref, wo_ref, gffn_ref,
                  wrh_ref, wrl_ref, br_ref, x1_ref, h2_ref, gate_ref):
    x = x_ref[...]
    hb = _rms(x, gmix_ref[...]).astype(BF16)
    z = None
    for n, o_ref in enumerate((oa_ref, ob_ref, oc_ref)):
        cs = slice(n * D_MODEL, (n + 1) * D_MODEL)
        gate = _sigmoid(_dot(hb, wg_ref[:, cs]) + bg_ref[:, cs])
        y = _dot(o_ref[...], wb_ref[n])
        z = gate * y if z is None else z + gate * y
    x1 = x + _dot(z.astype(BF16), wo_ref[...])
    x1_ref[...] = x1
    h2 = _rms(x1, gffn_ref[...])
    h2_ref[...] = h2.astype(BF16)

    hi, lo = _split_bf16(h2)
    lg = _dot(hi, wrh_ref[...]) + _dot(lo, wrh_ref[...]) + _dot(hi, wrl_ref[...]) + br_ref[...]
    col = lax.broadcasted_iota(I32, lg.shape, 1).astype(F32)
    big = float(4 * LANES)
    is_g = (col >= N_EXPERTS) & (col < N_EXPERTS + N_GROUPS)
    gl = jnp.where(is_g, lg, -jnp.inf)
    gmax = jnp.max(gl, axis=-1, keepdims=True)
    p_group = 1.0 / jnp.sum(jnp.exp(gl - gmax), axis=-1, keepdims=True)
    g_sel = jnp.min(jnp.where(gl == gmax, col, big), axis=-1, keepdims=True) - N_EXPERTS
    in_g = (col >= g_sel * EXPERTS_PER_GROUP) & (col < (g_sel + 1) * EXPERTS_PER_GROUP)
    el = jnp.where(in_g, lg, -jnp.inf)
    e1 = jnp.max(el, axis=-1, keepdims=True)
    i1 = jnp.min(jnp.where(el == e1, col, big), axis=-1, keepdims=True)
    el2 = jnp.where(col == i1, -jnp.inf, el)
    e2 = jnp.max(el2, axis=-1, keepdims=True)
    i2 = jnp.min(jnp.where(el2 == e2, col, big), axis=-1, keepdims=True)
    t2 = jnp.exp(e2 - e1)
    w1 = p_group / (1.0 + t2)
    w2 = w1 * t2
    gates = jnp.where(col == i1, w1, 0.0) + jnp.where(col == i2, w2, 0.0)
    gate_ref[...] = gates[:, :N_EXPERTS]


def _merge(x2, oa, ob, oc, gmix, wg, bg, wb, wo, gffn, wrh, wrl, br, tm):
    T, D = x2.shape
    full = lambda a: pl.BlockSpec(a.shape, lambda t: (0,) * a.ndim)
    rowb = lambda c: pl.BlockSpec((tm, c), lambda t: (t, 0))
    vmem = 2 * (tm * D * 4 * 2 + 3 * tm * 512 * 2 + tm * D * 2 + wg.size * 2 + wb.size * 2 + wo.size * 2
                + 2 * D * LANES * 2) + 10 * tm * D * 4
    return pl.pallas_call(
        _merge_kernel,
        grid=(T // tm,),
        in_specs=[rowb(D), rowb(512), rowb(512), rowb(512), full(gmix), full(wg), full(bg), full(wb),
                  full(wo), full(gffn), full(wrh), full(wrl), full(br)],
        out_specs=[rowb(D), rowb(D), rowb(N_EXPERTS)],
        out_shape=[jax.ShapeDtypeStruct((T, D), F32), jax.ShapeDtypeStruct((T, D), BF16),
                   jax.ShapeDtypeStruct((T, N_EXPERTS), F32)],
        compiler_params=pltpu.CompilerParams(dimension_semantics=("arbitrary",),
                                             vmem_limit_bytes=_vmem_limit(vmem)),
        name="merge",
    )(x2, oa, ob, oc, gmix, wg, bg, wb, wo, gffn, wrh, wrl, br)


def _moe_kernel(x1_ref, h2_ref, gate_ref, wgu_ref, wd_ref, o_ref):
    e = pl.program_id(1)
    gu = _dot(h2_ref[...], wgu_ref[0])
    g = gu[:, :EXPERT_FF]
    hid = g * _sigmoid(g) * gu[:, EXPERT_FF:] * gate_ref[0]
    y = _dot(hid.astype(BF16), wd_ref[0])

    @pl.when(e == 0)
    def _():
        o_ref[...] = x1_ref[...] + y

    @pl.when(e > 0)
    def _():
        o_ref[...] += y


def _moe(x1, h2, gate_e, wgu, wd, tm):
    T, D = x1.shape
    vmem = 2 * (tm * D * 4 * 2 + tm * D * 2 + tm * LANES * 4 + D * 2 * EXPERT_FF * 2 + EXPERT_FF * D * 2) \
        + 6 * tm * 2 * EXPERT_FF * 4 + 2 * tm * D * 4
    return pl.pallas_call(
        _moe_kernel,
        grid=(T // tm, N_EXPERTS),
        in_specs=[pl.BlockSpec((tm, D), lambda t, e: (t, 0)),
                  pl.BlockSpec((tm, D), lambda t, e: (t, 0)),
                  pl.BlockSpec((1, tm, 1), lambda t, e: (e, t, 0)),
                  pl.BlockSpec((1, D, 2 * EXPERT_FF), lambda t, e: (e, 0, 0)),
                  pl.BlockSpec((1, EXPERT_FF, D), lambda t, e: (e, 0, 0))],
        out_specs=pl.BlockSpec((tm, D), lambda t, e: (t, 0)),
        out_shape=jax.ShapeDtypeStruct((T, D), F32),
        compiler_params=pltpu.CompilerParams(dimension_semantics=("arbitrary", "arbitrary"),
                                             vmem_limit_bytes=_vmem_limit(vmem)),
        name="moe",
    )(x1, h2, gate_e, wgu, wd)


def _t5_bucket_np(rel):
    half = NUM_BUCKETS // 2
    max_exact = half // 2
    n = np.abs(rel)
    n_f = np.maximum(n, 1).astype(np.float32)
    large = max_exact + (np.log(n_f / np.float32(max_exact)) / np.float32(math.log(MAX_DISTANCE / max_exact))
                         * np.float32(half - max_exact)).astype(np.int32)
    large = np.minimum(large, half - 1)
    return np.where(rel > 0, half, 0) + np.where(n < max_exact, n, large)


def _bias_tables(rel_bias):
    u = np.arange(2 * TILE)
    off = np.where(u < TILE, u, u - 2 * TILE)
    rel = np.stack([off - TILE * d for d in range(NEAR_TILES)])
    onehot = (_t5_bucket_np(rel)[..., None] == np.arange(NUM_BUCKETS)).astype(np.float32)
    vals = jnp.einsum("dub,bh->hdu", jnp.asarray(onehot), rel_bias, precision=lax.Precision.HIGHEST)
    n_heads = vals.shape[0]
    flat = jnp.tile(vals, (1, 1, TILE))[..., :TILE * (2 * TILE - 1)]
    tab = flat.reshape(n_heads, NEAR_TILES, TILE, 2 * TILE - 1)[..., :TILE]
    tab = jnp.swapaxes(tab, -1, -2)

    far = rel_bias[NUM_BUCKETS // 2 - 1]
    kk = np.arange(TILE)[:, None]
    qc = np.arange(TILE)[None, :] // CHUNK
    kc = np.stack([(kk // CHUNK) - (TILE // CHUNK) * d for d in range(NEAR_TILES)]) + 0 * qc
    a_ok = jnp.asarray(kc <= qc)
    ta = jnp.where(a_ok[None], (tab[:A_HEADS] - far[:A_HEADS, None, None, None]) * LOG2E, NEG)
    b_ok = jnp.asarray((qc - kc[:2] >= 0) & (qc - kc[:2] <= W_CHUNKS))
    tb = jnp.where(b_ok[None], tab[A_HEADS:A_HEADS + B_HEADS, :2], NEG)
    tb = jnp.concatenate([tb, jnp.full((B_HEADS, 1, TILE, TILE), NEG, F32)], axis=1)
    tc = (tab[A_HEADS + B_HEADS:] - far[A_HEADS + B_HEADS:, None, None, None]) * LOG2E
    return ta.astype(F32), tb.astype(F32), tc.astype(F32)


def _proj_weight(w_in):
    cols = np.cumsum((0, 512, 512, 512, 512, 128, 128, 512, 64, 64, 128, 32, 4))
    aq, ak, av, bq, bk, bv, cq, ck, cv, iq, ik, iw = [w_in[:, cols[n]:cols[n + 1]] for n in range(12)]
    z = lambda n: jnp.zeros((w_in.shape[0], n), w_in.dtype)
    w = jnp.concatenate([aq, ak, av, bq, cq, bk[:, :64], z(64), bk[:, 64:], z(64), bv, iq,
                         ck, cv, ik, iw, z(2 * LANES - _M_IW - IDX_HEADS)], axis=1)
    return w.astype(BF16)


def kernel(x, rel_bias, norm_mix_g, w_in, qk_norm_g, diff_lambda, diff_subln_g, sinks, w_branch, w_gate, b_gate,
           w_out, norm_ffn_g, w_router_group, b_router_group, w_router_expert, b_router_expert, w_ff_gate,
           w_ff_up, w_ff_down):
    B, S, D = x.shape
    assert D == D_MODEL and S % TILE == 0
    T = B * S
    top_k = min(TOPK_MAX, S // 4)
    tm_proj = 512 if S % 512 == 0 else TILE
    tm_merge = 512 if T % 512 == 0 else TILE
    tm_moe = 1024 if T % 1024 == 0 else TILE

    bias_a, bias_b, bias_c = _bias_tables(rel_bias)
    seg = jnp.asarray(np.kron(np.eye(512 // HEAD_DIM), np.ones((HEAD_DIM, HEAD_DIM))), BF16)
    q_scale = HEAD_DIM ** -0.5

    for l in range(DEPTH):
        lambda_init = 0.8 - 0.6 * math.exp(-0.3 * l)
        qg = qk_norm_g[l]
        tile8 = lambda g: jnp.tile(g, 512 // HEAD_DIM)[None, :]
        gains = (tile8(qg[0, 0]) * (q_scale * LOG2E), tile8(qg[0, 1]), tile8(qg[1, 0]) * q_scale, qg[1, 1][None, :],
                 tile8(qg[2, 0]) * (q_scale * LOG2E), qg[2, 1][None, :])
        (aqT, ak, avT, bqT, bk, bvT, cqT, ck, cvT, iqT, ik, iwT) = _proj(
            x, norm_mix_g[l][None, :], _proj_weight(w_in[l]), seg, gains, tm_proj)

        oa = _attn_a(aqT, ak, avT, bias_a, diff_lambda[l], diff_subln_g[l][None, :], lambda_init)
        ob = _attn_b(bqT, bk, bvT, bias_b, sinks[l])
        oc = _attn_c(cqT, ck, cvT, iqT, ik, iwT, bias_c, top_k)

        w_r = jnp.concatenate([w_router_expert[l], w_router_group[l],
                               jnp.zeros((D, LANES - N_EXPERTS - N_GROUPS), F32)], axis=1)
        b_r = jnp.concatenate([b_router_expert[l], b_router_group[l],
                               jnp.zeros((LANES - N_EXPERTS - N_GROUPS,), F32)])[None, :]
        wrh = w_r.astype(BF16)
        wrl = (w_r - wrh.astype(F32)).astype(BF16)
        x1, h2, gate = _merge(
            x.reshape(T, D), oa.reshape(T, 512), ob.reshape(T, 512), oc.reshape(T, 512),
            norm_mix_g[l][None, :], w_gate[l].astype(BF16), b_gate[l][None, :], w_branch[l].astype(BF16),
            w_out[l].astype(BF16), norm_ffn_g[l][None, :], wrh, wrl, b_r, tm_merge)

        wgu = jnp.concatenate([w_ff_gate[l], w_ff_up[l]], axis=-1).astype(BF16)
        gate_e = jnp.transpose(gate)[:, :, None]
        x = _moe(x1, h2, gate_e, wgu, w_ff_down[l].astype(BF16), tm_moe).reshape(B, S, D)
    return x
```

```python
import functools
import math

import numpy as np
import jax
import jax.numpy as jnp
from jax import lax
from jax.experimental import pallas as pl
from jax.experimental.pallas import tpu as pltpu

F32 = jnp.float32
BF16 = jnp.bfloat16
I32 = jnp.int32
I16 = jnp.int16

D_MODEL = 1024
DEPTH = 2
CHUNK = 64
HEAD_DIM = 64
A_HEADS = 4
A_V_DIM = 2 * HEAD_DIM
B_HEADS = 8
B_KV_HEADS = 2
B_GROUP = B_HEADS // B_KV_HEADS
W_CHUNKS = 2
C_HEADS = 8
IDX_HEADS = 4
IDX_DIM = 32
TOPK_MAX = 256
NUM_BUCKETS = 32
MAX_DISTANCE = 1024
N_BRANCH = 3
BRANCH_WIDTH = 512
N_GROUPS = 4
EXPERTS_PER_GROUP = 4
N_EXPERTS = N_GROUPS * EXPERTS_PER_GROUP
EXPERT_FF = 256
EPS = 1e-6
NEG = -1e30
I16_MIN = -(2 ** 15)

LANES = 128
SUBLANES = 8
TILE = 256
NEAR_TILES = 4
BF16_ROWS = 16
A_VT_ROWS = A_V_DIM + BF16_ROWS
C_VT_ROWS = HEAD_DIM + BF16_ROWS
LOG2E = 1.4426950408889634
SKEW = 5
VMEM_CAP = 60000 * 1024

_C_AQ, _C_AK, _C_AV, _C_BQ, _C_CQ = 0, 512, 1024, 1536, 2048
_C_BK0, _C_BK1, _C_BV, _C_IQ, _C_MISC = 2560, 2688, 2816, 2944, 3072
_W_COLS = 3328
_M_CV, _M_IK, _M_IW = 64, 128, 160


def _dot(a, b):
    return jnp.dot(a, b, preferred_element_type=F32)


def _split_bf16(a):
    hi = a.astype(BF16)
    lo = (a - hi.astype(F32)).astype(BF16)
    return hi, lo


def _sigmoid(x):
    return 1.0 / (1.0 + jnp.exp(-x))


def _rms(x, g):
    return x * lax.rsqrt(jnp.mean(x * x, axis=-1, keepdims=True) + EPS) * g


def _colmax8(s):
    r, c = s.shape
    return jnp.max(s.reshape(r // SUBLANES, SUBLANES, c), axis=0)


def _colsum8(s):
    r, c = s.shape
    return jnp.sum(s.reshape(r // SUBLANES, SUBLANES, c), axis=0)


def _vmem_limit(nbytes):
    return int(min(VMEM_CAP, nbytes))


def _proj_kernel(x_ref, g_ref, w_ref, seg_ref, gaq_ref, gak_ref, gbq_ref, gbk_ref, gcq_ref, gck_ref,
                 aqT_ref, ak_ref, avT_ref, bqT_ref, bk_ref, bvT_ref, cqT_ref, ck_ref, cvT_ref,
                 iqT_ref, ik_ref, iwT_ref, *, tm, iw_scale):
    hb = _rms(x_ref[0], g_ref[...]).astype(BF16)
    seg = seg_ref[...]
    n_sub = tm // TILE

    def grp(a, n):
        return _dot(hb, w_ref[:, a:a + n])

    def segnorm(t, g):
        ssq = _dot((t * t).astype(BF16), seg)
        return t * lax.rsqrt(ssq * (1.0 / HEAD_DIM) + EPS) * g

    def segnorm_t(t, gcol):
        n = t.shape[0] // HEAD_DIM
        t3 = t.reshape(n, HEAD_DIM, tm)
        ssq = jnp.sum(t3 * t3, axis=1, keepdims=True)
        return (t3 * lax.rsqrt(ssq * (1.0 / HEAD_DIM) + EPS)).reshape(t.shape) * gcol

    ones_rows = (lax.broadcasted_iota(I32, (BF16_ROWS, tm), 0) == 0).astype(F32)

    def put_slabs(ref, tT):
        for s in range(n_sub):
            ref[0, s] = tT[:, s * TILE:(s + 1) * TILE].astype(BF16)

    aqT_ref[0] = segnorm_t(grp(_C_AQ, 512).T, gaq_ref[...]).astype(BF16)
    ak_ref[0] = segnorm(grp(_C_AK, 512), gak_ref[...]).astype(BF16)
    avT = grp(_C_AV, 512).T
    put_slabs(avT_ref, jnp.concatenate(
        [p for h in range(A_HEADS) for p in (avT[h * A_V_DIM:(h + 1) * A_V_DIM, :], ones_rows)], axis=0))
    bqT_ref[0] = segnorm_t(grp(_C_BQ, 512).T, gbq_ref[...]).astype(BF16)
    cqT_ref[0] = segnorm_t(grp(_C_CQ, 512).T, gcq_ref[...]).astype(BF16)
    for g, col in enumerate((_C_BK0, _C_BK1)):
        t = grp(col, LANES)
        ssq = jnp.sum(t * t, axis=-1, keepdims=True)
        t = t * lax.rsqrt(ssq * (1.0 / HEAD_DIM) + EPS)
        bk_ref[0, g] = (t[:, :HEAD_DIM] * gbk_ref[...]).astype(BF16)
    put_slabs(bvT_ref, grp(_C_BV, LANES).T)
    iqT_ref[0] = grp(_C_IQ, LANES).T.astype(BF16)
    misc = grp(_C_MISC, 2 * LANES)
    ck = misc[:, :HEAD_DIM]
    ssq = jnp.sum(ck * ck, axis=-1, keepdims=True)
    ck_ref[0] = (ck * lax.rsqrt(ssq * (1.0 / HEAD_DIM) + EPS) * gck_ref[...]).astype(BF16)
    ik_ref[0] = misc[:, _M_IK:_M_IK + IDX_DIM].astype(BF16)
    miscT = misc.T
    put_slabs(cvT_ref, jnp.concatenate([miscT[_M_CV:_M_CV + HEAD_DIM, :], ones_rows], axis=0))
    iwT_ref[0] = miscT[_M_IW:_M_IW + IDX_HEADS, :] * iw_scale


def _proj(x, g, w, seg, gains, tm):
    B, S, D = x.shape
    nt = S // TILE
    n_sub = tm // TILE
    full = lambda shape: pl.BlockSpec(shape, lambda b, t: (0,) * len(shape))
    out_shape = [
        jax.ShapeDtypeStruct((B, 512, S), BF16),
        jax.ShapeDtypeStruct((B, S, 512), BF16),
        jax.ShapeDtypeStruct((B, nt, A_HEADS * A_VT_ROWS, TILE), BF16),
        jax.ShapeDtypeStruct((B, 512, S), BF16),
        jax.ShapeDtypeStruct((B, B_KV_HEADS, S, HEAD_DIM), BF16),
        jax.ShapeDtypeStruct((B, nt, LANES, TILE), BF16),
        jax.ShapeDtypeStruct((B, 512, S), BF16),
        jax.ShapeDtypeStruct((B, S, HEAD_DIM), BF16),
        jax.ShapeDtypeStruct((B, nt, C_VT_ROWS, TILE), BF16),
        jax.ShapeDtypeStruct((B, LANES, S), BF16),
        jax.ShapeDtypeStruct((B, S, IDX_DIM), BF16),
        jax.ShapeDtypeStruct((B, IDX_HEADS, S), F32),
    ]
    colT = lambda r: pl.BlockSpec((1, r, tm), lambda b, t: (b, 0, t))
    row = lambda c: pl.BlockSpec((1, tm, c), lambda b, t: (b, t, 0))
    slab = lambda r: pl.BlockSpec((1, n_sub, r, TILE), lambda b, t: (b, t, 0, 0))
    out_specs = [colT(512), row(512), slab(A_HEADS * A_VT_ROWS), colT(512),
                 pl.BlockSpec((1, B_KV_HEADS, tm, HEAD_DIM), lambda b, t: (b, 0, t, 0)),
                 slab(LANES), colT(512), row(HEAD_DIM), slab(C_VT_ROWS), colT(LANES), row(IDX_DIM),
                 pl.BlockSpec((1, IDX_HEADS, tm), lambda b, t: (b, 0, t))]
    in_specs = [pl.BlockSpec((1, tm, D), lambda b, t: (b, t, 0)), full((1, D)), full((D, _W_COLS)),
                full((512, 512))] + [full(gn.shape) for gn in gains]
    vmem = 2 * (tm * D * 4 + D * _W_COLS * 2 + 512 * 512 * 2 + tm * 3400 * 2) + 24 * tm * 512 * 4
    return pl.pallas_call(
        functools.partial(_proj_kernel, tm=tm, iw_scale=IDX_HEADS ** -0.5 * IDX_DIM ** -0.5),
        grid=(B, S // tm), in_specs=in_specs, out_specs=out_specs, out_shape=out_shape,
        compiler_params=pltpu.CompilerParams(dimension_semantics=("arbitrary", "arbitrary"),
                                             vmem_limit_bytes=_vmem_limit(vmem)),
        name="proj",
    )(x, g, w, seg, *gains)


def _online_step(s, m_old, vt, acc_ref, ch):
    m_new = jnp.maximum(m_old, jnp.max(_colmax8(s), axis=0, keepdims=True))
    alpha = jnp.exp2(m_old - m_new)
    e = jnp.exp2(s - m_new).astype(BF16)
    acc_ref[ch] = acc_ref[ch] * alpha + _dot(vt, e)
    return m_new


def _skewed_chains(n_chain, scores, vt, ms, acc_ref):
    new = []
    pending = [scores(ch) for ch in range(min(SKEW, n_chain))]
    for ch in range(n_chain):
        s = pending.pop(0)
        if ch + SKEW < n_chain:
            pending.append(scores(ch + SKEW))
        new.append(_online_step(s, ms[ch], vt(ch), acc_ref, ch))
    return tuple(new)


def _attn_a_kernel(qT_ref, k_ref, vT_ref, bias_ref, lam_ref, sub_ref, o_ref, q2_ref, acc_ref, *, lambda_init):
    i = pl.program_id(1)
    n_far = jnp.maximum(i - (NEAR_TILES - 1), 0)
    n_near = jnp.minimum(i, NEAR_TILES - 1) + 1
    lp = lam_ref[...]
    lam = (jnp.exp(jnp.sum(lp[0:1] * lp[1:2], axis=-1, keepdims=True))
           - jnp.exp(jnp.sum(lp[2:3] * lp[3:4], axis=-1, keepdims=True)) + lambda_init)
    row = lax.broadcasted_iota(I32, (2 * HEAD_DIM, TILE), 0)
    n_chain = 2 * A_HEADS

    for h in range(A_HEADS):
        qh = qT_ref[0, h * A_V_DIM:(h + 1) * A_V_DIM, :]
        zero = jnp.zeros_like(qh)
        q2_ref[2 * h] = jnp.where(row < HEAD_DIM, qh, zero)
        q2_ref[2 * h + 1] = jnp.where(row >= HEAD_DIM, qh, zero)
    acc_ref[...] = jnp.zeros_like(acc_ref)

    def tile_step(j, ms, d):
        rows = pl.ds(pl.multiple_of(j * TILE, TILE), TILE)

        def scores(ch):
            h = ch // 2
            s = _dot(k_ref[0, rows, h * A_V_DIM:(h + 1) * A_V_DIM], q2_ref[ch])
            return s if d is None else s + bias_ref[h, d]

        def vt(ch):
            h = ch // 2
            return vT_ref[0, j, h * A_VT_ROWS:(h + 1) * A_VT_ROWS, :]

        return _skewed_chains(n_chain, scores, vt, ms, acc_ref)

    m0 = tuple(jnp.full((1, TILE), -jnp.inf, F32) for _ in range(n_chain))
    ms = lax.fori_loop(0, n_far, lambda j, ms: tile_step(j, ms, None), m0)
    lax.fori_loop(0, n_near, lambda d, ms: tile_step(i - d, ms, d), ms)

    for h in range(A_HEADS):
        a0 = acc_ref[2 * h]
        a1 = acc_ref[2 * h + 1]
        r0 = 1.0 / a0[A_V_DIM:A_V_DIM + 1, :]
        r1 = 1.0 / a1[A_V_DIM:A_V_DIM + 1, :]
        outT = a0[:A_V_DIM, :] * r0 - lam * (a1[:A_V_DIM, :] * r1)
        out = _rms(outT.T, sub_ref[...]) * (1.0 - lambda_init)
        o_ref[0, :, h * A_V_DIM:(h + 1) * A_V_DIM] = out.astype(BF16)


def _attn_a(aqT, ak, avT, bias, lam_par, subln_g, lambda_init):
    B, _, S = aqT.shape
    nt = S // TILE
    vt_rows = A_HEADS * A_VT_ROWS
    vmem = 2 * (512 * TILE * 2 + S * 512 * 2 + S * vt_rows * 2 + bias.size * 4 + TILE * 512 * 2) \
        + 8 * A_V_DIM * TILE * 2 + 8 * A_VT_ROWS * TILE * 4 + 48 * TILE * TILE * 4
    return pl.pallas_call(
        functools.partial(_attn_a_kernel, lambda_init=lambda_init),
        grid=(B, nt),
        in_specs=[pl.BlockSpec((1, 512, TILE), lambda b, i: (b, 0, i)),
                  pl.BlockSpec((1, S, 512), lambda b, i: (b, 0, 0)),
                  pl.BlockSpec((1, nt, vt_rows, TILE), lambda b, i: (b, 0, 0, 0)),
                  pl.BlockSpec(bias.shape, lambda b, i: (0, 0, 0, 0)),
                  pl.BlockSpec((4, HEAD_DIM), lambda b, i: (0, 0)),
                  pl.BlockSpec((1, A_V_DIM), lambda b, i: (0, 0))],
        out_specs=pl.BlockSpec((1, TILE, 512), lambda b, i: (b, i, 0)),
        out_shape=jax.ShapeDtypeStruct((B, S, 512), BF16),
        scratch_shapes=[pltpu.VMEM((2 * A_HEADS, A_V_DIM, TILE), BF16),
                        pltpu.VMEM((2 * A_HEADS, A_VT_ROWS, TILE), F32)],
        compiler_params=pltpu.CompilerParams(dimension_semantics=("arbitrary", "arbitrary"),
                                             vmem_limit_bytes=_vmem_limit(vmem)),
        name="attn_a",
    )(aqT, ak, avT, bias, lam_par, subln_g)


def _attn_b_kernel(sink_ref, qT_ref, k_ref, vT_ref, bias_ref, o_ref, oT_ref):
    i = pl.program_id(1)
    jp = jnp.maximum(i - 1, 0)
    p_idx = jnp.where(i > 0, 1, 2)
    cur = pl.ds(pl.multiple_of(i * TILE, TILE), TILE)
    prev = pl.ds(pl.multiple_of(jp * TILE, TILE), TILE)

    def scores(h):
        g = h // B_GROUP
        qh = qT_ref[0, h * HEAD_DIM:(h + 1) * HEAD_DIM, :]
        return (_dot(k_ref[0, g, cur, :], qh) + bias_ref[h, 0], _dot(k_ref[0, g, prev, :], qh) + bias_ref[h, p_idx])

    def finish(h, s):
        sc, sp = s
        gs = slice((h // B_GROUP) * HEAD_DIM, (h // B_GROUP + 1) * HEAD_DIM)
        sink = sink_ref[h]
        m = jnp.max(jnp.maximum(_colmax8(sc), _colmax8(sp)), axis=0, keepdims=True)
        m = jnp.maximum(m, sink)
        ec = jnp.exp(sc - m)
        ep = jnp.exp(sp - m)
        den = jnp.sum(_colsum8(ec) + _colsum8(ep), axis=0, keepdims=True) + jnp.exp(sink - m)
        outT = _dot(vT_ref[0, i, gs, :], ec.astype(BF16)) + _dot(vT_ref[0, jp, gs, :], ep.astype(BF16))
        oT_ref[h * HEAD_DIM:(h + 1) * HEAD_DIM, :] = outT * (1.0 / den)

    pending = [scores(h) for h in range(SKEW)]
    for h in range(B_HEADS):
        s = pending.pop(0)
        if h + SKEW < B_HEADS:
            pending.append(scores(h + SKEW))
        finish(h, s)
    o_ref[0] = oT_ref[...].T.astype(BF16)


def _attn_b(bqT, bk, bvT, bias, sinks):
    B, _, S = bqT.shape
    nt = S // TILE
    vmem = 2 * (512 * TILE * 2 + 2 * S * LANES * 2 + S * LANES * 2 + bias.size * 4 + TILE * 512 * 2) \
        + 24 * TILE * TILE * 4
    return pl.pallas_call(
        _attn_b_kernel,
        grid=(B, nt),
        in_specs=[pl.BlockSpec(memory_space=pltpu.SMEM),
                  pl.BlockSpec((1, 512, TILE), lambda b, i: (b, 0, i)),
                  pl.BlockSpec((1, B_KV_HEADS, S, HEAD_DIM), lambda b, i: (b, 0, 0, 0)),
                  pl.BlockSpec((1, nt, LANES, TILE), lambda b, i: (b, 0, 0, 0)),
                  pl.BlockSpec(bias.shape, lambda b, i: (0, 0, 0, 0))],
        out_specs=pl.BlockSpec((1, TILE, 512), lambda b, i: (b, i, 0)),
        out_shape=jax.ShapeDtypeStruct((B, S, 512), BF16),
        scratch_shapes=[pltpu.VMEM((512, TILE), F32)],
        compiler_params=pltpu.CompilerParams(dimension_semantics=("arbitrary", "arbitrary"),
                                             vmem_limit_bytes=_vmem_limit(vmem)),
        name="attn_b",
    )(sinks, bqT, bk, bvT, bias)


def _attn_c_kernel(qT_ref, k_ref, vT_ref, iqT_ref, ik_ref, iwT_ref, bias_ref, o_ref,
                   hi_ref, lo_ref, mb_ref, acc_ref, oT_ref, *, top_k, idx_bits):
    i = pl.program_id(1)
    n_t = i + 1
    n_far = jnp.maximum(i - (NEAR_TILES - 1), 0)
    n_near = jnp.minimum(i, NEAR_TILES - 1) + 1
    krow = lax.broadcasted_iota(I32, (TILE, TILE), 0)
    qcol = lax.broadcasted_iota(I32, (TILE, TILE), 1)
    allowed = (krow // CHUNK) <= (qcol // CHUNK)

    def idx_keys(j, diag):
        ikt = ik_ref[0, pl.ds(pl.multiple_of(j * TILE, TILE), TILE), :]
        sc = jnp.zeros((TILE, TILE), F32)
        for hh in range(IDX_HEADS):
            lg = _dot(ikt, iqT_ref[0, hh * IDX_DIM:(hh + 1) * IDX_DIM, :])
            sc = sc + jnp.maximum(lg, 0.0) * iwT_ref[0, hh:hh + 1, :]
        if diag:
            sc = jnp.where(allowed, sc, NEG)
        bits = lax.bitcast_convert_type(sc, I32)
        key = bits ^ ((bits >> 31) & 0x7FFFFFFF)
        hi_ref[j] = (key >> 16).astype(I16)
        lo_ref[j] = ((key & 0xFFFF) + I16_MIN).astype(I16)

    def fill(j, carry):
        idx_keys(j, False)
        return carry

    lax.fori_loop(0, i, fill, 0)
    idx_keys(i, True)

    n_pair = (n_t + 1) // 2

    @pl.when(n_t % 2 == 1)
    def _():
        hi_ref[n_t] = jnp.full((TILE, TILE), I16_MIN, I16)
        lo_ref[n_t] = jnp.full((TILE, TILE), I16_MIN, I16)

    groups = TILE // BF16_ROWS

    def rows16(ref, j):
        return ref[j].reshape(groups, BF16_ROWS, TILE)

    def bcast16(v):
        return jnp.broadcast_to(v, (BF16_ROWS, TILE)).astype(I16)

    def count(pred):
        def body(p, c):
            for u in (0, 1):
                j = 2 * p + u
                hi, lo = rows16(hi_ref, j), rows16(lo_ref, j)
                for r in range(groups):
                    c = c + pred(hi[r], lo[r], j, r).astype(I16)
            return c
        c = lax.fori_loop(0, n_pair, body, jnp.zeros((BF16_ROWS, TILE), I16))
        return jnp.sum(c.astype(I32), axis=0, keepdims=True)

    def search(n_bits, accept):
        def step(b, t):
            cand = t + lax.shift_left(jnp.int32(1), n_bits - 1 - b)
            return jnp.where(accept(cand), cand, t)
        return lax.fori_loop(0, n_bits, step, jnp.full((1, TILE), I16_MIN, I32))

    def hi_accept(cand):
        c16 = bcast16(cand)
        return count(lambda hi, lo, j, r: hi >= c16) >= top_k

    t_hi = search(16, hi_accept)
    th16 = bcast16(t_hi)
    cnt_above = count(lambda hi, lo, j, r: hi > th16)
    cnt_bucket = count(lambda hi, lo, j, r: hi == th16)
    r_lo = top_k - cnt_above

    def bucket_only(p, carry):
        for u in (0, 1):
            j = 2 * p + u
            hi, lo = rows16(hi_ref, j), rows16(lo_ref, j)
            for r in range(groups):
                lo_ref[j, r * BF16_ROWS:(r + 1) * BF16_ROWS, :] = jnp.where(hi[r] == th16, lo[r], I16_MIN)
        return carry

    lax.fori_loop(0, n_pair, bucket_only, 0)

    def lo_accept(cand):
        c16 = bcast16(cand)
        return count(lambda hi, lo, j, r: lo >= c16) >= r_lo

    t_lo = search(16, lo_accept)
    tl16 = bcast16(t_lo)
    cnt_gt = count(lambda hi, lo, j, r: lo > tl16)
    cnt_ge = jnp.where(t_lo == I16_MIN, cnt_bucket, count(lambda hi, lo, j, r: lo >= tl16))
    r_eq = r_lo - cnt_gt
    need_tb = jnp.max(jnp.where(cnt_ge - cnt_gt > r_eq, 1.0, 0.0))

    sub16 = lax.broadcasted_iota(I32, (BF16_ROWS, TILE), 0).astype(I16)

    def local16(v, j, r):
        return bcast16(jnp.clip(v - j * TILE - r * BF16_ROWS, I16_MIN, -I16_MIN - 1))

    def tie_break():
        def idx_accept(cand):
            cnt = count(lambda hi, lo, j, r: (lo == tl16) & (hi == th16) & (sub16 < local16(cand, j, r)))
            return cnt < r_eq
        def step(b, p):
            cand = p + lax.shift_left(jnp.int32(1), idx_bits - 1 - b)
            return jnp.where(idx_accept(cand), cand, p)
        return lax.fori_loop(0, idx_bits, step, jnp.zeros((1, TILE), I32))

    p_last = lax.cond(need_tb > 0.5, tie_break, lambda: jnp.full((1, TILE), 2 ** idx_bits, I32))

    def mask_tile(j, diag):
        hi, lo = rows16(hi_ref, j), rows16(lo_ref, j)
        for r in range(groups):
            tie = (lo[r] == tl16) & (hi[r] == th16) & (sub16 <= local16(p_last, j, r))
            sel = (hi[r] > th16) | (lo[r] > tl16) | tie
            rs = slice(r * BF16_ROWS, (r + 1) * BF16_ROWS)
            m16 = jnp.where(sel, jnp.zeros((), BF16), jnp.full((), NEG, BF16)).astype(F32)
            if diag:
                m16 = jnp.where(allowed[rs, :], m16, NEG)
            mb_ref[j, rs, :] = m16

    def mask_far(j, carry):
        mask_tile(j, False)
        return carry

    lax.fori_loop(0, i, mask_far, 0)
    mask_tile(i, True)

    acc_ref[...] = jnp.zeros_like(acc_ref)

    def tile_step(j, ms, d):
        kt = k_ref[0, pl.ds(pl.multiple_of(j * TILE, TILE), TILE), :]

        def scores(h):
            s = _dot(kt, qT_ref[0, h * HEAD_DIM:(h + 1) * HEAD_DIM, :]) + mb_ref[j]
            return s if d is None else s + bias_ref[h, d]

        return _skewed_chains(C_HEADS, scores, lambda h: vT_ref[0, j], ms, acc_ref)

    m0 = tuple(jnp.full((1, TILE), -jnp.inf, F32) for _ in range(C_HEADS))
    ms = lax.fori_loop(0, n_far, lambda j, ms: tile_step(j, ms, None), m0)
    lax.fori_loop(0, n_near, lambda d, ms: tile_step(i - d, ms, d), ms)

    for h in range(C_HEADS):
        a = acc_ref[h]
        oT_ref[h * HEAD_DIM:(h + 1) * HEAD_DIM, :] = a[:HEAD_DIM, :] * (1.0 / a[HEAD_DIM:HEAD_DIM + 1, :])
    o_ref[0] = oT_ref[...].T.astype(BF16)


def _attn_c(cqT, ck, cvT, iqT, ik, iwT, bias, top_k):
    B, _, S = cqT.shape
    nt = S // TILE
    idx_bits = max(1, (S - 1).bit_length())
    vmem = 2 * (512 * TILE * 2 + 2 * S * LANES * 2 + S * HEAD_DIM * 2 + LANES * TILE * 2 + 8 * TILE * 4
                + bias.size * 4 + TILE * 512 * 2) + 2 * S * TILE * 4 + 48 * TILE * TILE * 4
    return pl.pallas_call(
        functools.partial(_attn_c_kernel, top_k=top_k, idx_bits=idx_bits),
        grid=(B, nt),
        in_specs=[pl.BlockSpec((1, 512, TILE), lambda b, i: (b, 0, i)),
                  pl.BlockSpec((1, S, HEAD_DIM), lambda b, i: (b, 0, 0)),
                  pl.BlockSpec((1, nt, C_VT_ROWS, TILE), lambda b, i: (b, 0, 0, 0)),
                  pl.BlockSpec((1, LANES, TILE), lambda b, i: (b, 0, i)),
                  pl.BlockSpec((1, S, IDX_DIM), lambda b, i: (b, 0, 0)),
                  pl.BlockSpec((1, IDX_HEADS, TILE), lambda b, i: (b, 0, i)),
                  pl.BlockSpec(bias.shape, lambda b, i: (0, 0, 0, 0))],
        out_specs=pl.BlockSpec((1, TILE, 512), lambda b, i: (b, i, 0)),
        out_shape=jax.ShapeDtypeStruct((B, S, 512), BF16),
        scratch_shapes=[pltpu.VMEM((nt + nt % 2, TILE, TILE), I16), pltpu.VMEM((nt + nt % 2, TILE, TILE), I16),
                        pltpu.VMEM((nt, TILE, TILE), F32),
                        pltpu.VMEM((C_HEADS, C_VT_ROWS, TILE), F32), pltpu.VMEM((512, TILE), F32)],
        compiler_params=pltpu.CompilerParams(dimension_semantics=("arbitrary", "arbitrary"),
                                             vmem_limit_bytes=_vmem_limit(vmem)),
        name="attn_c",
    )(cqT, ck, cvT, iqT, ik, iwT, bias)


def _merge_kernel(x_ref, oa_ref, ob_ref, oc_ref, gmix_ref, wg_ref, bg_ref, wb_ref, wo_ref, gffn_ref,
                  wrh_ref, wrl_ref, br_ref, x1_ref, h2_ref, gate_ref):
    x = x_ref[...]
    hb = _rms(x, gmix_ref[...]).astype(BF16)
    z = None
    for n, o_ref in enumerate((oa_ref, ob_ref, oc_ref)):
        cs = slice(n * D_MODEL, (n + 1) * D_MODEL)
        gate = _sigmoid(_dot(hb, wg_ref[:, cs]) + bg_ref[:, cs])
        y = _dot(o_ref[...], wb_ref[n])
        z = gate * y if z is None else z + gate * y
    x1 = x + _dot(z.astype(BF16), wo_ref[...])
    x1_ref[...] = x1
    h2 = _rms(x1, gffn_ref[...])
    h2_ref[...] = h2.astype(BF16)

    hi, lo = _split_bf16(h2)
    lg = _dot(hi, wrh_ref[...]) + _dot(lo, wrh_ref[...]) + _dot(hi, wrl_ref[...]) + br_ref[...]
    col = lax.broadcasted_iota(I32, lg.shape, 1).astype(F32)
    big = float(4 * LANES)
    is_g = (col >= N_EXPERTS) & (col < N_EXPERTS + N_GROUPS)
    gl = jnp.where(is_g, lg, -jnp.inf)
    gmax = jnp.max(gl, axis=-1, keepdims=True)
    p_group = 1.0 / jnp.sum(jnp.exp(gl - gmax), axis=-1, keepdims=True)
    g_sel = jnp.min(jnp.where(gl == gmax, col, big), axis=-1, keepdims=True) - N_EXPERTS
    in_g = (col >= g_sel * EXPERTS_PER_GROUP) & (col < (g_sel + 1) * EXPERTS_PER_GROUP)
    el = jnp.where(in_g, lg, -jnp.inf)
    e1 = jnp.max(el, axis=-1, keepdims=True)
    i1 = jnp.min(jnp.where(el == e1, col, big), axis=-1, keepdims=True)
    el2 = jnp.where(col == i1, -jnp.inf, el)
    e2 = jnp.max(el2, axis=-1, keepdims=True)
    i2 = jnp.min(jnp.where(el2 == e2, col, big), axis=-1, keepdims=True)
    t2 = jnp.exp(e2 - e1)
    w1 = p_group / (1.0 + t2)
    w2 = w1 * t2
    gates = jnp.where(col == i1, w1, 0.0) + jnp.where(col == i2, w2, 0.0)
    gate_ref[...] = gates[:, :N_EXPERTS]


def _merge(x2, oa, ob, oc, gmix, wg, bg, wb, wo, gffn, wrh, wrl, br, tm):
    T, D = x2.shape
    full = lambda a: pl.BlockSpec(a.shape, lambda t: (0,) * a.ndim)
    rowb = lambda c: pl.BlockSpec((tm, c), lambda t: (t, 0))
    vmem = 2 * (tm * D * 4 * 2 + 3 * tm * 512 * 2 + tm * D * 2 + wg.size * 2 + wb.size * 2 + wo.size * 2
                + 2 * D * LANES * 2) + 10 * tm * D * 4
    return pl.pallas_call(
        _merge_kernel,
        grid=(T // tm,),
        in_specs=[rowb(D), rowb(512), rowb(512), rowb(512), full(gmix), full(wg), full(bg), full(wb),
                  full(wo), full(gffn), full(wrh), full(wrl), full(br)],
        out_specs=[rowb(D), rowb(D), rowb(N_EXPERTS)],
        out_shape=[jax.ShapeDtypeStruct((T, D), F32), jax.ShapeDtypeStruct((T, D), BF16),
                   jax.ShapeDtypeStruct((T, N_EXPERTS), F32)],
        compiler_params=pltpu.CompilerParams(dimension_semantics=("arbitrary",),
                                             vmem_limit_bytes=_vmem_limit(vmem)),
        name="merge",
    )(x2, oa, ob, oc, gmix, wg, bg, wb, wo, gffn, wrh, wrl, br)


def _moe_kernel(x1_ref, h2_ref, gate_ref, wgu_ref, wd_ref, o_ref):
    g = pl.program_id(1)
    h2 = h2_ref[...]
    gate = gate_ref[...]
    lane = lax.broadcasted_iota(I32, gate.shape, 1)
    hids = []
    for u in range(EXPERTS_PER_GROUP):
        gu = _dot(h2, wgu_ref[u])
        a = gu[:, :EXPERT_FF]
        w = jnp.sum(jnp.where(lane == g * EXPERTS_PER_GROUP + u, gate, 0.0), axis=-1, keepdims=True)
        hids.append((a * _sigmoid(a) * gu[:, EXPERT_FF:] * w).astype(BF16))
    y = _dot(jnp.concatenate(hids, axis=1), wd_ref[...])

    @pl.when(g == 0)
    def _():
        o_ref[...] = x1_ref[...] + y

    @pl.when(g > 0)
    def _():
        o_ref[...] += y


def _moe(x1, h2, gate, wgu, wd, tm):
    T, D = x1.shape
    n_e, ff = EXPERTS_PER_GROUP, EXPERT_FF
    vmem = 2 * (tm * D * 4 * 2 + tm * D * 2 + tm * LANES * 4 + n_e * D * 2 * ff * 2 + n_e * ff * D * 2) \
        + 3 * n_e * tm * 2 * ff * 4 + 2 * tm * D * 4
    return pl.pallas_call(
        _moe_kernel,
        grid=(T // tm, N_GROUPS),
        in_specs=[pl.BlockSpec((tm, D), lambda t, g: (t, 0)),
                  pl.BlockSpec((tm, D), lambda t, g: (t, 0)),
                  pl.BlockSpec((tm, N_EXPERTS), lambda t, g: (t, 0)),
                  pl.BlockSpec((n_e, D, 2 * ff), lambda t, g: (g, 0, 0)),
                  pl.BlockSpec((n_e * ff, D), lambda t, g: (g, 0))],
        out_specs=pl.BlockSpec((tm, D), lambda t, g: (t, 0)),
        out_shape=jax.ShapeDtypeStruct((T, D), F32),
        compiler_params=pltpu.CompilerParams(dimension_semantics=("arbitrary", "arbitrary"),
                                             vmem_limit_bytes=_vmem_limit(vmem)),
        name="moe",
    )(x1, h2, gate, wgu, wd)


def _t5_bucket_np(rel):
    half = NUM_BUCKETS // 2
    max_exact = half // 2
    n = np.abs(rel)
    n_f = np.maximum(n, 1).astype(np.float32)
    large = max_exact + (np.log(n_f / np.float32(max_exact)) / np.float32(math.log(MAX_DISTANCE / max_exact))
                         * np.float32(half - max_exact)).astype(np.int32)
    large = np.minimum(large, half - 1)
    return np.where(rel > 0, half, 0) + np.where(n < max_exact, n, large)


def _bias_tables(rel_bias):
    u = np.arange(2 * TILE)
    off = np.where(u < TILE, u, u - 2 * TILE)
    rel = np.stack([-off - TILE * d for d in range(NEAR_TILES)])
    onehot = (_t5_bucket_np(rel)[..., None] == np.arange(NUM_BUCKETS)).astype(np.float32)
    vals = jnp.einsum("dub,bh->hdu", jnp.asarray(onehot), rel_bias, precision=lax.Precision.HIGHEST)
    n_heads = vals.shape[0]
    flat = jnp.tile(vals, (1, 1, TILE))[..., :TILE * (2 * TILE - 1)]
    tab = flat.reshape(n_heads, NEAR_TILES, TILE, 2 * TILE - 1)[..., :TILE]

    far = rel_bias[NUM_BUCKETS // 2 - 1]
    kk = np.arange(TILE)[:, None]
    qc = np.arange(TILE)[None, :] // CHUNK
    kc = np.stack([(kk // CHUNK) - (TILE // CHUNK) * d for d in range(NEAR_TILES)]) + 0 * qc
    a_ok = jnp.asarray(kc <= qc)
    ta = jnp.where(a_ok[None], (tab[:A_HEADS] - far[:A_HEADS, None, None, None]) * LOG2E, NEG)
    b_ok = jnp.asarray((qc - kc[:2] >= 0) & (qc - kc[:2] <= W_CHUNKS))
    tb = jnp.where(b_ok[None], tab[A_HEADS:A_HEADS + B_HEADS, :2], NEG)
    tb = jnp.concatenate([tb, jnp.full((B_HEADS, 1, TILE, TILE), NEG, F32)], axis=1)
    tc = (tab[A_HEADS + B_HEADS:] - far[A_HEADS + B_HEADS:, None, None, None]) * LOG2E
    return ta.astype(F32), tb.astype(F32), tc.astype(F32)


def _proj_weight(w_in):
    cols = np.cumsum((0, 512, 512, 512, 512, 128, 128, 512, 64, 64, 128, 32, 4))
    aq, ak, av, bq, bk, bv, cq, ck, cv, iq, ik, iw = [w_in[:, cols[n]:cols[n + 1]] for n in range(12)]
    z = lambda n: jnp.zeros((w_in.shape[0], n), w_in.dtype)
    w = jnp.concatenate([aq, ak, av, bq, cq, bk[:, :64], z(64), bk[:, 64:], z(64), bv, iq,
                         ck, cv, ik, iw, z(2 * LANES - _M_IW - IDX_HEADS)], axis=1)
    return w.astype(BF16)


def kernel(x, rel_bias, norm_mix_g, w_in, qk_norm_g, diff_lambda, diff_subln_g, sinks, w_branch, w_gate, b_gate,
           w_out, norm_ffn_g, w_router_group, b_router_group, w_router_expert, b_router_expert, w_ff_gate,
           w_ff_up, w_ff_down):
    B, S, D = x.shape
    assert D == D_MODEL and S % TILE == 0
    T = B * S
    top_k = min(TOPK_MAX, S // 4)
    tm_proj = 512 if S % 512 == 0 else TILE
    tm_merge = 512 if T % 512 == 0 else TILE
    tm_moe = 1024 if T % 1024 == 0 else TILE

    bias_a, bias_b, bias_c = _bias_tables(rel_bias)
    seg = jnp.asarray(np.kron(np.eye(512 // HEAD_DIM), np.ones((HEAD_DIM, HEAD_DIM))), BF16)
    q_scale = HEAD_DIM ** -0.5

    for l in range(DEPTH):
        lambda_init = 0.8 - 0.6 * math.exp(-0.3 * l)
        qg = qk_norm_g[l]
        tile8 = lambda g: jnp.tile(g, 512 // HEAD_DIM)
        gains = ((tile8(qg[0, 0]) * (q_scale * LOG2E))[:, None], tile8(qg[0, 1])[None, :],
                 (tile8(qg[1, 0]) * q_scale)[:, None], qg[1, 1][None, :],
                 (tile8(qg[2, 0]) * (q_scale * LOG2E))[:, None], qg[2, 1][None, :])
        (aqT, ak, avT, bqT, bk, bvT, cqT, ck, cvT, iqT, ik, iwT) = _proj(
            x, norm_mix_g[l][None, :], _proj_weight(w_in[l]), seg, gains, tm_proj)

        oa = _attn_a(aqT, ak, avT, bias_a, diff_lambda[l], diff_subln_g[l][None, :], lambda_init)
        ob = _attn_b(bqT, bk, bvT, bias_b, sinks[l])
        oc = _attn_c(cqT, ck, cvT, iqT, ik, iwT, bias_c, top_k)

        w_r = jnp.concatenate([w_router_expert[l], w_router_group[l],
                               jnp.zeros((D, LANES - N_EXPERTS - N_GROUPS), F32)], axis=1)
        b_r = jnp.concatenate([b_router_expert[l], b_router_group[l],
                               jnp.zeros((LANES - N_EXPERTS - N_GROUPS,), F32)])[None, :]
        wrh = w_r.astype(BF16)
        wrl = (w_r - wrh.astype(F32)).astype(BF16)
        x1, h2, gate = _merge(
            x.reshape(T, D), oa.reshape(T, 512), ob.reshape(T, 512), oc.reshape(T, 512),
            norm_mix_g[l][None, :], w_gate[l].astype(BF16), b_gate[l][None, :], w_branch[l].astype(BF16),
            w_out[l].astype(BF16), norm_ffn_g[l][None, :], wrh, wrl, b_r, tm_merge)

        wgu = jnp.concatenate([w_ff_gate[l], w_ff_up[l]], axis=-1).astype(BF16)
        wd = w_ff_down[l].astype(BF16).reshape(N_EXPERTS * EXPERT_FF, D)
        x = _moe(x1, h2, gate, wgu, wd, tm_moe).reshape(B, S, D)
    return x
```

```python
import functools
import math

import numpy as np
import jax
import jax.numpy as jnp
from jax import lax
from jax.experimental import pallas as pl
from jax.experimental.pallas import tpu as pltpu

F32 = jnp.float32
BF16 = jnp.bfloat16
I32 = jnp.int32
I16 = jnp.int16

D_MODEL = 1024
DEPTH = 2
CHUNK = 64
HEAD_DIM = 64
A_HEADS = 4
A_V_DIM = 2 * HEAD_DIM
B_HEADS = 8
B_KV_HEADS = 2
B_GROUP = B_HEADS // B_KV_HEADS
W_CHUNKS = 2
C_HEADS = 8
IDX_HEADS = 4
IDX_DIM = 32
TOPK_MAX = 256
NUM_BUCKETS = 32
MAX_DISTANCE = 1024
N_BRANCH = 3
BRANCH_WIDTH = 512
N_GROUPS = 4
EXPERTS_PER_GROUP = 4
N_EXPERTS = N_GROUPS * EXPERTS_PER_GROUP
EXPERT_FF = 256
EPS = 1e-6
NEG = -1e30
I16_MIN = -(2 ** 15)

LANES = 128
SUBLANES = 8
TILE = 256
NEAR_TILES = 4
BF16_ROWS = 16
A_VT_ROWS = A_V_DIM + BF16_ROWS
C_VT_ROWS = HEAD_DIM + BF16_ROWS
LOG2E = 1.4426950408889634
SKEW = 5
VMEM_CAP = 60000 * 1024

_C_AQ, _C_AK, _C_AV, _C_BQ, _C_CQ = 0, 512, 1024, 1536, 2048
_C_BK0, _C_BK1, _C_BV, _C_IQ, _C_MISC = 2560, 2688, 2816, 2944, 3072
_W_COLS = 3328
_M_CV, _M_IK, _M_IW = 64, 128, 160


def _dot(a, b):
    return jnp.dot(a, b, preferred_element_type=F32)


def _split_bf16(a):
    hi = a.astype(BF16)
    lo = (a - hi.astype(F32)).astype(BF16)
    return hi, lo


def _sigmoid(x):
    return 1.0 / (1.0 + jnp.exp(-x))


def _rms(x, g):
    return x * lax.rsqrt(jnp.mean(x * x, axis=-1, keepdims=True) + EPS) * g


def _colmax8(s):
    r, c = s.shape
    return jnp.max(s.reshape(r // SUBLANES, SUBLANES, c), axis=0)


def _colsum8(s):
    r, c = s.shape
    return jnp.sum(s.reshape(r // SUBLANES, SUBLANES, c), axis=0)


def _vmem_limit(nbytes):
    return int(min(VMEM_CAP, nbytes))


def _proj_kernel(x_ref, g_ref, w_ref, seg_ref, gaq_ref, gak_ref, gbq_ref, gbk_ref, gcq_ref, gck_ref,
                 aqT_ref, ak_ref, avT_ref, bqT_ref, bk_ref, bvT_ref, cqT_ref, ck_ref, cvT_ref,
                 iqT_ref, ik_ref, iwT_ref, *, tm, iw_scale):
    hb = _rms(x_ref[0], g_ref[...]).astype(BF16)
    seg = seg_ref[...]
    n_sub = tm // TILE

    def grp(a, n):
        return _dot(hb, w_ref[:, a:a + n])

    def segnorm(t, g):
        ssq = _dot((t * t).astype(BF16), seg)
        return t * lax.rsqrt(ssq * (1.0 / HEAD_DIM) + EPS) * g

    def segnorm_t(t, gcol):
        n = t.shape[0] // HEAD_DIM
        t3 = t.reshape(n, HEAD_DIM, tm)
        ssq = jnp.sum(t3 * t3, axis=1, keepdims=True)
        return (t3 * lax.rsqrt(ssq * (1.0 / HEAD_DIM) + EPS)).reshape(t.shape) * gcol

    ones_rows = (lax.broadcasted_iota(I32, (BF16_ROWS, tm), 0) == 0).astype(F32)

    def put_slabs(ref, tT):
        for s in range(n_sub):
            ref[0, s] = tT[:, s * TILE:(s + 1) * TILE].astype(BF16)

    aqT_ref[0] = segnorm_t(grp(_C_AQ, 512).T, gaq_ref[...]).astype(BF16)
    ak_ref[0] = segnorm(grp(_C_AK, 512), gak_ref[...]).astype(BF16)
    avT = grp(_C_AV, 512).T
    put_slabs(avT_ref, jnp.concatenate(
        [p for h in range(A_HEADS) for p in (avT[h * A_V_DIM:(h + 1) * A_V_DIM, :], ones_rows)], axis=0))
    bqT_ref[0] = segnorm_t(grp(_C_BQ, 512).T, gbq_ref[...]).astype(BF16)
    cqT_ref[0] = segnorm_t(grp(_C_CQ, 512).T, gcq_ref[...]).astype(BF16)
    for g, col in enumerate((_C_BK0, _C_BK1)):
        t = grp(col, LANES)
        ssq = jnp.sum(t * t, axis=-1, keepdims=True)
        t = t * lax.rsqrt(ssq * (1.0 / HEAD_DIM) + EPS)
        bk_ref[0, g] = (t[:, :HEAD_DIM] * gbk_ref[...]).astype(BF16)
    put_slabs(bvT_ref, grp(_C_BV, LANES).T)
    iqT_ref[0] = grp(_C_IQ, LANES).T.astype(BF16)
    misc = grp(_C_MISC, 2 * LANES)
    ck = misc[:, :HEAD_DIM]
    ssq = jnp.sum(ck * ck, axis=-1, keepdims=True)
    ck_ref[0] = (ck * lax.rsqrt(ssq * (1.0 / HEAD_DIM) + EPS) * gck_ref[...]).astype(BF16)
    ik_ref[0] = misc[:, _M_IK:_M_IK + IDX_DIM].astype(BF16)
    miscT = misc.T
    put_slabs(cvT_ref, jnp.concatenate([miscT[_M_CV:_M_CV + HEAD_DIM, :], ones_rows], axis=0))
    iwT_ref[0] = miscT[_M_IW:_M_IW + IDX_HEADS, :] * iw_scale


def _proj(x, g, w, seg, gains, tm):
    B, S, D = x.shape
    nt = S // TILE
    n_sub = tm // TILE
    full = lambda shape: pl.BlockSpec(shape, lambda b, t: (0,) * len(shape))
    out_shape = [
        jax.ShapeDtypeStruct((B, 512, S), BF16),
        jax.ShapeDtypeStruct((B, S, 512), BF16),
        jax.ShapeDtypeStruct((B, nt, A_HEADS * A_VT_ROWS, TILE), BF16),
        jax.ShapeDtypeStruct((B, 512, S), BF16),
        jax.ShapeDtypeStruct((B, B_KV_HEADS, S, HEAD_DIM), BF16),
        jax.ShapeDtypeStruct((B, nt, LANES, TILE), BF16),
        jax.ShapeDtypeStruct((B, 512, S), BF16),
        jax.ShapeDtypeStruct((B, S, HEAD_DIM), BF16),
        jax.ShapeDtypeStruct((B, nt, C_VT_ROWS, TILE), BF16),
        jax.ShapeDtypeStruct((B, LANES, S), BF16),
        jax.ShapeDtypeStruct((B, S, IDX_DIM), BF16),
        jax.ShapeDtypeStruct((B, IDX_HEADS, S), F32),
    ]
    colT = lambda r: pl.BlockSpec((1, r, tm), lambda b, t: (b, 0, t))
    row = lambda c: pl.BlockSpec((1, tm, c), lambda b, t: (b, t, 0))
    slab = lambda r: pl.BlockSpec((1, n_sub, r, TILE), lambda b, t: (b, t, 0, 0))
    out_specs = [colT(512), row(512), slab(A_HEADS * A_VT_ROWS), colT(512),
                 pl.BlockSpec((1, B_KV_HEADS, tm, HEAD_DIM), lambda b, t: (b, 0, t, 0)),
                 slab(LANES), colT(512), row(HEAD_DIM), slab(C_VT_ROWS), colT(LANES), row(IDX_DIM),
                 pl.BlockSpec((1, IDX_HEADS, tm), lambda b, t: (b, 0, t))]
    in_specs = [pl.BlockSpec((1, tm, D), lambda b, t: (b, t, 0)), full((1, D)), full((D, _W_COLS)),
                full((512, 512))] + [full(gn.shape) for gn in gains]
    vmem = 2 * (tm * D * 4 + D * _W_COLS * 2 + 512 * 512 * 2 + tm * 3400 * 2) + 24 * tm * 512 * 4
    return pl.pallas_call(
        functools.partial(_proj_kernel, tm=tm, iw_scale=IDX_HEADS ** -0.5 * IDX_DIM ** -0.5),
        grid=(B, S // tm), in_specs=in_specs, out_specs=out_specs, out_shape=out_shape,
        compiler_params=pltpu.CompilerParams(dimension_semantics=("arbitrary", "arbitrary"),
                                             vmem_limit_bytes=_vmem_limit(vmem)),
        name="proj",
    )(x, g, w, seg, *gains)


def _online_step(s, m_old, vt, acc_ref, ch):
    m_new = jnp.maximum(m_old, jnp.max(_colmax8(s), axis=0, keepdims=True))
    alpha = jnp.exp2(m_old - m_new)
    e = jnp.exp2(s - m_new).astype(BF16)
    acc_ref[ch] = acc_ref[ch] * alpha + _dot(vt, e)
    return m_new


def _skewed_chains(n_chain, scores, vt, ms, acc_ref):
    new = []
    pending = [scores(ch) for ch in range(min(SKEW, n_chain))]
    for ch in range(n_chain):
        s = pending.pop(0)
        if ch + SKEW < n_chain:
            pending.append(scores(ch + SKEW))
        new.append(_online_step(s, ms[ch], vt(ch), acc_ref, ch))
    return tuple(new)


def _attn_a_kernel(qT_ref, k_ref, vT_ref, bias_ref, lam_ref, sub_ref, o_ref, q2_ref, acc_ref, *, lambda_init):
    i = pl.program_id(1)
    n_far = jnp.maximum(i - (NEAR_TILES - 1), 0)
    n_near = jnp.minimum(i, NEAR_TILES - 1) + 1
    lp = lam_ref[...]
    lam = (jnp.exp(jnp.sum(lp[0:1] * lp[1:2], axis=-1, keepdims=True))
           - jnp.exp(jnp.sum(lp[2:3] * lp[3:4], axis=-1, keepdims=True)) + lambda_init)
    row = lax.broadcasted_iota(I32, (2 * HEAD_DIM, TILE), 0)
    n_chain = 2 * A_HEADS

    for h in range(A_HEADS):
        qh = qT_ref[0, h * A_V_DIM:(h + 1) * A_V_DIM, :]
        zero = jnp.zeros_like(qh)
        q2_ref[2 * h] = jnp.where(row < HEAD_DIM, qh, zero)
        q2_ref[2 * h + 1] = jnp.where(row >= HEAD_DIM, qh, zero)
    acc_ref[...] = jnp.zeros_like(acc_ref)

    def tile_step(j, ms, d):
        rows = pl.ds(pl.multiple_of(j * TILE, TILE), TILE)

        def scores(ch):
            h = ch // 2
            s = _dot(k_ref[0, rows, h * A_V_DIM:(h + 1) * A_V_DIM], q2_ref[ch])
            return s if d is None else s + bias_ref[h, d]

        def vt(ch):
            h = ch // 2
            return vT_ref[0, j, h * A_VT_ROWS:(h + 1) * A_VT_ROWS, :]

        return _skewed_chains(n_chain, scores, vt, ms, acc_ref)

    m0 = tuple(jnp.full((1, TILE), -jnp.inf, F32) for _ in range(n_chain))
    ms = lax.fori_loop(0, n_far, lambda j, ms: tile_step(j, ms, None), m0)
    lax.fori_loop(0, n_near, lambda d, ms: tile_step(i - d, ms, d), ms)

    for h in range(A_HEADS):
        a0 = acc_ref[2 * h]
        a1 = acc_ref[2 * h + 1]
        r0 = 1.0 / a0[A_V_DIM:A_V_DIM + 1, :]
        r1 = 1.0 / a1[A_V_DIM:A_V_DIM + 1, :]
        outT = a0[:A_V_DIM, :] * r0 - lam * (a1[:A_V_DIM, :] * r1)
        out = _rms(outT.T, sub_ref[...]) * (1.0 - lambda_init)
        o_ref[0, :, h * A_V_DIM:(h + 1) * A_V_DIM] = out.astype(BF16)


def _attn_a(aqT, ak, avT, bias, lam_par, subln_g, lambda_init):
    B, _, S = aqT.shape
    nt = S // TILE
    vt_rows = A_HEADS * A_VT_ROWS
    vmem = 2 * (512 * TILE * 2 + S * 512 * 2 + S * vt_rows * 2 + bias.size * 4 + TILE * 512 * 2) \
        + 8 * A_V_DIM * TILE * 2 + 8 * A_VT_ROWS * TILE * 4 + 48 * TILE * TILE * 4
    return pl.pallas_call(
        functools.partial(_attn_a_kernel, lambda_init=lambda_init),
        grid=(B, nt),
        in_specs=[pl.BlockSpec((1, 512, TILE), lambda b, i: (b, 0, i)),
                  pl.BlockSpec((1, S, 512), lambda b, i: (b, 0, 0)),
                  pl.BlockSpec((1, nt, vt_rows, TILE), lambda b, i: (b, 0, 0, 0)),
                  pl.BlockSpec(bias.shape, lambda b, i: (0, 0, 0, 0)),
                  pl.BlockSpec((4, HEAD_DIM), lambda b, i: (0, 0)),
                  pl.BlockSpec((1, A_V_DIM), lambda b, i: (0, 0))],
        out_specs=pl.BlockSpec((1, TILE, 512), lambda b, i: (b, i, 0)),
        out_shape=jax.ShapeDtypeStruct((B, S, 512), BF16),
        scratch_shapes=[pltpu.VMEM((2 * A_HEADS, A_V_DIM, TILE), BF16),
                        pltpu.VMEM((2 * A_HEADS, A_VT_ROWS, TILE), F32)],
        compiler_params=pltpu.CompilerParams(dimension_semantics=("arbitrary", "arbitrary"),
                                             vmem_limit_bytes=_vmem_limit(vmem)),
        name="attn_a",
    )(aqT, ak, avT, bias, lam_par, subln_g)


def _attn_b_kernel(sink_ref, qT_ref, k_ref, vT_ref, bias_ref, o_ref, oT_ref):
    i = pl.program_id(1)
    jp = jnp.maximum(i - 1, 0)
    p_idx = jnp.where(i > 0, 1, 2)
    cur = pl.ds(pl.multiple_of(i * TILE, TILE), TILE)
    prev = pl.ds(pl.multiple_of(jp * TILE, TILE), TILE)

    def scores(h):
        g = h // B_GROUP
        qh = qT_ref[0, h * HEAD_DIM:(h + 1) * HEAD_DIM, :]
        return (_dot(k_ref[0, g, cur, :], qh) + bias_ref[h, 0], _dot(k_ref[0, g, prev, :], qh) + bias_ref[h, p_idx])

    def finish(h, s):
        sc, sp = s
        gs = slice((h // B_GROUP) * HEAD_DIM, (h // B_GROUP + 1) * HEAD_DIM)
        sink = sink_ref[h]
        m = jnp.max(jnp.maximum(_colmax8(sc), _colmax8(sp)), axis=0, keepdims=True)
        m = jnp.maximum(m, sink)
        ec = jnp.exp(sc - m)
        ep = jnp.exp(sp - m)
        den = jnp.sum(_colsum8(ec) + _colsum8(ep), axis=0, keepdims=True) + jnp.exp(sink - m)
        outT = _dot(vT_ref[0, i, gs, :], ec.astype(BF16)) + _dot(vT_ref[0, jp, gs, :], ep.astype(BF16))
        oT_ref[h * HEAD_DIM:(h + 1) * HEAD_DIM, :] = outT * (1.0 / den)

    pending = [scores(h) for h in range(SKEW)]
    for h in range(B_HEADS):
        s = pending.pop(0)
        if h + SKEW < B_HEADS:
            pending.append(scores(h + SKEW))
        finish(h, s)
    o_ref[0] = oT_ref[...].T.astype(BF16)


def _attn_b(bqT, bk, bvT, bias, sinks):
    B, _, S = bqT.shape
    nt = S // TILE
    vmem = 2 * (512 * TILE * 2 + 2 * S * LANES * 2 + S * LANES * 2 + bias.size * 4 + TILE * 512 * 2) \
        + 24 * TILE * TILE * 4
    return pl.pallas_call(
        _attn_b_kernel,
        grid=(B, nt),
        in_specs=[pl.BlockSpec(memory_space=pltpu.SMEM),
                  pl.BlockSpec((1, 512, TILE), lambda b, i: (b, 0, i)),
                  pl.BlockSpec((1, B_KV_HEADS, S, HEAD_DIM), lambda b, i: (b, 0, 0, 0)),
                  pl.BlockSpec((1, nt, LANES, TILE), lambda b, i: (b, 0, 0, 0)),
                  pl.BlockSpec(bias.shape, lambda b, i: (0, 0, 0, 0))],
        out_specs=pl.BlockSpec((1, TILE, 512), lambda b, i: (b, i, 0)),
        out_shape=jax.ShapeDtypeStruct((B, S, 512), BF16),
        scratch_shapes=[pltpu.VMEM((512, TILE), F32)],
        compiler_params=pltpu.CompilerParams(dimension_semantics=("arbitrary", "arbitrary"),
                                             vmem_limit_bytes=_vmem_limit(vmem)),
        name="attn_b",
    )(sinks, bqT, bk, bvT, bias)


def _attn_c_kernel(qT_ref, k_ref, vT_ref, iqT_ref, ik_ref, iwT_ref, bias_ref, o_ref,
                   hi_ref, lo_ref, mb_ref, acc_ref, oT_ref, *, top_k):
    i = pl.program_id(1)
    n_t = i + 1
    n_far = jnp.maximum(i - (NEAR_TILES - 1), 0)
    n_near = jnp.minimum(i, NEAR_TILES - 1) + 1
    krow = lax.broadcasted_iota(I32, (TILE, TILE), 0)
    qcol = lax.broadcasted_iota(I32, (TILE, TILE), 1)
    allowed = (krow // CHUNK) <= (qcol // CHUNK)

    def idx_keys(j, diag):
        ikt = ik_ref[0, pl.ds(pl.multiple_of(j * TILE, TILE), TILE), :]
        sc = jnp.zeros((TILE, TILE), F32)
        for hh in range(IDX_HEADS):
            lg = _dot(ikt, iqT_ref[0, hh * IDX_DIM:(hh + 1) * IDX_DIM, :])
            sc = sc + jnp.maximum(lg, 0.0) * iwT_ref[0, hh:hh + 1, :]
        if diag:
            sc = jnp.where(allowed, sc, NEG)
        bits = lax.bitcast_convert_type(sc, I32)
        key = bits ^ ((bits >> 31) & 0x7FFFFFFF)
        hi_ref[j] = (key >> 16).astype(I16)
        lo_ref[j] = ((key & 0xFFFF) + I16_MIN).astype(I16)

    def fill(j, carry):
        idx_keys(j, False)
        return carry

    lax.fori_loop(0, i, fill, 0)
    idx_keys(i, True)

    n_pair = (n_t + 1) // 2

    @pl.when(n_t % 2 == 1)
    def _():
        hi_ref[n_t] = jnp.full((TILE, TILE), I16_MIN, I16)
        lo_ref[n_t] = jnp.full((TILE, TILE), I16_MIN, I16)

    groups = TILE // BF16_ROWS

    def rows16(ref, j):
        return ref[j].reshape(groups, BF16_ROWS, TILE)

    def bcast16(v):
        return jnp.broadcast_to(v, (BF16_ROWS, TILE)).astype(I16)

    def count(pred):
        def body(p, c):
            for u in (0, 1):
                j = 2 * p + u
                hi, lo = rows16(hi_ref, j), rows16(lo_ref, j)
                for r in range(groups):
                    c = c + pred(hi[r], lo[r], j, r).astype(I16)
            return c
        c = lax.fori_loop(0, n_pair, body, jnp.zeros((BF16_ROWS, TILE), I16))
        return jnp.sum(c.astype(I32), axis=0, keepdims=True)

    def search(n_bits, accept):
        def step(b, t):
            cand = t + lax.shift_left(jnp.int32(1), n_bits - 1 - b)
            return jnp.where(accept(cand), cand, t)
        return lax.fori_loop(0, n_bits, step, jnp.full((1, TILE), I16_MIN, I32))

    def hi_accept(cand):
        c16 = bcast16(cand)
        return count(lambda hi, lo, j, r: hi >= c16) >= top_k

    t_hi = search(16, hi_accept)
    th16 = bcast16(t_hi)
    cnt_above = count(lambda hi, lo, j, r: hi > th16)
    r_lo = top_k - cnt_above

    def bucket_only(p, carry):
        for u in (0, 1):
            j = 2 * p + u
            hi, lo = rows16(hi_ref, j), rows16(lo_ref, j)
            for r in range(groups):
                lo_ref[j, r * BF16_ROWS:(r + 1) * BF16_ROWS, :] = jnp.where(hi[r] == th16, lo[r], I16_MIN)
        return carry

    lax.fori_loop(0, n_pair, bucket_only, 0)

    def lo_accept(cand):
        c16 = bcast16(cand)
        return count(lambda hi, lo, j, r: lo >= c16) >= r_lo

    t_lo = search(16, lo_accept)
    tl16 = bcast16(t_lo)
    cnt_gt = count(lambda hi, lo, j, r: lo > tl16)
    r_eq = (r_lo - cnt_gt).astype(F32)

    tri = (krow >= qcol).astype(BF16)
    one, zero = jnp.ones((), BF16), jnp.zeros((), BF16)

    def mask_tile(j, ties_before, diag):
        hi, lo = rows16(hi_ref, j), rows16(lo_ref, j)
        eq = jnp.concatenate([jnp.where((lo[r] == tl16) & (hi[r] == th16), one, zero) for r in range(groups)], axis=0)
        gt = jnp.concatenate([jnp.where((hi[r] > th16) | (lo[r] > tl16), one, zero) for r in range(groups)], axis=0)
        ties = _dot(tri, eq) + ties_before
        sel = (gt.astype(F32) > 0.5) | ((eq.astype(F32) > 0.5) & (ties <= r_eq))
        if diag:
            sel = sel & allowed
        mb_ref[j] = jnp.where(sel, 0.0, NEG)
        return ties[TILE - 1:TILE, :]

    ties_before = lax.fori_loop(0, i, lambda j, c: mask_tile(j, c, False), jnp.zeros((1, TILE), F32))
    mask_tile(i, ties_before, True)

    acc_ref[...] = jnp.zeros_like(acc_ref)

    def tile_step(j, ms, d):
        kt = k_ref[0, pl.ds(pl.multiple_of(j * TILE, TILE), TILE), :]

        def scores(h):
            s = _dot(kt, qT_ref[0, h * HEAD_DIM:(h + 1) * HEAD_DIM, :]) + mb_ref[j]
            return s if d is None else s + bias_ref[h, d]

        return _skewed_chains(C_HEADS, scores, lambda h: vT_ref[0, j], ms, acc_ref)

    m0 = tuple(jnp.full((1, TILE), -jnp.inf, F32) for _ in range(C_HEADS))
    ms = lax.fori_loop(0, n_far, lambda j, ms: tile_step(j, ms, None), m0)
    lax.fori_loop(0, n_near, lambda d, ms: tile_step(i - d, ms, d), ms)

    for h in range(C_HEADS):
        a = acc_ref[h]
        oT_ref[h * HEAD_DIM:(h + 1) * HEAD_DIM, :] = a[:HEAD_DIM, :] * (1.0 / a[HEAD_DIM:HEAD_DIM + 1, :])
    o_ref[0] = oT_ref[...].T.astype(BF16)


def _attn_c(cqT, ck, cvT, iqT, ik, iwT, bias, top_k):
    B, _, S = cqT.shape
    nt = S // TILE
    vmem = 2 * (512 * TILE * 2 + 2 * S * LANES * 2 + S * HEAD_DIM * 2 + LANES * TILE * 2 + 8 * TILE * 4
                + bias.size * 4 + TILE * 512 * 2) + 2 * S * TILE * 4 + 48 * TILE * TILE * 4
    return pl.pallas_call(
        functools.partial(_attn_c_kernel, top_k=top_k),
        grid=(B, nt),
        in_specs=[pl.BlockSpec((1, 512, TILE), lambda b, i: (b, 0, i)),
                  pl.BlockSpec((1, S, HEAD_DIM), lambda b, i: (b, 0, 0)),
                  pl.BlockSpec((1, nt, C_VT_ROWS, TILE), lambda b, i: (b, 0, 0, 0)),
                  pl.BlockSpec((1, LANES, TILE), lambda b, i: (b, 0, i)),
                  pl.BlockSpec((1, S, IDX_DIM), lambda b, i: (b, 0, 0)),
                  pl.BlockSpec((1, IDX_HEADS, TILE), lambda b, i: (b, 0, i)),
                  pl.BlockSpec(bias.shape, lambda b, i: (0, 0, 0, 0))],
        out_specs=pl.BlockSpec((1, TILE, 512), lambda b, i: (b, i, 0)),
        out_shape=jax.ShapeDtypeStruct((B, S, 512), BF16),
        scratch_shapes=[pltpu.VMEM((nt + nt % 2, TILE, TILE), I16), pltpu.VMEM((nt + nt % 2, TILE, TILE), I16),
                        pltpu.VMEM((nt, TILE, TILE), F32),
                        pltpu.VMEM((C_HEADS, C_VT_ROWS, TILE), F32), pltpu.VMEM((512, TILE), F32)],
        compiler_params=pltpu.CompilerParams(dimension_semantics=("arbitrary", "arbitrary"),
                                             vmem_limit_bytes=_vmem_limit(vmem)),
        name="attn_c",
    )(cqT, ck, cvT, iqT, ik, iwT, bias)


def _merge_kernel(x_ref, oa_ref, ob_ref, oc_ref, gmix_ref, wg_ref, bg_ref, wb_ref, wo_ref, gffn_ref,
                  wrh_ref, wrl_ref, br_ref, x1_ref, h2_ref, gate_ref):
    x = x_ref[...]
    hb = _rms(x, gmix_ref[...]).astype(BF16)
    z = None
    for n, o_ref in enumerate((oa_ref, ob_ref, oc_ref)):
        cs = slice(n * D_MODEL, (n + 1) * D_MODEL)
        gate = _sigmoid(_dot(hb, wg_ref[:, cs]) + bg_ref[:, cs])
        y = _dot(o_ref[...], wb_ref[n])
        z = gate * y if z is None else z + gate * y
    x1 = x + _dot(z.astype(BF16), wo_ref[...])
    x1_ref[...] = x1
    h2 = _rms(x1, gffn_ref[...])
    h2_ref[...] = h2.astype(BF16)

    hi, lo = _split_bf16(h2)
    lg = _dot(hi, wrh_ref[...]) + _dot(lo, wrh_ref[...]) + _dot(hi, wrl_ref[...]) + br_ref[...]
    col = lax.broadcasted_iota(I32, lg.shape, 1).astype(F32)
    big = float(4 * LANES)
    is_g = (col >= N_EXPERTS) & (col < N_EXPERTS + N_GROUPS)
    gl = jnp.where(is_g, lg, -jnp.inf)
    gmax = jnp.max(gl, axis=-1, keepdims=True)
    p_group = 1.0 / jnp.sum(jnp.exp(gl - gmax), axis=-1, keepdims=True)
    g_sel = jnp.min(jnp.where(gl == gmax, col, big), axis=-1, keepdims=True) - N_EXPERTS
    in_g = (col >= g_sel * EXPERTS_PER_GROUP) & (col < (g_sel + 1) * EXPERTS_PER_GROUP)
    el = jnp.where(in_g, lg, -jnp.inf)
    e1 = jnp.max(el, axis=-1, keepdims=True)
    i1 = jnp.min(jnp.where(el == e1, col, big), axis=-1, keepdims=True)
    el2 = jnp.where(col == i1, -jnp.inf, el)
    e2 = jnp.max(el2, axis=-1, keepdims=True)
    i2 = jnp.min(jnp.where(el2 == e2, col, big), axis=-1, keepdims=True)
    t2 = jnp.exp(e2 - e1)
    w1 = p_group / (1.0 + t2)
    w2 = w1 * t2
    gates = jnp.where(col == i1, w1, 0.0) + jnp.where(col == i2, w2, 0.0)
    gate_ref[...] = gates[:, :N_EXPERTS]


def _merge(x2, oa, ob, oc, gmix, wg, bg, wb, wo, gffn, wrh, wrl, br, tm):
    T, D = x2.shape
    full = lambda a: pl.BlockSpec(a.shape, lambda t: (0,) * a.ndim)
    rowb = lambda c: pl.BlockSpec((tm, c), lambda t: (t, 0))
    vmem = 2 * (tm * D * 4 * 2 + 3 * tm * 512 * 2 + tm * D * 2 + wg.size * 2 + wb.size * 2 + wo.size * 2
                + 2 * D * LANES * 2) + 10 * tm * D * 4
    return pl.pallas_call(
        _merge_kernel,
        grid=(T // tm,),
        in_specs=[rowb(D), rowb(512), rowb(512), rowb(512), full(gmix), full(wg), full(bg), full(wb),
                  full(wo), full(gffn), full(wrh), full(wrl), full(br)],
        out_specs=[rowb(D), rowb(D), rowb(N_EXPERTS)],
        out_shape=[jax.ShapeDtypeStruct((T, D), F32), jax.ShapeDtypeStruct((T, D), BF16),
                   jax.ShapeDtypeStruct((T, N_EXPERTS), F32)],
        compiler_params=pltpu.CompilerParams(dimension_semantics=("arbitrary",),
                                             vmem_limit_bytes=_vmem_limit(vmem)),
        name="merge",
    )(x2, oa, ob, oc, gmix, wg, bg, wb, wo, gffn, wrh, wrl, br)


def _moe_kernel(x1_ref, h2_ref, gate_ref, wgu_ref, wd_ref, o_ref):
    g = pl.program_id(1)
    h2 = h2_ref[...]
    gate = gate_ref[...]
    lane = lax.broadcasted_iota(I32, gate.shape, 1)
    hids = []
    for u in range(EXPERTS_PER_GROUP):
        gu = _dot(h2, wgu_ref[u])
        a = gu[:, :EXPERT_FF]
        w = jnp.sum(jnp.where(lane == g * EXPERTS_PER_GROUP + u, gate, 0.0), axis=-1, keepdims=True)
        hids.append((a * _sigmoid(a) * gu[:, EXPERT_FF:] * w).astype(BF16))
    y = _dot(jnp.concatenate(hids, axis=1), wd_ref[...])

    @pl.when(g == 0)
    def _():
        o_ref[...] = x1_ref[...] + y

    @pl.when(g > 0)
    def _():
        o_ref[...] += y


def _moe(x1, h2, gate, wgu, wd, tm):
    T, D = x1.shape
    n_e, ff = EXPERTS_PER_GROUP, EXPERT_FF
    vmem = 2 * (tm * D * 4 * 2 + tm * D * 2 + tm * LANES * 4 + n_e * D * 2 * ff * 2 + n_e * ff * D * 2) \
        + 3 * n_e * tm * 2 * ff * 4 + 2 * tm * D * 4
    return pl.pallas_call(
        _moe_kernel,
        grid=(T // tm, N_GROUPS),
        in_specs=[pl.BlockSpec((tm, D), lambda t, g: (t, 0)),
                  pl.BlockSpec((tm, D), lambda t, g: (t, 0)),
                  pl.BlockSpec((tm, N_EXPERTS), lambda t, g: (t, 0)),
                  pl.BlockSpec((n_e, D, 2 * ff), lambda t, g: (g, 0, 0)),
                  pl.BlockSpec((n_e * ff, D), lambda t, g: (g, 0))],
        out_specs=pl.BlockSpec((tm, D), lambda t, g: (t, 0)),
        out_shape=jax.ShapeDtypeStruct((T, D), F32),
        compiler_params=pltpu.CompilerParams(dimension_semantics=("arbitrary", "arbitrary"),
                                             vmem_limit_bytes=_vmem_limit(vmem)),
        name="moe",
    )(x1, h2, gate, wgu, wd)


def _t5_bucket_np(rel):
    half = NUM_BUCKETS // 2
    max_exact = half // 2
    n = np.abs(rel)
    n_f = np.maximum(n, 1).astype(np.float32)
    large = max_exact + (np.log(n_f / np.float32(max_exact)) / np.float32(math.log(MAX_DISTANCE / max_exact))
                         * np.float32(half - max_exact)).astype(np.int32)
    large = np.minimum(large, half - 1)
    return np.where(rel > 0, half, 0) + np.where(n < max_exact, n, large)


def _toeplitz_kernel(v_ref, o_ref):
    v = v_ref[0]
    for d in range(NEAR_TILES):
        x = jnp.broadcast_to(v[d:d + 1, :], (TILE, 2 * TILE))
        o_ref[0, d] = pltpu.roll(x, 0, 1, stride=1, stride_axis=0)[:, :TILE]


def _toeplitz(vals):
    n_heads = vals.shape[0]
    return pl.pallas_call(
        _toeplitz_kernel,
        grid=(n_heads,),
        in_specs=[pl.BlockSpec((1, NEAR_TILES, 2 * TILE), lambda h: (h, 0, 0))],
        out_specs=pl.BlockSpec((1, NEAR_TILES, TILE, TILE), lambda h: (h, 0, 0, 0)),
        out_shape=jax.ShapeDtypeStruct((n_heads, NEAR_TILES, TILE, TILE), F32),
        name="bias_tiles",
    )(vals)


def _bias_tables(rel_bias):
    u = np.arange(2 * TILE)
    off = np.where(u < TILE, u, u - 2 * TILE)
    rel = np.stack([-off - TILE * d for d in range(NEAR_TILES)])
    onehot = (_t5_bucket_np(rel)[..., None] == np.arange(NUM_BUCKETS)).astype(np.float32)
    vals = jnp.einsum("dub,bh->hdu", jnp.asarray(onehot), rel_bias, precision=lax.Precision.HIGHEST)
    tab = _toeplitz(vals)

    far = rel_bias[NUM_BUCKETS // 2 - 1]
    kk = np.arange(TILE)[:, None]
    qc = np.arange(TILE)[None, :] // CHUNK
    kc = np.stack([(kk // CHUNK) - (TILE // CHUNK) * d for d in range(NEAR_TILES)]) + 0 * qc
    a_ok = jnp.asarray(kc <= qc)
    ta = jnp.where(a_ok[None], (tab[:A_HEADS] - far[:A_HEADS, None, None, None]) * LOG2E, NEG)
    b_ok = jnp.asarray((qc - kc[:2] >= 0) & (qc - kc[:2] <= W_CHUNKS))
    tb = jnp.where(b_ok[None], tab[A_HEADS:A_HEADS + B_HEADS, :2], NEG)
    tb = jnp.concatenate([tb, jnp.full((B_HEADS, 1, TILE, TILE), NEG, F32)], axis=1)
    tc = (tab[A_HEADS + B_HEADS:] - far[A_HEADS + B_HEADS:, None, None, None]) * LOG2E
    return ta.astype(F32), tb.astype(F32), tc.astype(F32)


def _proj_weight(w_in):
    cols = np.cumsum((0, 512, 512, 512, 512, 128, 128, 512, 64, 64, 128, 32, 4))
    aq, ak, av, bq, bk, bv, cq, ck, cv, iq, ik, iw = [w_in[:, cols[n]:cols[n + 1]] for n in range(12)]
    z = lambda n: jnp.zeros((w_in.shape[0], n), w_in.dtype)
    w = jnp.concatenate([aq, ak, av, bq, cq, bk[:, :64], z(64), bk[:, 64:], z(64), bv, iq,
                         ck, cv, ik, iw, z(2 * LANES - _M_IW - IDX_HEADS)], axis=1)
    return w.astype(BF16)


def kernel(x, rel_bias, norm_mix_g, w_in, qk_norm_g, diff_lambda, diff_subln_g, sinks, w_branch, w_gate, b_gate,
           w_out, norm_ffn_g, w_router_group, b_router_group, w_router_expert, b_router_expert, w_ff_gate,
           w_ff_up, w_ff_down):
    B, S, D = x.shape
    assert D == D_MODEL and S % TILE == 0
    T = B * S
    top_k = min(TOPK_MAX, S // 4)
    tm_proj = 512 if S % 512 == 0 else TILE
    tm_merge = 512 if T % 512 == 0 else TILE
    tm_moe = 1024 if T % 1024 == 0 else TILE

    bias_a, bias_b, bias_c = _bias_tables(rel_bias)
    seg = jnp.asarray(np.kron(np.eye(512 // HEAD_DIM), np.ones((HEAD_DIM, HEAD_DIM))), BF16)
    q_scale = HEAD_DIM ** -0.5

    for l in range(DEPTH):
        lambda_init = 0.8 - 0.6 * math.exp(-0.3 * l)
        qg = qk_norm_g[l]
        tile8 = lambda g: jnp.tile(g, 512 // HEAD_DIM)
        gains = ((tile8(qg[0, 0]) * (q_scale * LOG2E))[:, None], tile8(qg[0, 1])[None, :],
                 (tile8(qg[1, 0]) * q_scale)[:, None], qg[1, 1][None, :],
                 (tile8(qg[2, 0]) * (q_scale * LOG2E))[:, None], qg[2, 1][None, :])
        (aqT, ak, avT, bqT, bk, bvT, cqT, ck, cvT, iqT, ik, iwT) = _proj(
            x, norm_mix_g[l][None, :], _proj_weight(w_in[l]), seg, gains, tm_proj)

        oa = _attn_a(aqT, ak, avT, bias_a, diff_lambda[l], diff_subln_g[l][None, :], lambda_init)
        ob = _attn_b(bqT, bk, bvT, bias_b, sinks[l])
        oc = _attn_c(cqT, ck, cvT, iqT, ik, iwT, bias_c, top_k)

        w_r = jnp.concatenate([w_router_expert[l], w_router_group[l],
                               jnp.zeros((D, LANES - N_EXPERTS - N_GROUPS), F32)], axis=1)
        b_r = jnp.concatenate([b_router_expert[l], b_router_group[l],
                               jnp.zeros((LANES - N_EXPERTS - N_GROUPS,), F32)])[None, :]
        wrh = w_r.astype(BF16)
        wrl = (w_r - wrh.astype(F32)).astype(BF16)
        x1, h2, gate = _merge(
            x.reshape(T, D), oa.reshape(T, 512), ob.reshape(T, 512), oc.reshape(T, 512),
            norm_mix_g[l][None, :], w_gate[l].astype(BF16), b_gate[l][None, :], w_branch[l].astype(BF16),
            w_out[l].astype(BF16), norm_ffn_g[l][None, :], wrh, wrl, b_r, tm_merge)

        wgu = jnp.concatenate([w_ff_gate[l], w_ff_up[l]], axis=-1).astype(BF16)
        wd = w_ff_down[l].astype(BF16).reshape(N_EXPERTS * EXPERT_FF, D)
        x = _moe(x1, h2, gate, wgu, wd, tm_moe).reshape(B, S, D)
    return x
```

```python
import functools
import math

import numpy as np
import jax
import jax.numpy as jnp
from jax import lax
from jax.experimental import pallas as pl
from jax.experimental.pallas import tpu as pltpu

F32 = jnp.float32
BF16 = jnp.bfloat16
I32 = jnp.int32
I16 = jnp.int16

D_MODEL = 1024
DEPTH = 2
CHUNK = 64
HEAD_DIM = 64
A_HEADS = 4
A_V_DIM = 2 * HEAD_DIM
B_HEADS = 8
B_KV_HEADS = 2
B_GROUP = B_HEADS // B_KV_HEADS
W_CHUNKS = 2
C_HEADS = 8
IDX_HEADS = 4
IDX_DIM = 32
TOPK_MAX = 256
NUM_BUCKETS = 32
MAX_DISTANCE = 1024
N_BRANCH = 3
BRANCH_WIDTH = 512
N_GROUPS = 4
EXPERTS_PER_GROUP = 4
N_EXPERTS = N_GROUPS * EXPERTS_PER_GROUP
EXPERT_FF = 256
EPS = 1e-6
NEG = -1e30
I16_MIN = -(2 ** 15)

LANES = 128
SUBLANES = 8
TILE = 256
NEAR_TILES = 4
BF16_ROWS = 16
A_VT_ROWS = A_V_DIM + BF16_ROWS
C_VT_ROWS = HEAD_DIM + BF16_ROWS
LOG2E = 1.4426950408889634
SKEW = 5
VMEM_CAP = 60000 * 1024

_C_AQ, _C_AK, _C_AV, _C_BQ, _C_BK, _C_BV, _C_CQ = 0, 512, 1024, 1536, 2048, 2176, 2304
_C_CKV, _C_IQ, _C_IKW = 2816, 2944, 3072
_IN_COLS = 3108
_W_COLS = 3200


def _dot(a, b):
    return jnp.dot(a, b, preferred_element_type=F32)


def _split_bf16(a):
    hi = a.astype(BF16)
    lo = (a - hi.astype(F32)).astype(BF16)
    return hi, lo


def _sigmoid(x):
    return 1.0 / (1.0 + jnp.exp(-x))


def _rms(x, g):
    return x * lax.rsqrt(jnp.mean(x * x, axis=-1, keepdims=True) + EPS) * g


def _colmax8(s):
    r, c = s.shape
    return jnp.max(s.reshape(r // SUBLANES, SUBLANES, c), axis=0)


def _colsum8(s):
    r, c = s.shape
    return jnp.sum(s.reshape(r // SUBLANES, SUBLANES, c), axis=0)


def _vmem_limit(nbytes):
    return int(min(VMEM_CAP, nbytes))


def _proj_kernel(x_ref, g_ref, w_ref, seg_ref, gaq_ref, gak_ref, gbq_ref, gbk_ref, gcq_ref, gck_ref,
                 aqT_ref, ak_ref, avT_ref, bqT_ref, bk_ref, bvT_ref, cqT_ref, ck_ref, cvT_ref,
                 iqT_ref, ik_ref, iwT_ref, *, tm, iw_scale):
    hb = _rms(x_ref[0], g_ref[...]).astype(BF16)
    seg = seg_ref[...]
    n_sub = tm // TILE

    def grp(a, n):
        return _dot(hb, w_ref[:, a:a + n])

    def segnorm(t, g):
        n = t.shape[1]
        ssq = _dot((t * t).astype(BF16), seg[:n, :n])
        return t * lax.rsqrt(ssq * (1.0 / HEAD_DIM) + EPS) * g

    def segnorm_t(t, gcol):
        n = t.shape[0] // HEAD_DIM
        t3 = t.reshape(n, HEAD_DIM, tm)
        ssq = jnp.sum(t3 * t3, axis=1, keepdims=True)
        return (t3 * lax.rsqrt(ssq * (1.0 / HEAD_DIM) + EPS)).reshape(t.shape) * gcol

    ones_rows = (lax.broadcasted_iota(I32, (BF16_ROWS, tm), 0) == 0).astype(F32)

    def put_slabs(ref, tT):
        for s in range(n_sub):
            ref[0, s] = tT[:, s * TILE:(s + 1) * TILE].astype(BF16)

    aqT_ref[0] = segnorm_t(grp(_C_AQ, 512).T, gaq_ref[...]).astype(BF16)
    ak_ref[0] = segnorm(grp(_C_AK, 512), gak_ref[...]).astype(BF16)
    avT = grp(_C_AV, 512).T
    put_slabs(avT_ref, jnp.concatenate(
        [p for h in range(A_HEADS) for p in (avT[h * A_V_DIM:(h + 1) * A_V_DIM, :], ones_rows)], axis=0))
    bqT_ref[0] = segnorm_t(grp(_C_BQ, 512).T, gbq_ref[...]).astype(BF16)
    cqT_ref[0] = segnorm_t(grp(_C_CQ, 512).T, gcq_ref[...]).astype(BF16)
    bk_ref[0] = segnorm(grp(_C_BK, LANES), gbk_ref[...]).astype(BF16)
    put_slabs(bvT_ref, grp(_C_BV, LANES).T)
    iqT_ref[0] = grp(_C_IQ, LANES).T.astype(BF16)
    ckv = grp(_C_CKV, LANES)
    ck = ckv[:, :HEAD_DIM]
    ssq = jnp.sum(ck * ck, axis=-1, keepdims=True)
    ck_ref[0] = (ck * lax.rsqrt(ssq * (1.0 / HEAD_DIM) + EPS) * gck_ref[...]).astype(BF16)
    put_slabs(cvT_ref, jnp.concatenate([ckv.T[HEAD_DIM:, :], ones_rows], axis=0))
    ikw = grp(_C_IKW, LANES)
    ik_ref[0] = ikw[:, :IDX_DIM].astype(BF16)
    iwT_ref[0] = ikw.T[IDX_DIM:IDX_DIM + IDX_HEADS, :] * iw_scale


def _proj(x, g, w, seg, gains, tm):
    B, S, D = x.shape
    nt = S // TILE
    n_sub = tm // TILE
    full = lambda shape: pl.BlockSpec(shape, lambda b, t: (0,) * len(shape))
    out_shape = [
        jax.ShapeDtypeStruct((B, 512, S), BF16),
        jax.ShapeDtypeStruct((B, S, 512), BF16),
        jax.ShapeDtypeStruct((B, nt, A_HEADS * A_VT_ROWS, TILE), BF16),
        jax.ShapeDtypeStruct((B, 512, S), BF16),
        jax.ShapeDtypeStruct((B, S, LANES), BF16),
        jax.ShapeDtypeStruct((B, nt, LANES, TILE), BF16),
        jax.ShapeDtypeStruct((B, 512, S), BF16),
        jax.ShapeDtypeStruct((B, S, HEAD_DIM), BF16),
        jax.ShapeDtypeStruct((B, nt, C_VT_ROWS, TILE), BF16),
        jax.ShapeDtypeStruct((B, LANES, S), BF16),
        jax.ShapeDtypeStruct((B, S, IDX_DIM), BF16),
        jax.ShapeDtypeStruct((B, IDX_HEADS, S), F32),
    ]
    colT = lambda r: pl.BlockSpec((1, r, tm), lambda b, t: (b, 0, t))
    row = lambda c: pl.BlockSpec((1, tm, c), lambda b, t: (b, t, 0))
    slab = lambda r: pl.BlockSpec((1, n_sub, r, TILE), lambda b, t: (b, t, 0, 0))
    out_specs = [colT(512), row(512), slab(A_HEADS * A_VT_ROWS), colT(512),
                 row(LANES),
                 slab(LANES), colT(512), row(HEAD_DIM), slab(C_VT_ROWS), colT(LANES), row(IDX_DIM),
                 pl.BlockSpec((1, IDX_HEADS, tm), lambda b, t: (b, 0, t))]
    in_specs = [pl.BlockSpec((1, tm, D), lambda b, t: (b, t, 0)), full((1, D)), full((D, _W_COLS)),
                full((512, 512))] + [full(gn.shape) for gn in gains]
    vmem = 2 * (tm * D * 4 + D * _W_COLS * 2 + 512 * 512 * 2 + tm * 3400 * 2) + 24 * tm * 512 * 4
    return pl.pallas_call(
        functools.partial(_proj_kernel, tm=tm, iw_scale=IDX_HEADS ** -0.5 * IDX_DIM ** -0.5),
        grid=(B, S // tm), in_specs=in_specs, out_specs=out_specs, out_shape=out_shape,
        compiler_params=pltpu.CompilerParams(dimension_semantics=("arbitrary", "arbitrary"),
                                             vmem_limit_bytes=_vmem_limit(vmem)),
        name="proj",
    )(x, g, w, seg, *gains)


def _online_step(s, m_old, vt, acc_ref, ch):
    m_new = jnp.maximum(m_old, jnp.max(_colmax8(s), axis=0, keepdims=True))
    alpha = jnp.exp2(m_old - m_new)
    e = jnp.exp2(s - m_new).astype(BF16)
    acc_ref[ch] = acc_ref[ch] * alpha + _dot(vt, e)
    return m_new


def _sweep_key_tiles(i, n_chain, scores, vt, acc_ref):
    n_far = jnp.maximum(i - (NEAR_TILES - 1), 0)

    def step(tiles, ms):
        ms = list(ms)
        chains = [(j, d, ch) for (j, d) in tiles for ch in range(n_chain)]
        pending = [scores(*c) for c in chains[:SKEW]]
        for n, (j, d, ch) in enumerate(chains):
            s = pending.pop(0)
            if n + SKEW < len(chains):
                pending.append(scores(*chains[n + SKEW]))
            ms[ch] = _online_step(s, ms[ch], vt(j, ch), acc_ref, ch)
        return tuple(ms)

    def near_pairs(ms):
        ms = step([(i - 3, 3), (i - 2, 2)], ms)
        return step([(i - 1, 1), (i, 0)], ms)

    def near_singles(ms):
        return lax.fori_loop(0, i + 1, lambda t, ms: step([(t, i - t)], ms), ms)

    assert NEAR_TILES == 4
    ms = tuple(jnp.full((1, TILE), -jnp.inf, F32) for _ in range(n_chain))
    ms = lax.fori_loop(0, n_far // 2, lambda p, ms: step([(2 * p, None), (2 * p + 1, None)], ms), ms)
    ms = lax.cond(n_far % 2 == 1, lambda ms: step([(n_far - 1, None)], ms), lambda ms: ms, ms)
    lax.cond(i >= NEAR_TILES - 1, near_pairs, near_singles, ms)


def _attn_a_kernel(qT_ref, k_ref, vT_ref, bias_ref, lam_ref, sub_ref, o_ref, q2_ref, acc_ref, *, lambda_init):
    i = pl.program_id(1)
    lp = lam_ref[...]
    lam = (jnp.exp(jnp.sum(lp[0:1] * lp[1:2], axis=-1, keepdims=True))
           - jnp.exp(jnp.sum(lp[2:3] * lp[3:4], axis=-1, keepdims=True)) + lambda_init)
    row = lax.broadcasted_iota(I32, (2 * HEAD_DIM, TILE), 0)
    n_chain = 2 * A_HEADS

    for h in range(A_HEADS):
        qh = qT_ref[0, h * A_V_DIM:(h + 1) * A_V_DIM, :]
        zero = jnp.zeros_like(qh)
        q2_ref[2 * h] = jnp.where(row < HEAD_DIM, qh, zero)
        q2_ref[2 * h + 1] = jnp.where(row >= HEAD_DIM, qh, zero)
    acc_ref[...] = jnp.zeros_like(acc_ref)

    def scores(j, d, ch):
        h = ch // 2
        rows = pl.ds(pl.multiple_of(j * TILE, TILE), TILE)
        s = _dot(k_ref[0, rows, h * A_V_DIM:(h + 1) * A_V_DIM], q2_ref[ch])
        return s if d is None else s + bias_ref[h, d]

    def vt(j, ch):
        h = ch // 2
        return vT_ref[0, j, h * A_VT_ROWS:(h + 1) * A_VT_ROWS, :]

    _sweep_key_tiles(i, n_chain, scores, vt, acc_ref)

    for h in range(A_HEADS):
        a0 = acc_ref[2 * h]
        a1 = acc_ref[2 * h + 1]
        r0 = 1.0 / a0[A_V_DIM:A_V_DIM + 1, :]
        r1 = 1.0 / a1[A_V_DIM:A_V_DIM + 1, :]
        outT = a0[:A_V_DIM, :] * r0 - lam * (a1[:A_V_DIM, :] * r1)
        out = _rms(outT.T, sub_ref[...]) * (1.0 - lambda_init)
        o_ref[0, :, h * A_V_DIM:(h + 1) * A_V_DIM] = out.astype(BF16)


def _attn_a(aqT, ak, avT, bias, lam_par, subln_g, lambda_init):
    B, _, S = aqT.shape
    nt = S // TILE
    vt_rows = A_HEADS * A_VT_ROWS
    vmem = 2 * (512 * TILE * 2 + S * 512 * 2 + S * vt_rows * 2 + bias.size * 4 + TILE * 512 * 2) \
        + 8 * A_V_DIM * TILE * 2 + 8 * A_VT_ROWS * TILE * 4 + 48 * TILE * TILE * 4
    return pl.pallas_call(
        functools.partial(_attn_a_kernel, lambda_init=lambda_init),
        grid=(B, nt),
        in_specs=[pl.BlockSpec((1, 512, TILE), lambda b, i: (b, 0, i)),
                  pl.BlockSpec((1, S, 512), lambda b, i: (b, 0, 0)),
                  pl.BlockSpec((1, nt, vt_rows, TILE), lambda b, i: (b, 0, 0, 0)),
                  pl.BlockSpec(bias.shape, lambda b, i: (0, 0, 0, 0)),
                  pl.BlockSpec((4, HEAD_DIM), lambda b, i: (0, 0)),
                  pl.BlockSpec((1, A_V_DIM), lambda b, i: (0, 0))],
        out_specs=pl.BlockSpec((1, TILE, 512), lambda b, i: (b, i, 0)),
        out_shape=jax.ShapeDtypeStruct((B, S, 512), BF16),
        scratch_shapes=[pltpu.VMEM((2 * A_HEADS, A_V_DIM, TILE), BF16),
                        pltpu.VMEM((2 * A_HEADS, A_VT_ROWS, TILE), F32)],
        compiler_params=pltpu.CompilerParams(dimension_semantics=("arbitrary", "arbitrary"),
                                             vmem_limit_bytes=_vmem_limit(vmem)),
        name="attn_a",
    )(aqT, ak, avT, bias, lam_par, subln_g)


def _attn_b_kernel(sink_ref, qT_ref, k_ref, vT_ref, bias_ref, o_ref, oT_ref):
    i = pl.program_id(1)
    jp = jnp.maximum(i - 1, 0)
    p_idx = jnp.where(i > 0, 1, 2)
    cur = pl.ds(pl.multiple_of(i * TILE, TILE), TILE)
    prev = pl.ds(pl.multiple_of(jp * TILE, TILE), TILE)

    def scores(h):
        qh = qT_ref[0, h * HEAD_DIM:(h + 1) * HEAD_DIM, :]
        zero = jnp.zeros_like(qh)
        q2 = jnp.concatenate([qh, zero] if h < B_GROUP else [zero, qh], axis=0)
        return (_dot(k_ref[0, cur, :], q2) + bias_ref[h, 0], _dot(k_ref[0, prev, :], q2) + bias_ref[h, p_idx])

    def finish(h, s):
        sc, sp = s
        gs = slice((h // B_GROUP) * HEAD_DIM, (h // B_GROUP + 1) * HEAD_DIM)
        sink = sink_ref[h]
        m = jnp.max(jnp.maximum(_colmax8(sc), _colmax8(sp)), axis=0, keepdims=True)
        m = jnp.maximum(m, sink)
        ec = jnp.exp(sc - m)
        ep = jnp.exp(sp - m)
        den = jnp.sum(_colsum8(ec) + _colsum8(ep), axis=0, keepdims=True) + jnp.exp(sink - m)
        outT = _dot(vT_ref[0, i, gs, :], ec.astype(BF16)) + _dot(vT_ref[0, jp, gs, :], ep.astype(BF16))
        oT_ref[h * HEAD_DIM:(h + 1) * HEAD_DIM, :] = outT * (1.0 / den)

    pending = [scores(h) for h in range(SKEW)]
    for h in range(B_HEADS):
        s = pending.pop(0)
        if h + SKEW < B_HEADS:
            pending.append(scores(h + SKEW))
        finish(h, s)
    o_ref[0] = oT_ref[...].T.astype(BF16)


def _attn_b(bqT, bk, bvT, bias, sinks):
    B, _, S = bqT.shape
    nt = S // TILE
    vmem = 2 * (512 * TILE * 2 + 2 * S * LANES * 2 + S * LANES * 2 + bias.size * 4 + TILE * 512 * 2) \
        + 24 * TILE * TILE * 4
    return pl.pallas_call(
        _attn_b_kernel,
        grid=(B, nt),
        in_specs=[pl.BlockSpec(memory_space=pltpu.SMEM),
                  pl.BlockSpec((1, 512, TILE), lambda b, i: (b, 0, i)),
                  pl.BlockSpec((1, S, LANES), lambda b, i: (b, 0, 0)),
                  pl.BlockSpec((1, nt, LANES, TILE), lambda b, i: (b, 0, 0, 0)),
                  pl.BlockSpec(bias.shape, lambda b, i: (0, 0, 0, 0))],
        out_specs=pl.BlockSpec((1, TILE, 512), lambda b, i: (b, i, 0)),
        out_shape=jax.ShapeDtypeStruct((B, S, 512), BF16),
        scratch_shapes=[pltpu.VMEM((512, TILE), F32)],
        compiler_params=pltpu.CompilerParams(dimension_semantics=("arbitrary", "arbitrary"),
                                             vmem_limit_bytes=_vmem_limit(vmem)),
        name="attn_b",
    )(sinks, bqT, bk, bvT, bias)


def _attn_c_kernel(qT_ref, k_ref, vT_ref, iqT_ref, ik_ref, iwT_ref, bias_ref, o_ref,
                   hi_ref, lo_ref, mb_ref, acc_ref, oT_ref, *, top_k):
    i = pl.program_id(1)
    n_t = i + 1
    krow = lax.broadcasted_iota(I32, (TILE, TILE), 0)
    qcol = lax.broadcasted_iota(I32, (TILE, TILE), 1)
    allowed = (krow // CHUNK) <= (qcol // CHUNK)

    def idx_keys(j, diag):
        ikt = ik_ref[0, pl.ds(pl.multiple_of(j * TILE, TILE), TILE), :]
        lgs = [_dot(ikt, iqT_ref[0, hh * IDX_DIM:(hh + 1) * IDX_DIM, :]) for hh in range(IDX_HEADS)]
        sc = jnp.zeros((TILE, TILE), F32)
        for hh in range(IDX_HEADS):
            sc = sc + jnp.maximum(lgs[hh], 0.0) * iwT_ref[0, hh:hh + 1, :]
        if diag:
            sc = jnp.where(allowed, sc, NEG)
        bits = lax.bitcast_convert_type(sc, I32)
        key = bits ^ ((bits >> 31) & 0x7FFFFFFF)
        hi_ref[j] = (key >> 16).astype(I16)
        lo_ref[j] = ((key & 0xFFFF) + I16_MIN).astype(I16)

    def fill_pair(p, carry):
        idx_keys(2 * p, False)
        idx_keys(2 * p + 1, False)
        return carry

    lax.fori_loop(0, i // 2, fill_pair, 0)

    @pl.when(i % 2 == 1)
    def _():
        idx_keys(i - 1, False)

    idx_keys(i, True)

    n_pair = (n_t + 1) // 2

    @pl.when(n_t % 2 == 1)
    def _():
        hi_ref[n_t] = jnp.full((TILE, TILE), I16_MIN, I16)
        lo_ref[n_t] = jnp.full((TILE, TILE), I16_MIN, I16)

    groups = TILE // BF16_ROWS

    def rows16(ref, j):
        return ref[j].reshape(groups, BF16_ROWS, TILE)

    def bcast16(v):
        return jnp.broadcast_to(v, (BF16_ROWS, TILE)).astype(I16)

    def count(pred):
        def body(p, c):
            for u in (0, 1):
                j = 2 * p + u
                hi, lo = rows16(hi_ref, j), rows16(lo_ref, j)
                for r in range(groups):
                    c = c + pred(hi[r], lo[r], j, r).astype(I16)
            return c
        c = lax.fori_loop(0, n_pair, body, jnp.zeros((BF16_ROWS, TILE), I16))
        return jnp.sum(c.astype(I32), axis=0, keepdims=True)

    def search(n_bits, accept):
        def step(b, t):
            cand = t + lax.shift_left(jnp.int32(1), n_bits - 1 - b)
            return jnp.where(accept(cand), cand, t)
        return lax.fori_loop(0, n_bits, step, jnp.full((1, TILE), I16_MIN, I32))

    def hi_accept(cand):
        c16 = bcast16(cand)
        return count(lambda hi, lo, j, r: hi >= c16) >= top_k

    t_hi = search(16, hi_accept)
    th16 = bcast16(t_hi)
    cnt_above = count(lambda hi, lo, j, r: hi > th16)
    r_lo = top_k - cnt_above

    def bucket_only(p, carry):
        for u in (0, 1):
            j = 2 * p + u
            hi, lo = rows16(hi_ref, j), rows16(lo_ref, j)
            for r in range(groups):
                lo_ref[j, r * BF16_ROWS:(r + 1) * BF16_ROWS, :] = jnp.where(hi[r] == th16, lo[r], I16_MIN)
        return carry

    lax.fori_loop(0, n_pair, bucket_only, 0)

    def lo_accept(cand):
        c16 = bcast16(cand)
        return count(lambda hi, lo, j, r: lo >= c16) >= r_lo

    t_lo = search(16, lo_accept)
    tl16 = bcast16(t_lo)
    cnt_gt = count(lambda hi, lo, j, r: lo > tl16)
    r_eq = (r_lo - cnt_gt).astype(F32)

    tri = (krow >= qcol).astype(BF16)
    one, zero = jnp.ones((), BF16), jnp.zeros((), BF16)

    def mask_tile(j, ties_before, diag):
        hi, lo = rows16(hi_ref, j), rows16(lo_ref, j)
        eq = jnp.concatenate([jnp.where((lo[r] == tl16) & (hi[r] == th16), one, zero) for r in range(groups)], axis=0)
        gt = jnp.concatenate([jnp.where((hi[r] > th16) | (lo[r] > tl16), one, zero) for r in range(groups)], axis=0)
        ties = _dot(tri, eq) + ties_before
        sel = (gt.astype(F32) > 0.5) | ((eq.astype(F32) > 0.5) & (ties <= r_eq))
        if diag:
            sel = sel & allowed
        mb_ref[j] = jnp.where(sel, 0.0, NEG)
        return ties[TILE - 1:TILE, :]

    ties_before = lax.fori_loop(0, i // 2, lambda p, c: mask_tile(2 * p + 1, mask_tile(2 * p, c, False), False),
                                jnp.zeros((1, TILE), F32))
    ties_before = lax.cond(i % 2 == 1, lambda c: mask_tile(i - 1, c, False), lambda c: c, ties_before)
    mask_tile(i, ties_before, True)

    acc_ref[...] = jnp.zeros_like(acc_ref)

    def scores(j, d, h):
        kt = k_ref[0, pl.ds(pl.multiple_of(j * TILE, TILE), TILE), :]
        s = _dot(kt, qT_ref[0, h * HEAD_DIM:(h + 1) * HEAD_DIM, :]) + mb_ref[j]
        return s if d is None else s + bias_ref[h, d]

    _sweep_key_tiles(i, C_HEADS, scores, lambda j, h: vT_ref[0, j], acc_ref)

    for h in range(C_HEADS):
        a = acc_ref[h]
        oT_ref[h * HEAD_DIM:(h + 1) * HEAD_DIM, :] = a[:HEAD_DIM, :] * (1.0 / a[HEAD_DIM:HEAD_DIM + 1, :])
    o_ref[0] = oT_ref[...].T.astype(BF16)


def _attn_c(cqT, ck, cvT, iqT, ik, iwT, bias, top_k):
    B, _, S = cqT.shape
    nt = S // TILE
    vmem = 2 * (512 * TILE * 2 + 2 * S * LANES * 2 + S * HEAD_DIM * 2 + LANES * TILE * 2 + 8 * TILE * 4
                + bias.size * 4 + TILE * 512 * 2) + 2 * S * TILE * 4 + 48 * TILE * TILE * 4
    return pl.pallas_call(
        functools.partial(_attn_c_kernel, top_k=top_k),
        grid=(B, nt),
        in_specs=[pl.BlockSpec((1, 512, TILE), lambda b, i: (b, 0, i)),
                  pl.BlockSpec((1, S, HEAD_DIM), lambda b, i: (b, 0, 0)),
                  pl.BlockSpec((1, nt, C_VT_ROWS, TILE), lambda b, i: (b, 0, 0, 0)),
                  pl.BlockSpec((1, LANES, TILE), lambda b, i: (b, 0, i)),
                  pl.BlockSpec((1, S, IDX_DIM), lambda b, i: (b, 0, 0)),
                  pl.BlockSpec((1, IDX_HEADS, TILE), lambda b, i: (b, 0, i)),
                  pl.BlockSpec(bias.shape, lambda b, i: (0, 0, 0, 0))],
        out_specs=pl.BlockSpec((1, TILE, 512), lambda b, i: (b, i, 0)),
        out_shape=jax.ShapeDtypeStruct((B, S, 512), BF16),
        scratch_shapes=[pltpu.VMEM((nt + nt % 2, TILE, TILE), I16), pltpu.VMEM((nt + nt % 2, TILE, TILE), I16),
                        pltpu.VMEM((nt, TILE, TILE), F32),
                        pltpu.VMEM((C_HEADS, C_VT_ROWS, TILE), F32), pltpu.VMEM((512, TILE), F32)],
        compiler_params=pltpu.CompilerParams(dimension_semantics=("arbitrary", "arbitrary"),
                                             vmem_limit_bytes=_vmem_limit(vmem)),
        name="attn_c",
    )(cqT, ck, cvT, iqT, ik, iwT, bias)


def _merge_kernel(x_ref, oa_ref, ob_ref, oc_ref, gmix_ref, wg_ref, bg_ref, wb_ref, wo_ref, gffn_ref,
                  wrh_ref, wrl_ref, br_ref, x1_ref, h2_ref, gate_ref):
    x = x_ref[...]
    hb = _rms(x, gmix_ref[...]).astype(BF16)
    z = None
    for n, o_ref in enumerate((oa_ref, ob_ref, oc_ref)):
        cs = slice(n * D_MODEL, (n + 1) * D_MODEL)
        gate = _sigmoid(_dot(hb, wg_ref[:, cs]) + bg_ref[:, cs])
        y = _dot(o_ref[...], wb_ref[n])
        z = gate * y if z is None else z + gate * y
    x1 = x + _dot(z.astype(BF16), wo_ref[...])
    x1_ref[...] = x1
    h2 = _rms(x1, gffn_ref[...])
    h2_ref[...] = h2.astype(BF16)

    hi, lo = _split_bf16(h2)
    lg = _dot(hi, wrh_ref[...]) + _dot(lo, wrh_ref[...]) + _dot(hi, wrl_ref[...]) + br_ref[...]
    col = lax.broadcasted_iota(I32, lg.shape, 1).astype(F32)
    big = float(4 * LANES)
    is_g = (col >= N_EXPERTS) & (col < N_EXPERTS + N_GROUPS)
    gl = jnp.where(is_g, lg, -jnp.inf)
    gmax = jnp.max(gl, axis=-1, keepdims=True)
    p_group = 1.0 / jnp.sum(jnp.exp(gl - gmax), axis=-1, keepdims=True)
    g_sel = jnp.min(jnp.where(gl == gmax, col, big), axis=-1, keepdims=True) - N_EXPERTS
    in_g = (col >= g_sel * EXPERTS_PER_GROUP) & (col < (g_sel + 1) * EXPERTS_PER_GROUP)
    el = jnp.where(in_g, lg, -jnp.inf)
    e1 = jnp.max(el, axis=-1, keepdims=True)
    i1 = jnp.min(jnp.where(el == e1, col, big), axis=-1, keepdims=True)
    el2 = jnp.where(col == i1, -jnp.inf, el)
    e2 = jnp.max(el2, axis=-1, keepdims=True)
    i2 = jnp.min(jnp.where(el2 == e2, col, big), axis=-1, keepdims=True)
    t2 = jnp.exp(e2 - e1)
    w1 = p_group / (1.0 + t2)
    w2 = w1 * t2
    gates = jnp.where(col == i1, w1, 0.0) + jnp.where(col == i2, w2, 0.0)
    gate_ref[...] = gates[:, :N_EXPERTS]


def _merge(x2, oa, ob, oc, gmix, wg, bg, wb, wo, gffn, wrh, wrl, br, tm):
    T, D = x2.shape
    full = lambda a: pl.BlockSpec(a.shape, lambda t: (0,) * a.ndim)
    rowb = lambda c: pl.BlockSpec((tm, c), lambda t: (t, 0))
    vmem = 2 * (tm * D * 4 * 2 + 3 * tm * 512 * 2 + tm * D * 2 + wg.size * 2 + wb.size * 2 + wo.size * 2
                + 2 * D * LANES * 2) + 10 * tm * D * 4
    return pl.pallas_call(
        _merge_kernel,
        grid=(T // tm,),
        in_specs=[rowb(D), rowb(512), rowb(512), rowb(512), full(gmix), full(wg), full(bg), full(wb),
                  full(wo), full(gffn), full(wrh), full(wrl), full(br)],
        out_specs=[rowb(D), rowb(D), rowb(N_EXPERTS)],
        out_shape=[jax.ShapeDtypeStruct((T, D), F32), jax.ShapeDtypeStruct((T, D), BF16),
                   jax.ShapeDtypeStruct((T, N_EXPERTS), F32)],
        compiler_params=pltpu.CompilerParams(dimension_semantics=("arbitrary",),
                                             vmem_limit_bytes=_vmem_limit(vmem)),
        name="merge",
    )(x2, oa, ob, oc, gmix, wg, bg, wb, wo, gffn, wrh, wrl, br)


def _moe_kernel(x1_ref, h2_ref, gate_ref, wgu_ref, wd_ref, o_ref):
    g = pl.program_id(1)
    h2 = h2_ref[...]
    gate = gate_ref[...]
    lane = lax.broadcasted_iota(I32, gate.shape, 1)
    hids = []
    for u in range(EXPERTS_PER_GROUP):
        gu = _dot(h2, wgu_ref[u])
        a = gu[:, :EXPERT_FF]
        w = jnp.sum(jnp.where(lane == g * EXPERTS_PER_GROUP + u, gate, 0.0), axis=-1, keepdims=True)
        hids.append((a * _sigmoid(a) * gu[:, EXPERT_FF:] * w).astype(BF16))
    y = _dot(jnp.concatenate(hids, axis=1), wd_ref[...])

    @pl.when(g == 0)
    def _():
        o_ref[...] = x1_ref[...] + y

    @pl.when(g > 0)
    def _():
        o_ref[...] += y


def _moe(x1, h2, gate, wgu, wd, tm):
    T, D = x1.shape
    n_e, ff = EXPERTS_PER_GROUP, EXPERT_FF
    vmem = 2 * (tm * D * 4 * 2 + tm * D * 2 + tm * LANES * 4 + n_e * D * 2 * ff * 2 + n_e * ff * D * 2) \
        + 3 * n_e * tm * 2 * ff * 4 + 2 * tm * D * 4
    return pl.pallas_call(
        _moe_kernel,
        grid=(T // tm, N_GROUPS),
        in_specs=[pl.BlockSpec((tm, D), lambda t, g: (t, 0)),
                  pl.BlockSpec((tm, D), lambda t, g: (t, 0)),
                  pl.BlockSpec((tm, N_EXPERTS), lambda t, g: (t, 0)),
                  pl.BlockSpec((n_e, D, 2 * ff), lambda t, g: (g, 0, 0)),
                  pl.BlockSpec((n_e * ff, D), lambda t, g: (g, 0))],
        out_specs=pl.BlockSpec((tm, D), lambda t, g: (t, 0)),
        out_shape=jax.ShapeDtypeStruct((T, D), F32),
        compiler_params=pltpu.CompilerParams(dimension_semantics=("arbitrary", "arbitrary"),
                                             vmem_limit_bytes=_vmem_limit(vmem)),
        name="moe",
    )(x1, h2, gate, wgu, wd)


def _t5_bucket_np(rel):
    half = NUM_BUCKETS // 2
    max_exact = half // 2
    n = np.abs(rel)
    n_f = np.maximum(n, 1).astype(np.float32)
    large = max_exact + (np.log(n_f / np.float32(max_exact)) / np.float32(math.log(MAX_DISTANCE / max_exact))
                         * np.float32(half - max_exact)).astype(np.int32)
    large = np.minimum(large, half - 1)
    return np.where(rel > 0, half, 0) + np.where(n < max_exact, n, large)


def _toeplitz_kernel(v_ref, o_ref):
    v = v_ref[0]
    for d in range(NEAR_TILES):
        x = jnp.broadcast_to(v[d:d + 1, :], (TILE, 2 * TILE))
        o_ref[0, d] = pltpu.roll(x, 0, 1, stride=1, stride_axis=0)[:, :TILE]


def _toeplitz(vals):
    n_heads = vals.shape[0]
    return pl.pallas_call(
        _toeplitz_kernel,
        grid=(n_heads,),
        in_specs=[pl.BlockSpec((1, NEAR_TILES, 2 * TILE), lambda h: (h, 0, 0))],
        out_specs=pl.BlockSpec((1, NEAR_TILES, TILE, TILE), lambda h: (h, 0, 0, 0)),
        out_shape=jax.ShapeDtypeStruct((n_heads, NEAR_TILES, TILE, TILE), F32),
        name="bias_tiles",
    )(vals)


def _bias_tables(rel_bias):
    u = np.arange(2 * TILE)
    off = np.where(u < TILE, u, u - 2 * TILE)
    rel = np.stack([-off - TILE * d for d in range(NEAR_TILES)])
    onehot = (_t5_bucket_np(rel)[..., None] == np.arange(NUM_BUCKETS)).astype(np.float32)
    vals = jnp.einsum("dub,bh->hdu", jnp.asarray(onehot), rel_bias, precision=lax.Precision.HIGHEST)
    tab = _toeplitz(vals)

    far = rel_bias[NUM_BUCKETS // 2 - 1]
    kk = np.arange(TILE)[:, None]
    qc = np.arange(TILE)[None, :] // CHUNK
    kc = np.stack([(kk // CHUNK) - (TILE // CHUNK) * d for d in range(NEAR_TILES)]) + 0 * qc
    a_ok = jnp.asarray(kc <= qc)
    ta = jnp.where(a_ok[None], (tab[:A_HEADS] - far[:A_HEADS, None, None, None]) * LOG2E, NEG)
    b_ok = jnp.asarray((qc - kc[:2] >= 0) & (qc - kc[:2] <= W_CHUNKS))
    tb = jnp.where(b_ok[None], tab[A_HEADS:A_HEADS + B_HEADS, :2], NEG)
    tb = jnp.concatenate([tb, jnp.full((B_HEADS, 1, TILE, TILE), NEG, F32)], axis=1)
    tc = (tab[A_HEADS + B_HEADS:] - far[A_HEADS + B_HEADS:, None, None, None]) * LOG2E
    return ta.astype(F32), tb.astype(F32), tc.astype(F32)


def _proj_weight_kernel(w_ref, o_ref):
    o_ref[...] = jnp.zeros_like(o_ref)
    o_ref[:, :_IN_COLS] = w_ref[0].astype(BF16)


def _proj_weight(w_in, l):
    _, d, n = w_in.shape
    assert n == _IN_COLS
    rows = 128
    return pl.pallas_call(
        _proj_weight_kernel,
        grid=(d // rows,),
        in_specs=[pl.BlockSpec((1, rows, n), lambda r: (l, r, 0))],
        out_specs=pl.BlockSpec((rows, _W_COLS), lambda r: (r, 0)),
        out_shape=jax.ShapeDtypeStruct((d, _W_COLS), BF16),
        name="proj_weight",
    )(w_in)


def kernel(x, rel_bias, norm_mix_g, w_in, qk_norm_g, diff_lambda, diff_subln_g, sinks, w_branch, w_gate, b_gate,
           w_out, norm_ffn_g, w_router_group, b_router_group, w_router_expert, b_router_expert, w_ff_gate,
           w_ff_up, w_ff_down):
    B, S, D = x.shape
    assert D == D_MODEL and S % TILE == 0
    T = B * S
    top_k = min(TOPK_MAX, S // 4)
    tm_proj = 512 if S % 512 == 0 else TILE
    tm_merge = 512 if T % 512 == 0 else TILE
    tm_moe = 1024 if T % 1024 == 0 else TILE

    bias_a, bias_b, bias_c = _bias_tables(rel_bias)
    seg = jnp.asarray(np.kron(np.eye(512 // HEAD_DIM), np.ones((HEAD_DIM, HEAD_DIM))), BF16)
    q_scale = HEAD_DIM ** -0.5

    for l in range(DEPTH):
        lambda_init = 0.8 - 0.6 * math.exp(-0.3 * l)
        qg = qk_norm_g[l]
        tile8 = lambda g: jnp.tile(g, 512 // HEAD_DIM)
        gains = ((tile8(qg[0, 0]) * (q_scale * LOG2E))[:, None], tile8(qg[0, 1])[None, :],
                 (tile8(qg[1, 0]) * q_scale)[:, None], jnp.tile(qg[1, 1], B_KV_HEADS)[None, :],
                 (tile8(qg[2, 0]) * (q_scale * LOG2E))[:, None], qg[2, 1][None, :])
        (aqT, ak, avT, bqT, bk, bvT, cqT, ck, cvT, iqT, ik, iwT) = _proj(
            x, norm_mix_g[l][None, :], _proj_weight(w_in, l), seg, gains, tm_proj)

        oa = _attn_a(aqT, ak, avT, bias_a, diff_lambda[l], diff_subln_g[l][None, :], lambda_init)
        ob = _attn_b(bqT, bk, bvT, bias_b, sinks[l])
        oc = _attn_c(cqT, ck, cvT, iqT, ik, iwT, bias_c, top_k)

        w_r = jnp.concatenate([w_router_expert[l], w_router_group[l],
                               jnp.zeros((D, LANES - N_EXPERTS - N_GROUPS), F32)], axis=1)
        b_r = jnp.concatenate([b_router_expert[l], b_router_group[l],
                               jnp.zeros((LANES - N_EXPERTS - N_GROUPS,), F32)])[None, :]
        wrh = w_r.astype(BF16)
        wrl = (w_r - wrh.astype(F32)).astype(BF16)
        x1, h2, gate = _merge(
            x.reshape(T, D), oa.reshape(T, 512), ob.reshape(T, 512), oc.reshape(T, 512),
            norm_mix_g[l][None, :], w_gate[l].astype(BF16), b_gate[l][None, :], w_branch[l].astype(BF16),
            w_out[l].astype(BF16), norm_ffn_g[l][None, :], wrh, wrl, b_r, tm_merge)

        wgu = jnp.concatenate([w_ff_gate[l], w_ff_up[l]], axis=-1).astype(BF16)
        wd = w_ff_down[l].astype(BF16).reshape(N_EXPERTS * EXPERT_FF, D)
        x = _moe(x1, h2, gate, wgu, wd, tm_moe).reshape(B, S, D)
    return x
```

```python
import functools
import math

import numpy as np
import jax
import jax.numpy as jnp
from jax import lax
from jax.experimental import pallas as pl
from jax.experimental.pallas import tpu as pltpu

F32 = jnp.float32
BF16 = jnp.bfloat16
I32 = jnp.int32
I16 = jnp.int16

D_MODEL = 1024
DEPTH = 2
CHUNK = 64
HEAD_DIM = 64
A_HEADS = 4
A_V_DIM = 2 * HEAD_DIM
B_HEADS = 8
B_KV_HEADS = 2
B_GROUP = B_HEADS // B_KV_HEADS
W_CHUNKS = 2
C_HEADS = 8
IDX_HEADS = 4
IDX_DIM = 32
TOPK_MAX = 256
NUM_BUCKETS = 32
MAX_DISTANCE = 1024
N_BRANCH = 3
BRANCH_WIDTH = 512
N_GROUPS = 4
EXPERTS_PER_GROUP = 4
N_EXPERTS = N_GROUPS * EXPERTS_PER_GROUP
EXPERT_FF = 256
EPS = 1e-6
NEG = -1e30
I16_MIN = -(2 ** 15)

LANES = 128
SUBLANES = 8
TILE = 256
NEAR_TILES = 4
BF16_ROWS = 16
A_VT_ROWS = A_V_DIM + BF16_ROWS
C_VT_ROWS = HEAD_DIM + BF16_ROWS
LOG2E = 1.4426950408889634
SKEW = 5
VMEM_CAP = 60000 * 1024

_C_AQ, _C_AK, _C_AV, _C_BQ, _C_BK, _C_BV, _C_CQ = 0, 512, 1024, 1536, 2048, 2176, 2304
_C_CKV, _C_IQ, _C_IKW = 2816, 2944, 3072
_IN_COLS = 3108
_W_COLS = 3200


def _dot(a, b):
    return jnp.dot(a, b, preferred_element_type=F32)


def _split_bf16(a):
    hi = a.astype(BF16)
    lo = (a - hi.astype(F32)).astype(BF16)
    return hi, lo


def _sigmoid(x):
    return 1.0 / (1.0 + jnp.exp(-x))


def _rms(x, g):
    return x * lax.rsqrt(jnp.mean(x * x, axis=-1, keepdims=True) + EPS) * g


def _colmax8(s):
    r, c = s.shape
    return jnp.max(s.reshape(r // SUBLANES, SUBLANES, c), axis=0)


def _colsum8(s):
    r, c = s.shape
    return jnp.sum(s.reshape(r // SUBLANES, SUBLANES, c), axis=0)


def _vmem_limit(nbytes):
    return int(min(VMEM_CAP, nbytes))


def _proj_kernel(x_ref, g_ref, w_ref, seg_ref, gaq_ref, gak_ref, gbq_ref, gbk_ref, gcq_ref, gck_ref,
                 aqT_ref, ak_ref, avT_ref, bqT_ref, bk_ref, bvT_ref, cqT_ref, ck_ref, cvT_ref,
                 iqT_ref, ik_ref, iwT_ref, *, tm, iw_scale):
    hb = _rms(x_ref[0], g_ref[...]).astype(BF16)
    seg = seg_ref[...]
    n_sub = tm // TILE

    def grp(a, n):
        return _dot(hb, w_ref[:, a:a + n])

    def segnorm(t, g):
        n = t.shape[1]
        ssq = _dot((t * t).astype(BF16), seg[:n, :n])
        return t * lax.rsqrt(ssq * (1.0 / HEAD_DIM) + EPS) * g

    def segnorm_t(t, gcol):
        n = t.shape[0] // HEAD_DIM
        t3 = t.reshape(n, HEAD_DIM, tm)
        ssq = jnp.sum(t3 * t3, axis=1, keepdims=True)
        return (t3 * lax.rsqrt(ssq * (1.0 / HEAD_DIM) + EPS)).reshape(t.shape) * gcol

    ones_rows = (lax.broadcasted_iota(I32, (BF16_ROWS, tm), 0) == 0).astype(F32)

    def put_slabs(ref, tT):
        for s in range(n_sub):
            ref[0, s] = tT[:, s * TILE:(s + 1) * TILE].astype(BF16)

    aqT_ref[0] = segnorm_t(grp(_C_AQ, 512).T, gaq_ref[...]).astype(BF16)
    ak_ref[0] = segnorm(grp(_C_AK, 512), gak_ref[...]).astype(BF16)
    avT = grp(_C_AV, 512).T
    put_slabs(avT_ref, jnp.concatenate(
        [p for h in range(A_HEADS) for p in (avT[h * A_V_DIM:(h + 1) * A_V_DIM, :], ones_rows)], axis=0))
    bqT_ref[0] = segnorm_t(grp(_C_BQ, 512).T, gbq_ref[...]).astype(BF16)
    cqT_ref[0] = segnorm_t(grp(_C_CQ, 512).T, gcq_ref[...]).astype(BF16)
    bk_ref[0] = segnorm(grp(_C_BK, LANES), gbk_ref[...]).astype(BF16)
    bvT = grp(_C_BV, LANES).T
    put_slabs(bvT_ref, jnp.concatenate(
        [p for g in range(B_KV_HEADS) for p in (bvT[g * HEAD_DIM:(g + 1) * HEAD_DIM, :], ones_rows)], axis=0))
    iqT_ref[0] = grp(_C_IQ, LANES).T.astype(BF16)
    ckv = grp(_C_CKV, LANES)
    ck = ckv[:, :HEAD_DIM]
    ssq = jnp.sum(ck * ck, axis=-1, keepdims=True)
    ck_ref[0] = (ck * lax.rsqrt(ssq * (1.0 / HEAD_DIM) + EPS) * gck_ref[...]).astype(BF16)
    put_slabs(cvT_ref, jnp.concatenate([ckv.T[HEAD_DIM:, :], ones_rows], axis=0))
    ikw = grp(_C_IKW, LANES)
    ik_ref[0] = ikw[:, :IDX_DIM].astype(BF16)
    iwT_ref[0] = ikw.T[IDX_DIM:IDX_DIM + IDX_HEADS, :] * iw_scale


def _proj(x, g, w, seg, gains, tm):
    B, S, D = x.shape
    nt = S // TILE
    n_sub = tm // TILE
    full = lambda shape: pl.BlockSpec(shape, lambda b, t: (0,) * len(shape))
    out_shape = [
        jax.ShapeDtypeStruct((B, 512, S), BF16),
        jax.ShapeDtypeStruct((B, S, 512), BF16),
        jax.ShapeDtypeStruct((B, nt, A_HEADS * A_VT_ROWS, TILE), BF16),
        jax.ShapeDtypeStruct((B, 512, S), BF16),
        jax.ShapeDtypeStruct((B, S, LANES), BF16),
        jax.ShapeDtypeStruct((B, nt, B_KV_HEADS * C_VT_ROWS, TILE), BF16),
        jax.ShapeDtypeStruct((B, 512, S), BF16),
        jax.ShapeDtypeStruct((B, S, HEAD_DIM), BF16),
        jax.ShapeDtypeStruct((B, nt, C_VT_ROWS, TILE), BF16),
        jax.ShapeDtypeStruct((B, LANES, S), BF16),
        jax.ShapeDtypeStruct((B, S, IDX_DIM), BF16),
        jax.ShapeDtypeStruct((B, IDX_HEADS, S), F32),
    ]
    colT = lambda r: pl.BlockSpec((1, r, tm), lambda b, t: (b, 0, t))
    row = lambda c: pl.BlockSpec((1, tm, c), lambda b, t: (b, t, 0))
    slab = lambda r: pl.BlockSpec((1, n_sub, r, TILE), lambda b, t: (b, t, 0, 0))
    out_specs = [colT(512), row(512), slab(A_HEADS * A_VT_ROWS), colT(512),
                 row(LANES),
                 slab(B_KV_HEADS * C_VT_ROWS), colT(512), row(HEAD_DIM), slab(C_VT_ROWS), colT(LANES), row(IDX_DIM),
                 pl.BlockSpec((1, IDX_HEADS, tm), lambda b, t: (b, 0, t))]
    in_specs = [pl.BlockSpec((1, tm, D), lambda b, t: (b, t, 0)), full((1, D)), full((D, _W_COLS)),
                full((512, 512))] + [full(gn.shape) for gn in gains]
    vmem = 2 * (tm * D * 4 + D * _W_COLS * 2 + 512 * 512 * 2 + tm * 3400 * 2) + 24 * tm * 512 * 4
    return pl.pallas_call(
        functools.partial(_proj_kernel, tm=tm, iw_scale=IDX_HEADS ** -0.5 * IDX_DIM ** -0.5),
        grid=(B, S // tm), in_specs=in_specs, out_specs=out_specs, out_shape=out_shape,
        compiler_params=pltpu.CompilerParams(dimension_semantics=("arbitrary", "arbitrary"),
                                             vmem_limit_bytes=_vmem_limit(vmem)),
        name="proj",
    )(x, g, w, seg, *gains)


def _online_step(s, m_old, vt, acc_ref, ch):
    m_new = jnp.maximum(m_old, jnp.max(_colmax8(s), axis=0, keepdims=True))
    alpha = jnp.exp2(m_old - m_new)
    e = jnp.exp2(s - m_new).astype(BF16)
    acc_ref[ch] = acc_ref[ch] * alpha + _dot(vt, e)
    return m_new


def _sweep_key_tiles(i, n_chain, scores, vt, acc_ref):
    n_far = jnp.maximum(i - (NEAR_TILES - 1), 0)

    def step(tiles, ms):
        ms = list(ms)
        chains = [(j, d, ch) for (j, d) in tiles for ch in range(n_chain)]
        pending = [scores(*c) for c in chains[:SKEW]]
        for n, (j, d, ch) in enumerate(chains):
            s = pending.pop(0)
            if n + SKEW < len(chains):
                pending.append(scores(*chains[n + SKEW]))
            ms[ch] = _online_step(s, ms[ch], vt(j, ch), acc_ref, ch)
        return tuple(ms)

    def near_pairs(ms):
        ms = step([(i - 3, 3), (i - 2, 2)], ms)
        return step([(i - 1, 1), (i, 0)], ms)

    def near_singles(ms):
        return lax.fori_loop(0, i + 1, lambda t, ms: step([(t, i - t)], ms), ms)

    assert NEAR_TILES == 4
    ms = tuple(jnp.full((1, TILE), -jnp.inf, F32) for _ in range(n_chain))
    ms = lax.fori_loop(0, n_far // 4, lambda p, ms: step([(4 * p + u, None) for u in range(4)], ms), ms)
    rest = n_far - n_far % 4
    ms = lax.cond(n_far % 4 >= 2, lambda ms: step([(rest, None), (rest + 1, None)], ms), lambda ms: ms, ms)
    ms = lax.cond(n_far % 2 == 1, lambda ms: step([(n_far - 1, None)], ms), lambda ms: ms, ms)
    lax.cond(i >= NEAR_TILES - 1, near_pairs, near_singles, ms)


def _attn_a_kernel(qT_ref, k_ref, vT_ref, bias_ref, lam_ref, sub_ref, o_ref, q2_ref, acc_ref, *, lambda_init):
    i = pl.program_id(1)
    lp = lam_ref[...]
    lam = (jnp.exp(jnp.sum(lp[0:1] * lp[1:2], axis=-1, keepdims=True))
           - jnp.exp(jnp.sum(lp[2:3] * lp[3:4], axis=-1, keepdims=True)) + lambda_init)
    row = lax.broadcasted_iota(I32, (2 * HEAD_DIM, TILE), 0)
    n_chain = 2 * A_HEADS

    for h in range(A_HEADS):
        qh = qT_ref[0, h * A_V_DIM:(h + 1) * A_V_DIM, :]
        zero = jnp.zeros_like(qh)
        q2_ref[2 * h] = jnp.where(row < HEAD_DIM, qh, zero)
        q2_ref[2 * h + 1] = jnp.where(row >= HEAD_DIM, qh, zero)
    acc_ref[...] = jnp.zeros_like(acc_ref)

    def scores(j, d, ch):
        h = ch // 2
        rows = pl.ds(pl.multiple_of(j * TILE, TILE), TILE)
        s = _dot(k_ref[0, rows, h * A_V_DIM:(h + 1) * A_V_DIM], q2_ref[ch])
        return s if d is None else s + bias_ref[h, d]

    def vt(j, ch):
        h = ch // 2
        return vT_ref[0, j, h * A_VT_ROWS:(h + 1) * A_VT_ROWS, :]

    _sweep_key_tiles(i, n_chain, scores, vt, acc_ref)

    for h in range(A_HEADS):
        a0 = acc_ref[2 * h]
        a1 = acc_ref[2 * h + 1]
        r0 = 1.0 / a0[A_V_DIM:A_V_DIM + 1, :]
        r1 = 1.0 / a1[A_V_DIM:A_V_DIM + 1, :]
        outT = a0[:A_V_DIM, :] * r0 - lam * (a1[:A_V_DIM, :] * r1)
        out = _rms(outT.T, sub_ref[...]) * (1.0 - lambda_init)
        o_ref[0, :, h * A_V_DIM:(h + 1) * A_V_DIM] = out.astype(BF16)


def _attn_a(aqT, ak, avT, bias, lam_par, subln_g, lambda_init):
    B, _, S = aqT.shape
    nt = S // TILE
    vt_rows = A_HEADS * A_VT_ROWS
    vmem = 2 * (512 * TILE * 2 + S * 512 * 2 + S * vt_rows * 2 + bias.size * 4 + TILE * 512 * 2) \
        + 8 * A_V_DIM * TILE * 2 + 8 * A_VT_ROWS * TILE * 4 + 48 * TILE * TILE * 4
    return pl.pallas_call(
        functools.partial(_attn_a_kernel, lambda_init=lambda_init),
        grid=(B, nt),
        in_specs=[pl.BlockSpec((1, 512, TILE), lambda b, i: (b, 0, i)),
                  pl.BlockSpec((1, S, 512), lambda b, i: (b, 0, 0)),
                  pl.BlockSpec((1, nt, vt_rows, TILE), lambda b, i: (b, 0, 0, 0)),
                  pl.BlockSpec(bias.shape, lambda b, i: (0, 0, 0, 0)),
                  pl.BlockSpec((4, HEAD_DIM), lambda b, i: (0, 0)),
                  pl.BlockSpec((1, A_V_DIM), lambda b, i: (0, 0))],
        out_specs=pl.BlockSpec((1, TILE, 512), lambda b, i: (b, i, 0)),
        out_shape=jax.ShapeDtypeStruct((B, S, 512), BF16),
        scratch_shapes=[pltpu.VMEM((2 * A_HEADS, A_V_DIM, TILE), BF16),
                        pltpu.VMEM((2 * A_HEADS, A_VT_ROWS, TILE), F32)],
        compiler_params=pltpu.CompilerParams(dimension_semantics=("arbitrary", "arbitrary"),
                                             vmem_limit_bytes=_vmem_limit(vmem)),
        name="attn_a",
    )(aqT, ak, avT, bias, lam_par, subln_g)


def _attn_b_kernel(sink_ref, qT_ref, k_ref, vT_ref, bias_ref, o_ref, oT_ref):
    i = pl.program_id(1)
    jp = jnp.maximum(i - 1, 0)
    p_idx = jnp.where(i > 0, 1, 2)
    cur = pl.ds(pl.multiple_of(i * TILE, TILE), TILE)
    prev = pl.ds(pl.multiple_of(jp * TILE, TILE), TILE)

    def scores(h):
        qh = qT_ref[0, h * HEAD_DIM:(h + 1) * HEAD_DIM, :]
        zero = jnp.zeros_like(qh)
        q2 = jnp.concatenate([qh, zero] if h < B_GROUP else [zero, qh], axis=0)
        return (_dot(k_ref[0, cur, :], q2) + bias_ref[h, 0], _dot(k_ref[0, prev, :], q2) + bias_ref[h, p_idx])

    def finish(h, s):
        sc, sp = s
        gs = slice((h // B_GROUP) * C_VT_ROWS, (h // B_GROUP + 1) * C_VT_ROWS)
        sink = sink_ref[h] * LOG2E
        m = jnp.max(jnp.maximum(_colmax8(sc), _colmax8(sp)), axis=0, keepdims=True)
        m = jnp.maximum(m, sink)
        ec = jnp.exp2(sc - m).astype(BF16)
        ep = jnp.exp2(sp - m).astype(BF16)
        outT = _dot(vT_ref[0, i, gs, :], ec) + _dot(vT_ref[0, jp, gs, :], ep)
        den = outT[HEAD_DIM:HEAD_DIM + 1, :] + jnp.exp2(sink - m)
        oT_ref[h * HEAD_DIM:(h + 1) * HEAD_DIM, :] = outT[:HEAD_DIM, :] * (1.0 / den)

    pending = [scores(h) for h in range(SKEW)]
    for h in range(B_HEADS):
        s = pending.pop(0)
        if h + SKEW < B_HEADS:
            pending.append(scores(h + SKEW))
        finish(h, s)
    o_ref[0] = oT_ref[...].T.astype(BF16)


def _attn_b(bqT, bk, bvT, bias, sinks):
    B, _, S = bqT.shape
    nt = S // TILE
    vmem = 2 * (512 * TILE * 2 + 2 * S * LANES * 2 + S * LANES * 2 + bias.size * 4 + TILE * 512 * 2) \
        + 24 * TILE * TILE * 4
    return pl.pallas_call(
        _attn_b_kernel,
        grid=(B, nt),
        in_specs=[pl.BlockSpec(memory_space=pltpu.SMEM),
                  pl.BlockSpec((1, 512, TILE), lambda b, i: (b, 0, i)),
                  pl.BlockSpec((1, S, LANES), lambda b, i: (b, 0, 0)),
                  pl.BlockSpec((1, nt, B_KV_HEADS * C_VT_ROWS, TILE), lambda b, i: (b, 0, 0, 0)),
                  pl.BlockSpec(bias.shape, lambda b, i: (0, 0, 0, 0))],
        out_specs=pl.BlockSpec((1, TILE, 512), lambda b, i: (b, i, 0)),
        out_shape=jax.ShapeDtypeStruct((B, S, 512), BF16),
        scratch_shapes=[pltpu.VMEM((512, TILE), F32)],
        compiler_params=pltpu.CompilerParams(dimension_semantics=("arbitrary", "arbitrary"),
                                             vmem_limit_bytes=_vmem_limit(vmem)),
        name="attn_b",
    )(sinks, bqT, bk, bvT, bias)


def _attn_c_kernel(qT_ref, k_ref, vT_ref, iqT_ref, ik_ref, iwT_ref, bias_ref, o_ref,
                   hi_ref, lo_ref, mb_ref, acc_ref, oT_ref, *, top_k):
    i = pl.program_id(1)
    n_t = i + 1
    krow = lax.broadcasted_iota(I32, (TILE, TILE), 0)
    qcol = lax.broadcasted_iota(I32, (TILE, TILE), 1)
    allowed = (krow // CHUNK) <= (qcol // CHUNK)

    def idx_keys(j, diag):
        ikt = ik_ref[0, pl.ds(pl.multiple_of(j * TILE, TILE), TILE), :]
        lgs = [_dot(ikt, iqT_ref[0, hh * IDX_DIM:(hh + 1) * IDX_DIM, :]) for hh in range(IDX_HEADS)]
        sc = jnp.zeros((TILE, TILE), F32)
        for hh in range(IDX_HEADS):
            sc = sc + jnp.maximum(lgs[hh], 0.0) * iwT_ref[0, hh:hh + 1, :]
        if diag:
            sc = jnp.where(allowed, sc, NEG)
        bits = lax.bitcast_convert_type(sc, I32)
        key = bits ^ ((bits >> 31) & 0x7FFFFFFF)
        hi_ref[j] = (key >> 16).astype(I16)
        lo_ref[j] = ((key & 0xFFFF) + I16_MIN).astype(I16)

    def fill_pair(p, carry):
        idx_keys(2 * p, False)
        idx_keys(2 * p + 1, False)
        return carry

    lax.fori_loop(0, i // 2, fill_pair, 0)

    @pl.when(i % 2 == 1)
    def _():
        idx_keys(i - 1, False)

    idx_keys(i, True)

    n_pair = (n_t + 1) // 2

    @pl.when(n_t % 2 == 1)
    def _():
        hi_ref[n_t] = jnp.full((TILE, TILE), I16_MIN, I16)
        lo_ref[n_t] = jnp.full((TILE, TILE), I16_MIN, I16)

    groups = TILE // BF16_ROWS

    def rows16(ref, j):
        return ref[j].reshape(groups, BF16_ROWS, TILE)

    def bcast16(v):
        return jnp.broadcast_to(v, (BF16_ROWS, TILE)).astype(I16)

    def count(pred):
        def body(p, cs):
            cs = list(cs)
            for u in (0, 1):
                j = 2 * p + u
                hi, lo = rows16(hi_ref, j), rows16(lo_ref, j)
                for r in range(groups):
                    cs[r % len(cs)] = cs[r % len(cs)] + pred(hi[r], lo[r], j, r).astype(I16)
            return tuple(cs)
        cs = lax.fori_loop(0, n_pair, body, (jnp.zeros((BF16_ROWS, TILE), I16),) * 4)
        c = (cs[0] + cs[1]) + (cs[2] + cs[3])
        return jnp.sum(c.astype(I32), axis=0, keepdims=True)

    def search(n_bits, accept):
        def step(b, t):
            cand = t + lax.shift_left(jnp.int32(1), n_bits - 1 - b)
            return jnp.where(accept(cand), cand, t)
        return lax.fori_loop(0, n_bits, step, jnp.full((1, TILE), I16_MIN, I32))

    def hi_accept(cand):
        c16 = bcast16(cand)
        return count(lambda hi, lo, j, r: hi >= c16) >= top_k

    t_hi = search(16, hi_accept)
    th16 = bcast16(t_hi)
    cnt_above = count(lambda hi, lo, j, r: hi > th16)
    r_lo = top_k - cnt_above

    def bucket_only(p, carry):
        for u in (0, 1):
            j = 2 * p + u
            hi, lo = rows16(hi_ref, j), rows16(lo_ref, j)
            for r in range(groups):
                lo_ref[j, r * BF16_ROWS:(r + 1) * BF16_ROWS, :] = jnp.where(hi[r] == th16, lo[r], I16_MIN)
        return carry

    lax.fori_loop(0, n_pair, bucket_only, 0)

    def lo_accept(cand):
        c16 = bcast16(cand)
        return count(lambda hi, lo, j, r: lo >= c16) >= r_lo

    t_lo = search(16, lo_accept)
    tl16 = bcast16(t_lo)
    cnt_gt = count(lambda hi, lo, j, r: lo > tl16)
    r_eq = (r_lo - cnt_gt).astype(F32)

    tri = (krow >= qcol).astype(BF16)
    one, zero = jnp.ones((), BF16), jnp.zeros((), BF16)

    def mask_tile(j, ties_before, diag):
        hi, lo = rows16(hi_ref, j), rows16(lo_ref, j)
        eq = jnp.concatenate([jnp.where((lo[r] == tl16) & (hi[r] == th16), one, zero) for r in range(groups)], axis=0)
        gt = jnp.concatenate([jnp.where((hi[r] > th16) | (lo[r] > tl16), one, zero) for r in range(groups)], axis=0)
        ties = _dot(tri, eq) + ties_before
        sel = (gt.astype(F32) > 0.5) | ((eq.astype(F32) > 0.5) & (ties <= r_eq))
        if diag:
            sel = sel & allowed
        mb_ref[j] = jnp.where(sel, 0.0, NEG)
        return ties[TILE - 1:TILE, :]

    ties_before = lax.fori_loop(0, i // 2, lambda p, c: mask_tile(2 * p + 1, mask_tile(2 * p, c, False), False),
                                jnp.zeros((1, TILE), F32))
    ties_before = lax.cond(i % 2 == 1, lambda c: mask_tile(i - 1, c, False), lambda c: c, ties_before)
    mask_tile(i, ties_before, True)

    acc_ref[...] = jnp.zeros_like(acc_ref)

    def scores(j, d, h):
        kt = k_ref[0, pl.ds(pl.multiple_of(j * TILE, TILE), TILE), :]
        s = _dot(kt, qT_ref[0, h * HEAD_DIM:(h + 1) * HEAD_DIM, :]) + mb_ref[j]
        return s if d is None else s + bias_ref[h, d]

    _sweep_key_tiles(i, C_HEADS, scores, lambda j, h: vT_ref[0, j], acc_ref)

    for h in range(C_HEADS):
        a = acc_ref[h]
        oT_ref[h * HEAD_DIM:(h + 1) * HEAD_DIM, :] = a[:HEAD_DIM, :] * (1.0 / a[HEAD_DIM:HEAD_DIM + 1, :])
    o_ref[0] = oT_ref[...].T.astype(BF16)


def _attn_c(cqT, ck, cvT, iqT, ik, iwT, bias, top_k):
    B, _, S = cqT.shape
    nt = S // TILE
    vmem = 2 * (512 * TILE * 2 + 2 * S * LANES * 2 + S * HEAD_DIM * 2 + LANES * TILE * 2 + 8 * TILE * 4
                + bias.size * 4 + TILE * 512 * 2) + 2 * S * TILE * 4 + 48 * TILE * TILE * 4
    return pl.pallas_call(
        functools.partial(_attn_c_kernel, top_k=top_k),
        grid=(B, nt),
        in_specs=[pl.BlockSpec((1, 512, TILE), lambda b, i: (b, 0, i)),
                  pl.BlockSpec((1, S, HEAD_DIM), lambda b, i: (b, 0, 0)),
                  pl.BlockSpec((1, nt, C_VT_ROWS, TILE), lambda b, i: (b, 0, 0, 0)),
                  pl.BlockSpec((1, LANES, TILE), lambda b, i: (b, 0, i)),
                  pl.BlockSpec((1, S, IDX_DIM), lambda b, i: (b, 0, 0)),
                  pl.BlockSpec((1, IDX_HEADS, TILE), lambda b, i: (b, 0, i)),
                  pl.BlockSpec(bias.shape, lambda b, i: (0, 0, 0, 0))],
        out_specs=pl.BlockSpec((1, TILE, 512), lambda b, i: (b, i, 0)),
        out_shape=jax.ShapeDtypeStruct((B, S, 512), BF16),
        scratch_shapes=[pltpu.VMEM((nt + nt % 2, TILE, TILE), I16), pltpu.VMEM((nt + nt % 2, TILE, TILE), I16),
                        pltpu.VMEM((nt, TILE, TILE), F32),
                        pltpu.VMEM((C_HEADS, C_VT_ROWS, TILE), F32), pltpu.VMEM((512, TILE), F32)],
        compiler_params=pltpu.CompilerParams(dimension_semantics=("arbitrary", "arbitrary"),
                                             vmem_limit_bytes=_vmem_limit(vmem)),
        name="attn_c",
    )(cqT, ck, cvT, iqT, ik, iwT, bias)


def _merge_kernel(x_ref, oa_ref, ob_ref, oc_ref, gmix_ref, wg_ref, bg_ref, wb_ref, wo_ref, gffn_ref,
                  wrh_ref, wrl_ref, br_ref, x1_ref, h2_ref, gate_ref):
    x = x_ref[...]
    hb = _rms(x, gmix_ref[...]).astype(BF16)
    z = None
    for n, o_ref in enumerate((oa_ref, ob_ref, oc_ref)):
        cs = slice(n * D_MODEL, (n + 1) * D_MODEL)
        gate = _sigmoid(_dot(hb, wg_ref[:, cs]) + bg_ref[:, cs])
        y = _dot(o_ref[...], wb_ref[n])
        z = gate * y if z is None else z + gate * y
    x1 = x + _dot(z.astype(BF16), wo_ref[...])
    x1_ref[...] = x1
    h2 = _rms(x1, gffn_ref[...])
    h2_ref[...] = h2.astype(BF16)

    hi, lo = _split_bf16(h2)
    lg = _dot(hi, wrh_ref[...]) + _dot(lo, wrh_ref[...]) + _dot(hi, wrl_ref[...]) + br_ref[...]
    col = lax.broadcasted_iota(I32, lg.shape, 1).astype(F32)
    big = float(4 * LANES)
    is_g = (col >= N_EXPERTS) & (col < N_EXPERTS + N_GROUPS)
    gl = jnp.where(is_g, lg, -jnp.inf)
    gmax = jnp.max(gl, axis=-1, keepdims=True)
    p_group = 1.0 / jnp.sum(jnp.exp(gl - gmax), axis=-1, keepdims=True)
    g_sel = jnp.min(jnp.where(gl == gmax, col, big), axis=-1, keepdims=True) - N_EXPERTS
    in_g = (col >= g_sel * EXPERTS_PER_GROUP) & (col < (g_sel + 1) * EXPERTS_PER_GROUP)
    el = jnp.where(in_g, lg, -jnp.inf)
    e1 = jnp.max(el, axis=-1, keepdims=True)
    i1 = jnp.min(jnp.where(el == e1, col, big), axis=-1, keepdims=True)
    el2 = jnp.where(col == i1, -jnp.inf, el)
    e2 = jnp.max(el2, axis=-1, keepdims=True)
    i2 = jnp.min(jnp.where(el2 == e2, col, big), axis=-1, keepdims=True)
    t2 = jnp.exp(e2 - e1)
    w1 = p_group / (1.0 + t2)
    w2 = w1 * t2
    gates = jnp.where(col == i1, w1, 0.0) + jnp.where(col == i2, w2, 0.0)
    gate_ref[...] = gates[:, :N_EXPERTS]


def _merge(x2, oa, ob, oc, gmix, wg, bg, wb, wo, gffn, wrh, wrl, br, tm):
    T, D = x2.shape
    full = lambda a: pl.BlockSpec(a.shape, lambda t: (0,) * a.ndim)
    rowb = lambda c: pl.BlockSpec((tm, c), lambda t: (t, 0))
    vmem = 2 * (tm * D * 4 * 2 + 3 * tm * 512 * 2 + tm * D * 2 + wg.size * 2 + wb.size * 2 + wo.size * 2
                + 2 * D * LANES * 2) + 10 * tm * D * 4
    return pl.pallas_call(
        _merge_kernel,
        grid=(T // tm,),
        in_specs=[rowb(D), rowb(512), rowb(512), rowb(512), full(gmix), full(wg), full(bg), full(wb),
                  full(wo), full(gffn), full(wrh), full(wrl), full(br)],
        out_specs=[rowb(D), rowb(D), rowb(N_EXPERTS)],
        out_shape=[jax.ShapeDtypeStruct((T, D), F32), jax.ShapeDtypeStruct((T, D), BF16),
                   jax.ShapeDtypeStruct((T, N_EXPERTS), F32)],
        compiler_params=pltpu.CompilerParams(dimension_semantics=("arbitrary",),
                                             vmem_limit_bytes=_vmem_limit(vmem)),
        name="merge",
    )(x2, oa, ob, oc, gmix, wg, bg, wb, wo, gffn, wrh, wrl, br)


def _moe_kernel(x1_ref, h2_ref, gate_ref, wgu_ref, wd_ref, o_ref):
    g = pl.program_id(1)
    h2 = h2_ref[...]
    gate = gate_ref[...]
    lane = lax.broadcasted_iota(I32, gate.shape, 1)
    hids = []
    for u in range(EXPERTS_PER_GROUP):
        gu = _dot(h2, wgu_ref[u])
        a = gu[:, :EXPERT_FF]
        w = jnp.sum(jnp.where(lane == g * EXPERTS_PER_GROUP + u, gate, 0.0), axis=-1, keepdims=True)
        hids.append((a * _sigmoid(a) * gu[:, EXPERT_FF:] * w).astype(BF16))
    y = _dot(jnp.concatenate(hids, axis=1), wd_ref[...])

    @pl.when(g == 0)
    def _():
        o_ref[...] = x1_ref[...] + y

    @pl.when(g > 0)
    def _():
        o_ref[...] += y


def _moe(x1, h2, gate, wgu, wd, tm):
    T, D = x1.shape
    n_e, ff = EXPERTS_PER_GROUP, EXPERT_FF
    vmem = 2 * (tm * D * 4 * 2 + tm * D * 2 + tm * LANES * 4 + n_e * D * 2 * ff * 2 + n_e * ff * D * 2) \
        + 3 * n_e * tm * 2 * ff * 4 + 2 * tm * D * 4
    return pl.pallas_call(
        _moe_kernel,
        grid=(T // tm, N_GROUPS),
        in_specs=[pl.BlockSpec((tm, D), lambda t, g: (t, 0)),
                  pl.BlockSpec((tm, D), lambda t, g: (t, 0)),
                  pl.BlockSpec((tm, N_EXPERTS), lambda t, g: (t, 0)),
                  pl.BlockSpec((n_e, D, 2 * ff), lambda t, g: (g, 0, 0)),
                  pl.BlockSpec((n_e * ff, D), lambda t, g: (g, 0))],
        out_specs=pl.BlockSpec((tm, D), lambda t, g: (t, 0)),
        out_shape=jax.ShapeDtypeStruct((T, D), F32),
        compiler_params=pltpu.CompilerParams(dimension_semantics=("arbitrary", "arbitrary"),
                                             vmem_limit_bytes=_vmem_limit(vmem)),
        name="moe",
    )(x1, h2, gate, wgu, wd)


def _t5_bucket_np(rel):
    half = NUM_BUCKETS // 2
    max_exact = half // 2
    n = np.abs(rel)
    n_f = np.maximum(n, 1).astype(np.float32)
    large = max_exact + (np.log(n_f / np.float32(max_exact)) / np.float32(math.log(MAX_DISTANCE / max_exact))
                         * np.float32(half - max_exact)).astype(np.int32)
    large = np.minimum(large, half - 1)
    return np.where(rel > 0, half, 0) + np.where(n < max_exact, n, large)


def _toeplitz_kernel(v_ref, o_ref):
    v = v_ref[0]
    for d in range(NEAR_TILES):
        x = jnp.broadcast_to(v[d:d + 1, :], (TILE, 2 * TILE))
        o_ref[0, d] = pltpu.roll(x, 0, 1, stride=1, stride_axis=0)[:, :TILE]


def _toeplitz(vals):
    n_heads = vals.shape[0]
    return pl.pallas_call(
        _toeplitz_kernel,
        grid=(n_heads,),
        in_specs=[pl.BlockSpec((1, NEAR_TILES, 2 * TILE), lambda h: (h, 0, 0))],
        out_specs=pl.BlockSpec((1, NEAR_TILES, TILE, TILE), lambda h: (h, 0, 0, 0)),
        out_shape=jax.ShapeDtypeStruct((n_heads, NEAR_TILES, TILE, TILE), F32),
        name="bias_tiles",
    )(vals)


def _bias_tables(rel_bias):
    u = np.arange(2 * TILE)
    off = np.where(u < TILE, u, u - 2 * TILE)
    rel = np.stack([-off - TILE * d for d in range(NEAR_TILES)])
    onehot = (_t5_bucket_np(rel)[..., None] == np.arange(NUM_BUCKETS)).astype(np.float32)
    vals = jnp.einsum("dub,bh->hdu", jnp.asarray(onehot), rel_bias, precision=lax.Precision.HIGHEST)
    tab = _toeplitz(vals)

    far = rel_bias[NUM_BUCKETS // 2 - 1]
    kk = np.arange(TILE)[:, None]
    qc = np.arange(TILE)[None, :] // CHUNK
    kc = np.stack([(kk // CHUNK) - (TILE // CHUNK) * d for d in range(NEAR_TILES)]) + 0 * qc
    a_ok = jnp.asarray(kc <= qc)
    ta = jnp.where(a_ok[None], (tab[:A_HEADS] - far[:A_HEADS, None, None, None]) * LOG2E, NEG)
    b_ok = jnp.asarray((qc - kc[:2] >= 0) & (qc - kc[:2] <= W_CHUNKS))
    tb = jnp.where(b_ok[None], tab[A_HEADS:A_HEADS + B_HEADS, :2] * LOG2E, NEG)
    tb = jnp.concatenate([tb, jnp.full((B_HEADS, 1, TILE, TILE), NEG, F32)], axis=1)
    tc = (tab[A_HEADS + B_HEADS:] - far[A_HEADS + B_HEADS:, None, None, None]) * LOG2E
    return ta.astype(F32), tb.astype(F32), tc.astype(F32)


def _proj_weight_kernel(w_ref, o_ref):
    o_ref[...] = jnp.zeros_like(o_ref)
    o_ref[:, :_IN_COLS] = w_ref[0].astype(BF16)


def _proj_weight(w_in, l):
    _, d, n = w_in.shape
    assert n == _IN_COLS
    rows = 128
    return pl.pallas_call(
        _proj_weight_kernel,
        grid=(d // rows,),
        in_specs=[pl.BlockSpec((1, rows, n), lambda r: (l, r, 0))],
        out_specs=pl.BlockSpec((rows, _W_COLS), lambda r: (r, 0)),
        out_shape=jax.ShapeDtypeStruct((d, _W_COLS), BF16),
        name="proj_weight",
    )(w_in)


def kernel(x, rel_bias, norm_mix_g, w_in, qk_norm_g, diff_lambda, diff_subln_g, sinks, w_branch, w_gate, b_gate,
           w_out, norm_ffn_g, w_router_group, b_router_group, w_router_expert, b_router_expert, w_ff_gate,
           w_ff_up, w_ff_down):
    B, S, D = x.shape
    assert D == D_MODEL and S % TILE == 0
    T = B * S
    top_k = min(TOPK_MAX, S // 4)
    tm_proj = 512 if S % 512 == 0 else TILE
    tm_merge = 512 if T % 512 == 0 else TILE
    tm_moe = 1024 if T % 1024 == 0 else TILE

    bias_a, bias_b, bias_c = _bias_tables(rel_bias)
    seg = jnp.asarray(np.kron(np.eye(512 // HEAD_DIM), np.ones((HEAD_DIM, HEAD_DIM))), BF16)
    q_scale = HEAD_DIM ** -0.5

    for l in range(DEPTH):
        lambda_init = 0.8 - 0.6 * math.exp(-0.3 * l)
        qg = qk_norm_g[l]
        tile8 = lambda g: jnp.tile(g, 512 // HEAD_DIM)
        gains = ((tile8(qg[0, 0]) * (q_scale * LOG2E))[:, None], tile8(qg[0, 1])[None, :],
                 (tile8(qg[1, 0]) * (q_scale * LOG2E))[:, None], jnp.tile(qg[1, 1], B_KV_HEADS)[None, :],
                 (tile8(qg[2, 0]) * (q_scale * LOG2E))[:, None], qg[2, 1][None, :])
        (aqT, ak, avT, bqT, bk, bvT, cqT, ck, cvT, iqT, ik, iwT) = _proj(
            x, norm_mix_g[l][None, :], _proj_weight(w_in, l), seg, gains, tm_proj)

        oa = _attn_a(aqT, ak, avT, bias_a, diff_lambda[l], diff_subln_g[l][None, :], lambda_init)
        ob = _attn_b(bqT, bk, bvT, bias_b, sinks[l])
        oc = _attn_c(cqT, ck, cvT, iqT, ik, iwT, bias_c, top_k)

        w_r = jnp.concatenate([w_router_expert[l], w_router_group[l],
                               jnp.zeros((D, LANES - N_EXPERTS - N_GROUPS), F32)], axis=1)
        b_r = jnp.concatenate([b_router_expert[l], b_router_group[l],
                               jnp.zeros((LANES - N_EXPERTS - N_GROUPS,), F32)])[None, :]
        wrh = w_r.astype(BF16)
        wrl = (w_r - wrh.astype(F32)).astype(BF16)
        x1, h2, gate = _merge(
            x.reshape(T, D), oa.reshape(T, 512), ob.reshape(T, 512), oc.reshape(T, 512),
            norm_mix_g[l][None, :], w_gate[l].astype(BF16), b_gate[l][None, :], w_branch[l].astype(BF16),
            w_out[l].astype(BF16), norm_ffn_g[l][None, :], wrh, wrl, b_r, tm_merge)

        wgu = jnp.concatenate([w_ff_gate[l], w_ff_up[l]], axis=-1).astype(BF16)
        wd = w_ff_down[l].astype(BF16).reshape(N_EXPERTS * EXPERT_FF, D)
        x = _moe(x1, h2, gate, wgu, wd, tm_moe).reshape(B, S, D)
    return x
```

```python
import functools
import math

import numpy as np
import jax
import jax.numpy as jnp
from jax import lax
from jax.experimental import pallas as pl
from jax.experimental.pallas import tpu as pltpu

F32 = jnp.float32
BF16 = jnp.bfloat16
I32 = jnp.int32
I16 = jnp.int16

D_MODEL = 1024
DEPTH = 2
CHUNK = 64
HEAD_DIM = 64
A_HEADS = 4
A_V_DIM = 2 * HEAD_DIM
B_HEADS = 8
B_KV_HEADS = 2
B_GROUP = B_HEADS // B_KV_HEADS
W_CHUNKS = 2
C_HEADS = 8
IDX_HEADS = 4
IDX_DIM = 32
TOPK_MAX = 256
NUM_BUCKETS = 32
MAX_DISTANCE = 1024
N_BRANCH = 3
BRANCH_WIDTH = 512
N_GROUPS = 4
EXPERTS_PER_GROUP = 4
N_EXPERTS = N_GROUPS * EXPERTS_PER_GROUP
EXPERT_FF = 256
EPS = 1e-6
NEG = -1e30
I16_MIN = -(2 ** 15)

LANES = 128
SUBLANES = 8
TILE = 256
NEAR_TILES = 4
BF16_ROWS = 16
A_VT_ROWS = A_V_DIM + BF16_ROWS
C_VT_ROWS = HEAD_DIM + BF16_ROWS
LOG2E = 1.4426950408889634
SKEW = 5
VMEM_CAP = 60000 * 1024

_C_AQ, _C_AK, _C_AV, _C_BQ, _C_BK, _C_BV, _C_CQ = 0, 512, 1024, 1536, 2048, 2176, 2304
_C_CKV, _C_IQ, _C_IKW = 2816, 2944, 3072
_IN_COLS = 3108
_W_COLS = 3200


def _dot(a, b):
    return jnp.dot(a, b, preferred_element_type=F32)


def _split_bf16(a):
    hi = a.astype(BF16)
    lo = (a - hi.astype(F32)).astype(BF16)
    return hi, lo


def _sigmoid(x):
    return 1.0 / (1.0 + jnp.exp(-x))


def _rms(x, g):
    return x * lax.rsqrt(jnp.mean(x * x, axis=-1, keepdims=True) + EPS) * g


def _colmax8(s):
    r, c = s.shape
    return jnp.max(s.reshape(r // SUBLANES, SUBLANES, c), axis=0)


def _colsum8(s):
    r, c = s.shape
    return jnp.sum(s.reshape(r // SUBLANES, SUBLANES, c), axis=0)


def _vmem_limit(nbytes):
    return int(min(VMEM_CAP, nbytes))


def _proj_kernel(x_ref, g_ref, w_ref, seg_ref, gaq_ref, gak_ref, gbq_ref, gbk_ref, gcq_ref, gck_ref,
                 aqT_ref, ak_ref, avT_ref, bqT_ref, bk_ref, bvT_ref, cqT_ref, ck_ref, cvT_ref,
                 iqT_ref, ik_ref, iwT_ref, *, tm, iw_scale):
    hb = _rms(x_ref[0], g_ref[...]).astype(BF16)
    seg = seg_ref[...]
    n_sub = tm // TILE

    def grp(a, n):
        return _dot(hb, w_ref[:, a:a + n])

    def segnorm(t, g):
        n = t.shape[1]
        ssq = _dot((t * t).astype(BF16), seg[:n, :n])
        return t * lax.rsqrt(ssq * (1.0 / HEAD_DIM) + EPS) * g

    def segnorm_t(t, gcol):
        n = t.shape[0] // HEAD_DIM
        t3 = t.reshape(n, HEAD_DIM, tm)
        ssq = jnp.sum(t3 * t3, axis=1, keepdims=True)
        return (t3 * lax.rsqrt(ssq * (1.0 / HEAD_DIM) + EPS)).reshape(t.shape) * gcol

    ones_rows = (lax.broadcasted_iota(I32, (BF16_ROWS, tm), 0) == 0).astype(F32)

    def put_slabs(ref, tT):
        for s in range(n_sub):
            ref[0, s] = tT[:, s * TILE:(s + 1) * TILE].astype(BF16)

    aqT_ref[0] = segnorm_t(grp(_C_AQ, BRANCH_WIDTH).T, gaq_ref[...]).astype(BF16)
    ak_ref[0] = segnorm(grp(_C_AK, BRANCH_WIDTH), gak_ref[...]).astype(BF16)
    avT = grp(_C_AV, BRANCH_WIDTH).T
    put_slabs(avT_ref, jnp.concatenate(
        [p for h in range(A_HEADS) for p in (avT[h * A_V_DIM:(h + 1) * A_V_DIM, :], ones_rows)], axis=0))
    bqT_ref[0] = segnorm_t(grp(_C_BQ, BRANCH_WIDTH).T, gbq_ref[...]).astype(BF16)
    cqT_ref[0] = segnorm_t(grp(_C_CQ, BRANCH_WIDTH).T, gcq_ref[...]).astype(BF16)
    bk_ref[0] = segnorm(grp(_C_BK, LANES), gbk_ref[...]).astype(BF16)
    bvT = grp(_C_BV, LANES).T
    put_slabs(bvT_ref, jnp.concatenate(
        [p for g in range(B_KV_HEADS) for p in (bvT[g * HEAD_DIM:(g + 1) * HEAD_DIM, :], ones_rows)], axis=0))
    iqT_ref[0] = grp(_C_IQ, LANES).T.astype(BF16)
    ckv = grp(_C_CKV, LANES)
    ck = ckv[:, :HEAD_DIM]
    ssq = jnp.sum(ck * ck, axis=-1, keepdims=True)
    ck_ref[0] = (ck * lax.rsqrt(ssq * (1.0 / HEAD_DIM) + EPS) * gck_ref[...]).astype(BF16)
    put_slabs(cvT_ref, jnp.concatenate([ckv.T[HEAD_DIM:, :], ones_rows], axis=0))
    ikw = grp(_C_IKW, LANES)
    ik_ref[0] = ikw[:, :IDX_DIM].astype(BF16)
    iwT_ref[0] = ikw.T[IDX_DIM:IDX_DIM + IDX_HEADS, :] * iw_scale


def _proj(x, g, w, seg, gains, tm):
    B, S, D = x.shape
    nt = S // TILE
    n_sub = tm // TILE
    full = lambda shape: pl.BlockSpec(shape, lambda b, t: (0,) * len(shape))
    out_shape = [
        jax.ShapeDtypeStruct((B, BRANCH_WIDTH, S), BF16),
        jax.ShapeDtypeStruct((B, S, BRANCH_WIDTH), BF16),
        jax.ShapeDtypeStruct((B, nt, A_HEADS * A_VT_ROWS, TILE), BF16),
        jax.ShapeDtypeStruct((B, BRANCH_WIDTH, S), BF16),
        jax.ShapeDtypeStruct((B, S, LANES), BF16),
        jax.ShapeDtypeStruct((B, nt, B_KV_HEADS * C_VT_ROWS, TILE), BF16),
        jax.ShapeDtypeStruct((B, BRANCH_WIDTH, S), BF16),
        jax.ShapeDtypeStruct((B, S, HEAD_DIM), BF16),
        jax.ShapeDtypeStruct((B, nt, C_VT_ROWS, TILE), BF16),
        jax.ShapeDtypeStruct((B, LANES, S), BF16),
        jax.ShapeDtypeStruct((B, S, IDX_DIM), BF16),
        jax.ShapeDtypeStruct((B, IDX_HEADS, S), F32),
    ]
    colT = lambda r: pl.BlockSpec((1, r, tm), lambda b, t: (b, 0, t))
    row = lambda c: pl.BlockSpec((1, tm, c), lambda b, t: (b, t, 0))
    slab = lambda r: pl.BlockSpec((1, n_sub, r, TILE), lambda b, t: (b, t, 0, 0))
    out_specs = [colT(BRANCH_WIDTH), row(BRANCH_WIDTH), slab(A_HEADS * A_VT_ROWS), colT(BRANCH_WIDTH),
                 row(LANES),
                 slab(B_KV_HEADS * C_VT_ROWS), colT(BRANCH_WIDTH), row(HEAD_DIM), slab(C_VT_ROWS), colT(LANES), row(IDX_DIM),
                 pl.BlockSpec((1, IDX_HEADS, tm), lambda b, t: (b, 0, t))]
    in_specs = [pl.BlockSpec((1, tm, D), lambda b, t: (b, t, 0)), full((1, D)), full((D, _W_COLS)),
                full((BRANCH_WIDTH, BRANCH_WIDTH))] + [full(gn.shape) for gn in gains]
    vmem = 2 * (tm * D * 4 + D * _W_COLS * 2 + BRANCH_WIDTH * BRANCH_WIDTH * 2 + tm * 3400 * 2) + 24 * tm * BRANCH_WIDTH * 4
    return pl.pallas_call(
        functools.partial(_proj_kernel, tm=tm, iw_scale=IDX_HEADS ** -0.5 * IDX_DIM ** -0.5),
        grid=(B, S // tm), in_specs=in_specs, out_specs=out_specs, out_shape=out_shape,
        compiler_params=pltpu.CompilerParams(dimension_semantics=("arbitrary", "arbitrary"),
                                             vmem_limit_bytes=_vmem_limit(vmem)),
        name="proj",
    )(x, g, w, seg, *gains)


def _online_step(s, m_old, vt, acc_ref, ch):
    m_new = jnp.maximum(m_old, jnp.max(_colmax8(s), axis=0, keepdims=True))
    alpha = jnp.exp2(m_old - m_new)
    e = jnp.exp2(s - m_new).astype(BF16)
    acc_ref[ch] = acc_ref[ch] * alpha + _dot(vt, e)
    return m_new


def _sweep_key_tiles(i, n_chain, scores, vt, acc_ref):
    n_far = jnp.maximum(i - (NEAR_TILES - 1), 0)

    def step(tiles, ms):
        ms = list(ms)
        chains = [(j, d, ch) for (j, d) in tiles for ch in range(n_chain)]
        pending = [scores(*c) for c in chains[:SKEW]]
        for n, (j, d, ch) in enumerate(chains):
            s = pending.pop(0)
            if n + SKEW < len(chains):
                pending.append(scores(*chains[n + SKEW]))
            ms[ch] = _online_step(s, ms[ch], vt(j, ch), acc_ref, ch)
        return tuple(ms)

    def near_pairs(ms):
        ms = step([(i - 3, 3), (i - 2, 2)], ms)
        return step([(i - 1, 1), (i, 0)], ms)

    def near_singles(ms):
        return lax.fori_loop(0, i + 1, lambda t, ms: step([(t, i - t)], ms), ms)

    assert NEAR_TILES == 4
    ms = tuple(jnp.full((1, TILE), -jnp.inf, F32) for _ in range(n_chain))
    ms = lax.fori_loop(0, n_far // 4, lambda p, ms: step([(4 * p + u, None) for u in range(4)], ms), ms)
    rest = n_far - n_far % 4
    ms = lax.cond(n_far % 4 >= 2, lambda ms: step([(rest, None), (rest + 1, None)], ms), lambda ms: ms, ms)
    ms = lax.cond(n_far % 2 == 1, lambda ms: step([(n_far - 1, None)], ms), lambda ms: ms, ms)
    lax.cond(i >= NEAR_TILES - 1, near_pairs, near_singles, ms)


def _attn_a_kernel(qT_ref, k_ref, vT_ref, bias_ref, lam_ref, sub_ref, o_ref, q2_ref, acc_ref, *, lambda_init):
    i = pl.program_id(1)
    lp = lam_ref[...]
    lam = (jnp.exp(jnp.sum(lp[0:1] * lp[1:2], axis=-1, keepdims=True))
           - jnp.exp(jnp.sum(lp[2:3] * lp[3:4], axis=-1, keepdims=True)) + lambda_init)
    row = lax.broadcasted_iota(I32, (2 * HEAD_DIM, TILE), 0)
    n_chain = 2 * A_HEADS

    for h in range(A_HEADS):
        qh = qT_ref[0, h * A_V_DIM:(h + 1) * A_V_DIM, :]
        zero = jnp.zeros_like(qh)
        q2_ref[2 * h] = jnp.where(row < HEAD_DIM, qh, zero)
        q2_ref[2 * h + 1] = jnp.where(row >= HEAD_DIM, qh, zero)
    acc_ref[...] = jnp.zeros_like(acc_ref)

    def scores(j, d, ch):
        h = ch // 2
        rows = pl.ds(pl.multiple_of(j * TILE, TILE), TILE)
        s = _dot(k_ref[0, rows, h * A_V_DIM:(h + 1) * A_V_DIM], q2_ref[ch])
        return s if d is None else s + bias_ref[h, d]

    def vt(j, ch):
        h = ch // 2
        return vT_ref[0, j, h * A_VT_ROWS:(h + 1) * A_VT_ROWS, :]

    _sweep_key_tiles(i, n_chain, scores, vt, acc_ref)

    for h in range(A_HEADS):
        a0 = acc_ref[2 * h]
        a1 = acc_ref[2 * h + 1]
        r0 = 1.0 / a0[A_V_DIM:A_V_DIM + 1, :]
        r1 = 1.0 / a1[A_V_DIM:A_V_DIM + 1, :]
        outT = a0[:A_V_DIM, :] * r0 - lam * (a1[:A_V_DIM, :] * r1)
        out = _rms(outT.T, sub_ref[...]) * (1.0 - lambda_init)
        o_ref[0, :, h * A_V_DIM:(h + 1) * A_V_DIM] = out.astype(BF16)


def _attn_a(aqT, ak, avT, bias, lam_par, subln_g, lambda_init):
    B, _, S = aqT.shape
    nt = S // TILE
    vt_rows = A_HEADS * A_VT_ROWS
    vmem = 2 * (BRANCH_WIDTH * TILE * 2 + S * BRANCH_WIDTH * 2 + S * vt_rows * 2 + bias.size * 4 + TILE * BRANCH_WIDTH * 2) \
        + 8 * A_V_DIM * TILE * 2 + 8 * A_VT_ROWS * TILE * 4 + 48 * TILE * TILE * 4
    return pl.pallas_call(
        functools.partial(_attn_a_kernel, lambda_init=lambda_init),
        grid=(B, nt),
        in_specs=[pl.BlockSpec((1, BRANCH_WIDTH, TILE), lambda b, i: (b, 0, i)),
                  pl.BlockSpec((1, S, BRANCH_WIDTH), lambda b, i: (b, 0, 0)),
                  pl.BlockSpec((1, nt, vt_rows, TILE), lambda b, i: (b, 0, 0, 0)),
                  pl.BlockSpec(bias.shape, lambda b, i: (0, 0, 0, 0)),
                  pl.BlockSpec((4, HEAD_DIM), lambda b, i: (0, 0)),
                  pl.BlockSpec((1, A_V_DIM), lambda b, i: (0, 0))],
        out_specs=pl.BlockSpec((1, TILE, BRANCH_WIDTH), lambda b, i: (b, i, 0)),
        out_shape=jax.ShapeDtypeStruct((B, S, BRANCH_WIDTH), BF16),
        scratch_shapes=[pltpu.VMEM((2 * A_HEADS, A_V_DIM, TILE), BF16),
                        pltpu.VMEM((2 * A_HEADS, A_VT_ROWS, TILE), F32)],
        compiler_params=pltpu.CompilerParams(dimension_semantics=("arbitrary", "arbitrary"),
                                             vmem_limit_bytes=_vmem_limit(vmem)),
        name="attn_a",
    )(aqT, ak, avT, bias, lam_par, subln_g)


def _attn_b_kernel(sink_ref, qT_ref, k_ref, vT_ref, bias_ref, o_ref, oT_ref):
    i = pl.program_id(1)
    jp = jnp.maximum(i - 1, 0)
    p_idx = jnp.where(i > 0, 1, 2)
    cur = pl.ds(pl.multiple_of(i * TILE, TILE), TILE)
    prev = pl.ds(pl.multiple_of(jp * TILE, TILE), TILE)

    def scores(h):
        qh = qT_ref[0, h * HEAD_DIM:(h + 1) * HEAD_DIM, :]
        zero = jnp.zeros_like(qh)
        q2 = jnp.concatenate([qh, zero] if h < B_GROUP else [zero, qh], axis=0)
        return (_dot(k_ref[0, cur, :], q2) + bias_ref[h, 0], _dot(k_ref[0, prev, :], q2) + bias_ref[h, p_idx])

    def finish(h, s):
        sc, sp = s
        gs = slice((h // B_GROUP) * C_VT_ROWS, (h // B_GROUP + 1) * C_VT_ROWS)
        sink = sink_ref[h] * LOG2E
        m = jnp.max(jnp.maximum(_colmax8(sc), _colmax8(sp)), axis=0, keepdims=True)
        m = jnp.maximum(m, sink)
        ec = jnp.exp2(sc - m).astype(BF16)
        ep = jnp.exp2(sp - m).astype(BF16)
        outT = _dot(vT_ref[0, i, gs, :], ec) + _dot(vT_ref[0, jp, gs, :], ep)
        den = outT[HEAD_DIM:HEAD_DIM + 1, :] + jnp.exp2(sink - m)
        oT_ref[h * HEAD_DIM:(h + 1) * HEAD_DIM, :] = outT[:HEAD_DIM, :] * (1.0 / den)

    pending = [scores(h) for h in range(SKEW)]
    for h in range(B_HEADS):
        s = pending.pop(0)
        if h + SKEW < B_HEADS:
            pending.append(scores(h + SKEW))
        finish(h, s)
    o_ref[0] = oT_ref[...].T.astype(BF16)


def _attn_b(bqT, bk, bvT, bias, sinks):
    B, _, S = bqT.shape
    nt = S // TILE
    vmem = 2 * (BRANCH_WIDTH * TILE * 2 + 2 * S * LANES * 2 + S * LANES * 2 + bias.size * 4 + TILE * BRANCH_WIDTH * 2) \
        + 24 * TILE * TILE * 4
    return pl.pallas_call(
        _attn_b_kernel,
        grid=(B, nt),
        in_specs=[pl.BlockSpec(memory_space=pltpu.SMEM),
                  pl.BlockSpec((1, BRANCH_WIDTH, TILE), lambda b, i: (b, 0, i)),
                  pl.BlockSpec((1, S, LANES), lambda b, i: (b, 0, 0)),
                  pl.BlockSpec((1, nt, B_KV_HEADS * C_VT_ROWS, TILE), lambda b, i: (b, 0, 0, 0)),
                  pl.BlockSpec(bias.shape, lambda b, i: (0, 0, 0, 0))],
        out_specs=pl.BlockSpec((1, TILE, BRANCH_WIDTH), lambda b, i: (b, i, 0)),
        out_shape=jax.ShapeDtypeStruct((B, S, BRANCH_WIDTH), BF16),
        scratch_shapes=[pltpu.VMEM((BRANCH_WIDTH, TILE), F32)],
        compiler_params=pltpu.CompilerParams(dimension_semantics=("arbitrary", "arbitrary"),
                                             vmem_limit_bytes=_vmem_limit(vmem)),
        name="attn_b",
    )(sinks, bqT, bk, bvT, bias)


def _attn_c_kernel(qT_ref, k_ref, vT_ref, iqT_ref, ik_ref, iwT_ref, bias_ref, o_ref,
                   hi_ref, lo_ref, mb_ref, acc_ref, oT_ref, *, top_k):
    i = pl.program_id(1)
    n_t = i + 1
    krow = lax.broadcasted_iota(I32, (TILE, TILE), 0)
    qcol = lax.broadcasted_iota(I32, (TILE, TILE), 1)
    allowed = (krow // CHUNK) <= (qcol // CHUNK)

    def idx_keys(j, diag):
        ikt = ik_ref[0, pl.ds(pl.multiple_of(j * TILE, TILE), TILE), :]
        lgs = [_dot(ikt, iqT_ref[0, hh * IDX_DIM:(hh + 1) * IDX_DIM, :]) for hh in range(IDX_HEADS)]
        sc = jnp.zeros((TILE, TILE), F32)
        for hh in range(IDX_HEADS):
            sc = sc + jnp.maximum(lgs[hh], 0.0) * iwT_ref[0, hh:hh + 1, :]
        if diag:
            sc = jnp.where(allowed, sc, NEG)
        bits = lax.bitcast_convert_type(sc, I32)
        key = bits ^ ((bits >> 31) & 0x7FFFFFFF)
        hi_ref[j] = (key >> 16).astype(I16)
        lo_ref[j] = ((key & 0xFFFF) + I16_MIN).astype(I16)

    def fill_pair(p, carry):
        idx_keys(2 * p, False)
        idx_keys(2 * p + 1, False)
        return carry

    lax.fori_loop(0, i // 2, fill_pair, 0)

    @pl.when(i % 2 == 1)
    def _():
        idx_keys(i - 1, False)

    idx_keys(i, True)

    n_pair = (n_t + 1) // 2

    @pl.when(n_t % 2 == 1)
    def _():
        hi_ref[n_t] = jnp.full((TILE, TILE), I16_MIN, I16)
        lo_ref[n_t] = jnp.full((TILE, TILE), I16_MIN, I16)

    groups = TILE // BF16_ROWS

    def rows16(ref, j):
        return ref[j].reshape(groups, BF16_ROWS, TILE)

    def bcast16(v):
        return jnp.broadcast_to(v, (BF16_ROWS, TILE)).astype(I16)

    def count(pred):
        def body(p, cs):
            cs = list(cs)
            for u in (0, 1):
                j = 2 * p + u
                hi, lo = rows16(hi_ref, j), rows16(lo_ref, j)
                for r in range(groups):
                    cs[r % len(cs)] = cs[r % len(cs)] + pred(hi[r], lo[r], j, r).astype(I16)
            return tuple(cs)
        cs = lax.fori_loop(0, n_pair, body, (jnp.zeros((BF16_ROWS, TILE), I16),) * 4)
        c = (cs[0] + cs[1]) + (cs[2] + cs[3])
        return jnp.sum(c.astype(I32), axis=0, keepdims=True)

    def search(n_bits, accept):
        def step(b, t):
            cand = t + lax.shift_left(jnp.int32(1), n_bits - 1 - b)
            return jnp.where(accept(cand), cand, t)
        return lax.fori_loop(0, n_bits, step, jnp.full((1, TILE), I16_MIN, I32))

    def hi_accept(cand):
        c16 = bcast16(cand)
        return count(lambda hi, lo, j, r: hi >= c16) >= top_k

    t_hi = search(16, hi_accept)
    th16 = bcast16(t_hi)
    cnt_above = count(lambda hi, lo, j, r: hi > th16)
    r_lo = top_k - cnt_above

    def bucket_only(p, carry):
        for u in (0, 1):
            j = 2 * p + u
            hi, lo = rows16(hi_ref, j), rows16(lo_ref, j)
            for r in range(groups):
                lo_ref[j, r * BF16_ROWS:(r + 1) * BF16_ROWS, :] = jnp.where(hi[r] == th16, lo[r], I16_MIN)
        return carry

    lax.fori_loop(0, n_pair, bucket_only, 0)

    def lo_accept(cand):
        c16 = bcast16(cand)
        return count(lambda hi, lo, j, r: lo >= c16) >= r_lo

    t_lo = search(16, lo_accept)
    tl16 = bcast16(t_lo)
    cnt_gt = count(lambda hi, lo, j, r: lo > tl16)
    r_eq = (r_lo - cnt_gt).astype(F32)

    tri = (krow >= qcol).astype(BF16)
    one, zero = jnp.ones((), BF16), jnp.zeros((), BF16)

    def mask_tile(j, ties_before, diag):
        hi, lo = rows16(hi_ref, j), rows16(lo_ref, j)
        eq = jnp.concatenate([jnp.where((lo[r] == tl16) & (hi[r] == th16), one, zero) for r in range(groups)], axis=0)
        gt = jnp.concatenate([jnp.where((hi[r] > th16) | (lo[r] > tl16), one, zero) for r in range(groups)], axis=0)
        ties = _dot(tri, eq) + ties_before
        sel = (gt.astype(F32) > 0.5) | ((eq.astype(F32) > 0.5) & (ties <= r_eq))
        if diag:
            sel = sel & allowed
        mb_ref[j] = jnp.where(sel, 0.0, NEG)
        return ties[TILE - 1:TILE, :]

    ties_before = lax.fori_loop(0, i // 2, lambda p, c: mask_tile(2 * p + 1, mask_tile(2 * p, c, False), False),
                                jnp.zeros((1, TILE), F32))
    ties_before = lax.cond(i % 2 == 1, lambda c: mask_tile(i - 1, c, False), lambda c: c, ties_before)
    mask_tile(i, ties_before, True)

    acc_ref[...] = jnp.zeros_like(acc_ref)

    def scores(j, d, h):
        kt = k_ref[0, pl.ds(pl.multiple_of(j * TILE, TILE), TILE), :]
        s = _dot(kt, qT_ref[0, h * HEAD_DIM:(h + 1) * HEAD_DIM, :]) + mb_ref[j]
        return s if d is None else s + bias_ref[h, d]

    _sweep_key_tiles(i, C_HEADS, scores, lambda j, h: vT_ref[0, j], acc_ref)

    for h in range(C_HEADS):
        a = acc_ref[h]
        oT_ref[h * HEAD_DIM:(h + 1) * HEAD_DIM, :] = a[:HEAD_DIM, :] * (1.0 / a[HEAD_DIM:HEAD_DIM + 1, :])
    o_ref[0] = oT_ref[...].T.astype(BF16)


def _attn_c(cqT, ck, cvT, iqT, ik, iwT, bias, top_k):
    B, _, S = cqT.shape
    nt = S // TILE
    vmem = 2 * (BRANCH_WIDTH * TILE * 2 + 2 * S * LANES * 2 + S * HEAD_DIM * 2 + LANES * TILE * 2 + 8 * TILE * 4
                + bias.size * 4 + TILE * BRANCH_WIDTH * 2) + 2 * S * TILE * 4 + 48 * TILE * TILE * 4
    return pl.pallas_call(
        functools.partial(_attn_c_kernel, top_k=top_k),
        grid=(B, nt),
        in_specs=[pl.BlockSpec((1, BRANCH_WIDTH, TILE), lambda b, i: (b, 0, i)),
                  pl.BlockSpec((1, S, HEAD_DIM), lambda b, i: (b, 0, 0)),
                  pl.BlockSpec((1, nt, C_VT_ROWS, TILE), lambda b, i: (b, 0, 0, 0)),
                  pl.BlockSpec((1, LANES, TILE), lambda b, i: (b, 0, i)),
                  pl.BlockSpec((1, S, IDX_DIM), lambda b, i: (b, 0, 0)),
                  pl.BlockSpec((1, IDX_HEADS, TILE), lambda b, i: (b, 0, i)),
                  pl.BlockSpec(bias.shape, lambda b, i: (0, 0, 0, 0))],
        out_specs=pl.BlockSpec((1, TILE, BRANCH_WIDTH), lambda b, i: (b, i, 0)),
        out_shape=jax.ShapeDtypeStruct((B, S, BRANCH_WIDTH), BF16),
        scratch_shapes=[pltpu.VMEM((nt + nt % 2, TILE, TILE), I16), pltpu.VMEM((nt + nt % 2, TILE, TILE), I16),
                        pltpu.VMEM((nt, TILE, TILE), F32),
                        pltpu.VMEM((C_HEADS, C_VT_ROWS, TILE), F32), pltpu.VMEM((BRANCH_WIDTH, TILE), F32)],
        compiler_params=pltpu.CompilerParams(dimension_semantics=("arbitrary", "arbitrary"),
                                             vmem_limit_bytes=_vmem_limit(vmem)),
        name="attn_c",
    )(cqT, ck, cvT, iqT, ik, iwT, bias)


def _merge_kernel(x_ref, oa_ref, ob_ref, oc_ref, gmix_ref, wg_ref, bg_ref, wb_ref, wo_ref, gffn_ref,
                  wrh_ref, wrl_ref, br_ref, x1_ref, h2_ref, gate_ref):
    x = x_ref[...]
    hb = _rms(x, gmix_ref[...]).astype(BF16)
    z = None
    for n, o_ref in enumerate((oa_ref, ob_ref, oc_ref)):
        cs = slice(n * D_MODEL, (n + 1) * D_MODEL)
        gate = _sigmoid(_dot(hb, wg_ref[:, cs]) + bg_ref[:, cs])
        y = _dot(o_ref[...], wb_ref[n])
        z = gate * y if z is None else z + gate * y
    x1 = x + _dot(z.astype(BF16), wo_ref[...])
    x1_ref[...] = x1
    h2 = _rms(x1, gffn_ref[...])
    h2_ref[...] = h2.astype(BF16)

    hi, lo = _split_bf16(h2)
    lg = _dot(hi, wrh_ref[...]) + _dot(lo, wrh_ref[...]) + _dot(hi, wrl_ref[...]) + br_ref[...]
    col = lax.broadcasted_iota(I32, lg.shape, 1).astype(F32)
    big = float(4 * LANES)
    is_g = (col >= N_EXPERTS) & (col < N_EXPERTS + N_GROUPS)
    gl = jnp.where(is_g, lg, -jnp.inf)
    gmax = jnp.max(gl, axis=-1, keepdims=True)
    p_group = 1.0 / jnp.sum(jnp.exp(gl - gmax), axis=-1, keepdims=True)
    g_sel = jnp.min(jnp.where(gl == gmax, col, big), axis=-1, keepdims=True) - N_EXPERTS
    in_g = (col >= g_sel * EXPERTS_PER_GROUP) & (col < (g_sel + 1) * EXPERTS_PER_GROUP)
    el = jnp.where(in_g, lg, -jnp.inf)
    e1 = jnp.max(el, axis=-1, keepdims=True)
    i1 = jnp.min(jnp.where(el == e1, col, big), axis=-1, keepdims=True)
    el2 = jnp.where(col == i1, -jnp.inf, el)
    e2 = jnp.max(el2, axis=-1, keepdims=True)
    i2 = jnp.min(jnp.where(el2 == e2, col, big), axis=-1, keepdims=True)
    t2 = jnp.exp(e2 - e1)
    w1 = p_group / (1.0 + t2)
    w2 = w1 * t2
    gates = jnp.where(col == i1, w1, 0.0) + jnp.where(col == i2, w2, 0.0)
    gate_ref[...] = gates[:, :N_EXPERTS]


def _merge(x2, oa, ob, oc, gmix, wg, bg, wb, wo, gffn, wrh, wrl, br, tm):
    T, D = x2.shape
    full = lambda a: pl.BlockSpec(a.shape, lambda t: (0,) * a.ndim)
    rowb = lambda c: pl.BlockSpec((tm, c), lambda t: (t, 0))
    vmem = 2 * (tm * D * 4 * 2 + 3 * tm * BRANCH_WIDTH * 2 + tm * D * 2 + wg.size * 2 + wb.size * 2 + wo.size * 2
                + 2 * D * LANES * 2) + 10 * tm * D * 4
    return pl.pallas_call(
        _merge_kernel,
        grid=(T // tm,),
        in_specs=[rowb(D), rowb(BRANCH_WIDTH), rowb(BRANCH_WIDTH), rowb(BRANCH_WIDTH), full(gmix), full(wg), full(bg), full(wb),
                  full(wo), full(gffn), full(wrh), full(wrl), full(br)],
        out_specs=[rowb(D), rowb(D), rowb(N_EXPERTS)],
        out_shape=[jax.ShapeDtypeStruct((T, D), F32), jax.ShapeDtypeStruct((T, D), BF16),
                   jax.ShapeDtypeStruct((T, N_EXPERTS), F32)],
        compiler_params=pltpu.CompilerParams(dimension_semantics=("arbitrary",),
                                             vmem_limit_bytes=_vmem_limit(vmem)),
        name="merge",
    )(x2, oa, ob, oc, gmix, wg, bg, wb, wo, gffn, wrh, wrl, br)


def _moe_kernel(x1_ref, h2_ref, gate_ref, wgu_ref, wd_ref, o_ref):
    g = pl.program_id(1)
    h2 = h2_ref[...]
    gate = gate_ref[...]
    lane = lax.broadcasted_iota(I32, gate.shape, 1)
    hids = []
    for u in range(EXPERTS_PER_GROUP):
        gu = _dot(h2, wgu_ref[u])
        a = gu[:, :EXPERT_FF]
        w = jnp.sum(jnp.where(lane == g * EXPERTS_PER_GROUP + u, gate, 0.0), axis=-1, keepdims=True)
        hids.append((a * _sigmoid(a) * gu[:, EXPERT_FF:] * w).astype(BF16))
    y = _dot(jnp.concatenate(hids, axis=1), wd_ref[...])

    @pl.when(g == 0)
    def _():
        o_ref[...] = x1_ref[...] + y

    @pl.when(g > 0)
    def _():
        o_ref[...] += y


def _moe(x1, h2, gate, wgu, wd, tm):
    T, D = x1.shape
    n_e, ff = EXPERTS_PER_GROUP, EXPERT_FF
    vmem = 2 * (tm * D * 4 * 2 + tm * D * 2 + tm * LANES * 4 + n_e * D * 2 * ff * 2 + n_e * ff * D * 2) \
        + 3 * n_e * tm * 2 * ff * 4 + 2 * tm * D * 4
    return pl.pallas_call(
        _moe_kernel,
        grid=(T // tm, N_GROUPS),
        in_specs=[pl.BlockSpec((tm, D), lambda t, g: (t, 0)),
                  pl.BlockSpec((tm, D), lambda t, g: (t, 0)),
                  pl.BlockSpec((tm, N_EXPERTS), lambda t, g: (t, 0)),
                  pl.BlockSpec((n_e, D, 2 * ff), lambda t, g: (g, 0, 0)),
                  pl.BlockSpec((n_e * ff, D), lambda t, g: (g, 0))],
        out_specs=pl.BlockSpec((tm, D), lambda t, g: (t, 0)),
        out_shape=jax.ShapeDtypeStruct((T, D), F32),
        compiler_params=pltpu.CompilerParams(dimension_semantics=("arbitrary", "arbitrary"),
                                             vmem_limit_bytes=_vmem_limit(vmem)),
        name="moe",
    )(x1, h2, gate, wgu, wd)


def _t5_bucket_np(rel):
    half = NUM_BUCKETS // 2
    max_exact = half // 2
    n = np.abs(rel)
    n_f = np.maximum(n, 1).astype(np.float32)
    large = max_exact + (np.log(n_f / np.float32(max_exact)) / np.float32(math.log(MAX_DISTANCE / max_exact))
                         * np.float32(half - max_exact)).astype(np.int32)
    large = np.minimum(large, half - 1)
    return np.where(rel > 0, half, 0) + np.where(n < max_exact, n, large)


def _toeplitz_kernel(v_ref, o_ref):
    v = v_ref[0]
    for d in range(NEAR_TILES):
        x = jnp.broadcast_to(v[d:d + 1, :], (TILE, 2 * TILE))
        o_ref[0, d] = pltpu.roll(x, 0, 1, stride=1, stride_axis=0)[:, :TILE]


def _toeplitz(vals):
    n_heads = vals.shape[0]
    return pl.pallas_call(
        _toeplitz_kernel,
        grid=(n_heads,),
        in_specs=[pl.BlockSpec((1, NEAR_TILES, 2 * TILE), lambda h: (h, 0, 0))],
        out_specs=pl.BlockSpec((1, NEAR_TILES, TILE, TILE), lambda h: (h, 0, 0, 0)),
        out_shape=jax.ShapeDtypeStruct((n_heads, NEAR_TILES, TILE, TILE), F32),
        name="bias_tiles",
    )(vals)


def _bias_tables(rel_bias):
    u = np.arange(2 * TILE)
    off = np.where(u < TILE, u, u - 2 * TILE)
    rel = np.stack([-off - TILE * d for d in range(NEAR_TILES)])
    onehot = (_t5_bucket_np(rel)[..., None] == np.arange(NUM_BUCKETS)).astype(np.float32)
    vals = jnp.einsum("dub,bh->hdu", jnp.asarray(onehot), rel_bias, precision=lax.Precision.HIGHEST)
    tab = _toeplitz(vals)

    far = rel_bias[NUM_BUCKETS // 2 - 1]
    kk = np.arange(TILE)[:, None]
    qc = np.arange(TILE)[None, :] // CHUNK
    kc = np.stack([(kk // CHUNK) - (TILE // CHUNK) * d for d in range(NEAR_TILES)]) + 0 * qc
    a_ok = jnp.asarray(kc <= qc)
    ta = jnp.where(a_ok[None], (tab[:A_HEADS] - far[:A_HEADS, None, None, None]) * LOG2E, NEG)
    b_ok = jnp.asarray((qc - kc[:2] >= 0) & (qc - kc[:2] <= W_CHUNKS))
    tb = jnp.where(b_ok[None], tab[A_HEADS:A_HEADS + B_HEADS, :2] * LOG2E, NEG)
    tb = jnp.concatenate([tb, jnp.full((B_HEADS, 1, TILE, TILE), NEG, F32)], axis=1)
    tc = (tab[A_HEADS + B_HEADS:] - far[A_HEADS + B_HEADS:, None, None, None]) * LOG2E
    return ta.astype(F32), tb.astype(F32), tc.astype(F32)


def _proj_weight_kernel(w_ref, o_ref):
    o_ref[...] = jnp.zeros_like(o_ref)
    o_ref[:, :_IN_COLS] = w_ref[0].astype(BF16)


def _proj_weight(w_in, l):
    _, d, n = w_in.shape
    assert n == _IN_COLS
    rows = 128
    return pl.pallas_call(
        _proj_weight_kernel,
        grid=(d // rows,),
        in_specs=[pl.BlockSpec((1, rows, n), lambda r: (l, r, 0))],
        out_specs=pl.BlockSpec((rows, _W_COLS), lambda r: (r, 0)),
        out_shape=jax.ShapeDtypeStruct((d, _W_COLS), BF16),
        name="proj_weight",
    )(w_in)


def kernel(x, rel_bias, norm_mix_g, w_in, qk_norm_g, diff_lambda, diff_subln_g, sinks, w_branch, w_gate, b_gate,
           w_out, norm_ffn_g, w_router_group, b_router_group, w_router_expert, b_router_expert, w_ff_gate,
           w_ff_up, w_ff_down):
    B, S, D = x.shape
    assert D == D_MODEL and S % TILE == 0
    T = B * S
    top_k = min(TOPK_MAX, S // 4)
    tm_proj = 1024 if S % 1024 == 0 else TILE
    tm_merge = 512 if T % 512 == 0 else TILE
    tm_moe = 1024 if T % 1024 == 0 else TILE

    bias_a, bias_b, bias_c = _bias_tables(rel_bias)
    seg = jnp.asarray(np.kron(np.eye(BRANCH_WIDTH // HEAD_DIM), np.ones((HEAD_DIM, HEAD_DIM))), BF16)
    q_scale = HEAD_DIM ** -0.5

    for l in range(DEPTH):
        lambda_init = 0.8 - 0.6 * math.exp(-0.3 * l)
        qg = qk_norm_g[l]
        tile8 = lambda g: jnp.tile(g, BRANCH_WIDTH // HEAD_DIM)
        gains = ((tile8(qg[0, 0]) * (q_scale * LOG2E))[:, None], tile8(qg[0, 1])[None, :],
                 (tile8(qg[1, 0]) * (q_scale * LOG2E))[:, None], jnp.tile(qg[1, 1], B_KV_HEADS)[None, :],
                 (tile8(qg[2, 0]) * (q_scale * LOG2E))[:, None], qg[2, 1][None, :])
        (aqT, ak, avT, bqT, bk, bvT, cqT, ck, cvT, iqT, ik, iwT) = _proj(
            x, norm_mix_g[l][None, :], _proj_weight(w_in, l), seg, gains, tm_proj)

        oa = _attn_a(aqT, ak, avT, bias_a, diff_lambda[l], diff_subln_g[l][None, :], lambda_init)
        ob = _attn_b(bqT, bk, bvT, bias_b, sinks[l])
        oc = _attn_c(cqT, ck, cvT, iqT, ik, iwT, bias_c, top_k)

        w_r = jnp.concatenate([w_router_expert[l], w_router_group[l],
                               jnp.zeros((D, LANES - N_EXPERTS - N_GROUPS), F32)], axis=1)
        b_r = jnp.concatenate([b_router_expert[l], b_router_group[l],
                               jnp.zeros((LANES - N_EXPERTS - N_GROUPS,), F32)])[None, :]
        wrh = w_r.astype(BF16)
        wrl = (w_r - wrh.astype(F32)).astype(BF16)
        x1, h2, gate = _merge(
            x.reshape(T, D), oa.reshape(T, BRANCH_WIDTH), ob.reshape(T, BRANCH_WIDTH), oc.reshape(T, BRANCH_WIDTH),
            norm_mix_g[l][None, :], w_gate[l].astype(BF16), b_gate[l][None, :], w_branch[l].astype(BF16),
            w_out[l].astype(BF16), norm_ffn_g[l][None, :], wrh, wrl, b_r, tm_merge)

        wgu = jnp.concatenate([w_ff_gate[l], w_ff_up[l]], axis=-1).astype(BF16)
        wd = w_ff_down[l].astype(BF16).reshape(N_EXPERTS * EXPERT_FF, D)
        x = _moe(x1, h2, gate, wgu, wd, tm_moe).reshape(B, S, D)
    return x
```

```python
import functools
import math

import numpy as np
import jax
import jax.numpy as jnp
from jax import lax
from jax.experimental import pallas as pl
from jax.experimental.pallas import tpu as pltpu

F32 = jnp.float32
BF16 = jnp.bfloat16
I32 = jnp.int32
I16 = jnp.int16

D_MODEL = 1024
DEPTH = 2
CHUNK = 64
HEAD_DIM = 64
A_HEADS = 4
A_V_DIM = 2 * HEAD_DIM
B_HEADS = 8
B_KV_HEADS = 2
B_GROUP = B_HEADS // B_KV_HEADS
W_CHUNKS = 2
C_HEADS = 8
IDX_HEADS = 4
IDX_DIM = 32
TOPK_MAX = 256
NUM_BUCKETS = 32
MAX_DISTANCE = 1024
N_BRANCH = 3
BRANCH_WIDTH = 512
N_GROUPS = 4
EXPERTS_PER_GROUP = 4
N_EXPERTS = N_GROUPS * EXPERTS_PER_GROUP
EXPERT_FF = 256
EPS = 1e-6
NEG = -1e30
I16_MIN = -(2 ** 15)

LANES = 128
SUBLANES = 8
TILE = 256
NEAR_TILES = 4
BF16_ROWS = 16
A_VT_ROWS = A_V_DIM + BF16_ROWS
C_VT_ROWS = HEAD_DIM + BF16_ROWS
LOG2E = 1.4426950408889634
SKEW = 5
VMEM_CAP = 60000 * 1024

_C_AQ, _C_AK, _C_AV, _C_BQ, _C_BK, _C_BV, _C_CQ = 0, 512, 1024, 1536, 2048, 2176, 2304
_C_CKV, _C_IQ, _C_IKW = 2816, 2944, 3072
_IN_COLS = 3108
_W_COLS = 3200


def _dot(a, b):
    return jnp.dot(a, b, preferred_element_type=F32)


def _split_bf16(a):
    hi = a.astype(BF16)
    lo = (a - hi.astype(F32)).astype(BF16)
    return hi, lo


def _sigmoid(x):
    return 1.0 / (1.0 + jnp.exp(-x))


def _rms(x, g):
    return x * lax.rsqrt(jnp.mean(x * x, axis=-1, keepdims=True) + EPS) * g


def _colmax8(s):
    r, c = s.shape
    return jnp.max(s.reshape(r // SUBLANES, SUBLANES, c), axis=0)


def _colsum8(s):
    r, c = s.shape
    return jnp.sum(s.reshape(r // SUBLANES, SUBLANES, c), axis=0)


def _vmem_limit(nbytes):
    return int(min(VMEM_CAP, nbytes))


def _proj_kernel(x_ref, g_ref, w_ref, seg_ref, gaq_ref, gak_ref, gbq_ref, gbk_ref, gcq_ref, gck_ref,
                 aqT_ref, ak_ref, avT_ref, bqT_ref, bk_ref, bvT_ref, cqT_ref, ck_ref, cvT_ref,
                 iqT_ref, ik_ref, iwT_ref, *, tm, iw_scale):
    hb = _rms(x_ref[0], g_ref[...]).astype(BF16)
    seg = seg_ref[...]
    n_sub = tm // TILE

    def grp(a, n):
        return _dot(hb, w_ref[:, a:a + n])

    def segnorm(t, g):
        n = t.shape[1]
        ssq = _dot((t * t).astype(BF16), seg[:n, :n])
        return t * lax.rsqrt(ssq * (1.0 / HEAD_DIM) + EPS) * g

    def segnorm_t(t, gcol):
        n = t.shape[0] // HEAD_DIM
        t3 = t.reshape(n, HEAD_DIM, tm)
        ssq = jnp.sum(t3 * t3, axis=1, keepdims=True)
        return (t3 * lax.rsqrt(ssq * (1.0 / HEAD_DIM) + EPS)).reshape(t.shape) * gcol

    ones_rows = (lax.broadcasted_iota(I32, (BF16_ROWS, tm), 0) == 0).astype(F32)

    def put_slabs(ref, tT):
        for s in range(n_sub):
            ref[0, s] = tT[:, s * TILE:(s + 1) * TILE].astype(BF16)

    aqT_ref[0] = segnorm_t(grp(_C_AQ, BRANCH_WIDTH).T, gaq_ref[...]).astype(BF16)
    ak_ref[0] = segnorm(grp(_C_AK, BRANCH_WIDTH), gak_ref[...]).astype(BF16)
    avT = grp(_C_AV, BRANCH_WIDTH).T
    put_slabs(avT_ref, jnp.concatenate(
        [p for h in range(A_HEADS) for p in (avT[h * A_V_DIM:(h + 1) * A_V_DIM, :], ones_rows)], axis=0))
    bqT_ref[0] = segnorm_t(grp(_C_BQ, BRANCH_WIDTH).T, gbq_ref[...]).astype(BF16)
    cqT_ref[0] = segnorm_t(grp(_C_CQ, BRANCH_WIDTH).T, gcq_ref[...]).astype(BF16)
    bk_ref[0] = segnorm(grp(_C_BK, LANES), gbk_ref[...]).astype(BF16)
    bvT = grp(_C_BV, LANES).T
    put_slabs(bvT_ref, jnp.concatenate(
        [p for g in range(B_KV_HEADS) for p in (bvT[g * HEAD_DIM:(g + 1) * HEAD_DIM, :], ones_rows)], axis=0))
    iqT_ref[0] = grp(_C_IQ, LANES).T.astype(BF16)
    ckv = grp(_C_CKV, LANES)
    ck = ckv[:, :HEAD_DIM]
    ssq = jnp.sum(ck * ck, axis=-1, keepdims=True)
    ck_ref[0] = (ck * lax.rsqrt(ssq * (1.0 / HEAD_DIM) + EPS) * gck_ref[...]).astype(BF16)
    put_slabs(cvT_ref, jnp.concatenate([ckv.T[HEAD_DIM:, :], ones_rows], axis=0))
    ikw = grp(_C_IKW, LANES)
    ik_ref[0] = ikw[:, :IDX_DIM].astype(BF16)
    iwT_ref[0] = ikw.T[IDX_DIM:IDX_DIM + IDX_HEADS, :] * iw_scale


def _proj(x, g, w, seg, gains, tm):
    B, S, D = x.shape
    nt = S // TILE
    n_sub = tm // TILE
    full = lambda shape: pl.BlockSpec(shape, lambda b, t: (0,) * len(shape))
    out_shape = [
        jax.ShapeDtypeStruct((B, BRANCH_WIDTH, S), BF16),
        jax.ShapeDtypeStruct((B, S, BRANCH_WIDTH), BF16),
        jax.ShapeDtypeStruct((B, nt, A_HEADS * A_VT_ROWS, TILE), BF16),
        jax.ShapeDtypeStruct((B, BRANCH_WIDTH, S), BF16),
        jax.ShapeDtypeStruct((B, S, LANES), BF16),
        jax.ShapeDtypeStruct((B, nt, B_KV_HEADS * C_VT_ROWS, TILE), BF16),
        jax.ShapeDtypeStruct((B, BRANCH_WIDTH, S), BF16),
        jax.ShapeDtypeStruct((B, S, HEAD_DIM), BF16),
        jax.ShapeDtypeStruct((B, nt, C_VT_ROWS, TILE), BF16),
        jax.ShapeDtypeStruct((B, LANES, S), BF16),
        jax.ShapeDtypeStruct((B, S, IDX_DIM), BF16),
        jax.ShapeDtypeStruct((B, IDX_HEADS, S), F32),
    ]
    colT = lambda r: pl.BlockSpec((1, r, tm), lambda b, t: (b, 0, t))
    row = lambda c: pl.BlockSpec((1, tm, c), lambda b, t: (b, t, 0))
    slab = lambda r: pl.BlockSpec((1, n_sub, r, TILE), lambda b, t: (b, t, 0, 0))
    out_specs = [colT(BRANCH_WIDTH), row(BRANCH_WIDTH), slab(A_HEADS * A_VT_ROWS), colT(BRANCH_WIDTH),
                 row(LANES),
                 slab(B_KV_HEADS * C_VT_ROWS), colT(BRANCH_WIDTH), row(HEAD_DIM), slab(C_VT_ROWS), colT(LANES), row(IDX_DIM),
                 pl.BlockSpec((1, IDX_HEADS, tm), lambda b, t: (b, 0, t))]
    in_specs = [pl.BlockSpec((1, tm, D), lambda b, t: (b, t, 0)), full((1, D)), full((D, _W_COLS)),
                full((BRANCH_WIDTH, BRANCH_WIDTH))] + [full(gn.shape) for gn in gains]
    vmem = 2 * (tm * D * 4 + D * _W_COLS * 2 + BRANCH_WIDTH * BRANCH_WIDTH * 2 + tm * 3400 * 2) + 24 * tm * BRANCH_WIDTH * 4
    return pl.pallas_call(
        functools.partial(_proj_kernel, tm=tm, iw_scale=IDX_HEADS ** -0.5 * IDX_DIM ** -0.5),
        grid=(B, S // tm), in_specs=in_specs, out_specs=out_specs, out_shape=out_shape,
        compiler_params=pltpu.CompilerParams(dimension_semantics=("arbitrary", "arbitrary"),
                                             vmem_limit_bytes=_vmem_limit(vmem)),
        name="proj",
    )(x, g, w, seg, *gains)


def _online_step(s, m_old, vt, acc_ref, ch):
    m_new = jnp.maximum(m_old, jnp.max(_colmax8(s), axis=0, keepdims=True))
    alpha = jnp.exp2(m_old - m_new)
    e = jnp.exp2(s - m_new).astype(BF16)
    acc_ref[ch] = acc_ref[ch] * alpha + _dot(vt, e)
    return m_new


def _sweep_key_tiles(i, n_chain, scores, vt, acc_ref):
    n_far = jnp.maximum(i - (NEAR_TILES - 1), 0)

    def step(tiles, ms):
        ms = list(ms)
        chains = [(j, d, ch) for (j, d) in tiles for ch in range(n_chain)]
        pending = [scores(*c) for c in chains[:SKEW]]
        for n, (j, d, ch) in enumerate(chains):
            s = pending.pop(0)
            if n + SKEW < len(chains):
                pending.append(scores(*chains[n + SKEW]))
            ms[ch] = _online_step(s, ms[ch], vt(j, ch), acc_ref, ch)
        return tuple(ms)

    def near_pairs(ms):
        ms = step([(i - 3, 3), (i - 2, 2)], ms)
        return step([(i - 1, 1), (i, 0)], ms)

    def near_singles(ms):
        return lax.fori_loop(0, i + 1, lambda t, ms: step([(t, i - t)], ms), ms)

    assert NEAR_TILES == 4
    ms = tuple(jnp.full((1, TILE), -jnp.inf, F32) for _ in range(n_chain))
    ms = lax.fori_loop(0, n_far // 4, lambda p, ms: step([(4 * p + u, None) for u in range(4)], ms), ms)
    rest = n_far - n_far % 4
    ms = lax.cond(n_far % 4 >= 2, lambda ms: step([(rest, None), (rest + 1, None)], ms), lambda ms: ms, ms)
    ms = lax.cond(n_far % 2 == 1, lambda ms: step([(n_far - 1, None)], ms), lambda ms: ms, ms)
    lax.cond(i >= NEAR_TILES - 1, near_pairs, near_singles, ms)


def _attn_a_kernel(qT_ref, k_ref, vT_ref, bias_ref, lam_ref, sub_ref, o_ref, q2_ref, acc_ref, *, lambda_init):
    i = pl.program_id(1)
    lp = lam_ref[...]
    lam = (jnp.exp(jnp.sum(lp[0:1] * lp[1:2], axis=-1, keepdims=True))
           - jnp.exp(jnp.sum(lp[2:3] * lp[3:4], axis=-1, keepdims=True)) + lambda_init)
    row = lax.broadcasted_iota(I32, (2 * HEAD_DIM, TILE), 0)
    n_chain = 2 * A_HEADS

    for h in range(A_HEADS):
        qh = qT_ref[0, h * A_V_DIM:(h + 1) * A_V_DIM, :]
        zero = jnp.zeros_like(qh)
        q2_ref[2 * h] = jnp.where(row < HEAD_DIM, qh, zero)
        q2_ref[2 * h + 1] = jnp.where(row >= HEAD_DIM, qh, zero)
    acc_ref[...] = jnp.zeros_like(acc_ref)

    def scores(j, d, ch):
        h = ch // 2
        rows = pl.ds(pl.multiple_of(j * TILE, TILE), TILE)
        s = _dot(k_ref[0, rows, h * A_V_DIM:(h + 1) * A_V_DIM], q2_ref[ch])
        return s if d is None else s + bias_ref[h, d]

    def vt(j, ch):
        h = ch // 2
        return vT_ref[0, j, h * A_VT_ROWS:(h + 1) * A_VT_ROWS, :]

    _sweep_key_tiles(i, n_chain, scores, vt, acc_ref)

    for h in range(A_HEADS):
        a0 = acc_ref[2 * h]
        a1 = acc_ref[2 * h + 1]
        r0 = 1.0 / a0[A_V_DIM:A_V_DIM + 1, :]
        r1 = 1.0 / a1[A_V_DIM:A_V_DIM + 1, :]
        outT = a0[:A_V_DIM, :] * r0 - lam * (a1[:A_V_DIM, :] * r1)
        out = _rms(outT.T, sub_ref[...]) * (1.0 - lambda_init)
        o_ref[0, :, h * A_V_DIM:(h + 1) * A_V_DIM] = out.astype(BF16)


def _attn_a(aqT, ak, avT, bias, lam_par, subln_g, lambda_init):
    B, _, S = aqT.shape
    nt = S // TILE
    vt_rows = A_HEADS * A_VT_ROWS
    vmem = 2 * (BRANCH_WIDTH * TILE * 2 + S * BRANCH_WIDTH * 2 + S * vt_rows * 2 + bias.size * 4 + TILE * BRANCH_WIDTH * 2) \
        + 8 * A_V_DIM * TILE * 2 + 8 * A_VT_ROWS * TILE * 4 + 48 * TILE * TILE * 4
    return pl.pallas_call(
        functools.partial(_attn_a_kernel, lambda_init=lambda_init),
        grid=(B, nt),
        in_specs=[pl.BlockSpec((1, BRANCH_WIDTH, TILE), lambda b, i: (b, 0, i)),
                  pl.BlockSpec((1, S, BRANCH_WIDTH), lambda b, i: (b, 0, 0)),
                  pl.BlockSpec((1, nt, vt_rows, TILE), lambda b, i: (b, 0, 0, 0)),
                  pl.BlockSpec(bias.shape, lambda b, i: (0, 0, 0, 0)),
                  pl.BlockSpec((4, HEAD_DIM), lambda b, i: (0, 0)),
                  pl.BlockSpec((1, A_V_DIM), lambda b, i: (0, 0))],
        out_specs=pl.BlockSpec((1, TILE, BRANCH_WIDTH), lambda b, i: (b, i, 0)),
        out_shape=jax.ShapeDtypeStruct((B, S, BRANCH_WIDTH), BF16),
        scratch_shapes=[pltpu.VMEM((2 * A_HEADS, A_V_DIM, TILE), BF16),
                        pltpu.VMEM((2 * A_HEADS, A_VT_ROWS, TILE), F32)],
        compiler_params=pltpu.CompilerParams(dimension_semantics=("arbitrary", "arbitrary"),
                                             vmem_limit_bytes=_vmem_limit(vmem)),
        name="attn_a",
    )(aqT, ak, avT, bias, lam_par, subln_g)


def _attn_b_kernel(sink_ref, qT_ref, k_ref, vT_ref, bias_ref, o_ref, oT_ref):
    i = pl.program_id(1)
    jp = jnp.maximum(i - 1, 0)
    p_idx = jnp.where(i > 0, 1, 2)
    cur = pl.ds(pl.multiple_of(i * TILE, TILE), TILE)
    prev = pl.ds(pl.multiple_of(jp * TILE, TILE), TILE)

    def scores(h):
        qh = qT_ref[0, h * HEAD_DIM:(h + 1) * HEAD_DIM, :]
        zero = jnp.zeros_like(qh)
        q2 = jnp.concatenate([qh, zero] if h < B_GROUP else [zero, qh], axis=0)
        return (_dot(k_ref[0, cur, :], q2) + bias_ref[h, 0], _dot(k_ref[0, prev, :], q2) + bias_ref[h, p_idx])

    def finish(h, s):
        sc, sp = s
        gs = slice((h // B_GROUP) * C_VT_ROWS, (h // B_GROUP + 1) * C_VT_ROWS)
        sink = sink_ref[h] * LOG2E
        m = jnp.max(jnp.maximum(_colmax8(sc), _colmax8(sp)), axis=0, keepdims=True)
        m = jnp.maximum(m, sink)
        ec = jnp.exp2(sc - m).astype(BF16)
        ep = jnp.exp2(sp - m).astype(BF16)
        outT = _dot(vT_ref[0, i, gs, :], ec) + _dot(vT_ref[0, jp, gs, :], ep)
        den = outT[HEAD_DIM:HEAD_DIM + 1, :] + jnp.exp2(sink - m)
        oT_ref[h * HEAD_DIM:(h + 1) * HEAD_DIM, :] = outT[:HEAD_DIM, :] * (1.0 / den)

    pending = [scores(h) for h in range(SKEW)]
    for h in range(B_HEADS):
        s = pending.pop(0)
        if h + SKEW < B_HEADS:
            pending.append(scores(h + SKEW))
        finish(h, s)
    o_ref[0] = oT_ref[...].T.astype(BF16)


def _attn_b(bqT, bk, bvT, bias, sinks):
    B, _, S = bqT.shape
    nt = S // TILE
    vmem = 2 * (BRANCH_WIDTH * TILE * 2 + 2 * S * LANES * 2 + S * LANES * 2 + bias.size * 4 + TILE * BRANCH_WIDTH * 2) \
        + 24 * TILE * TILE * 4
    return pl.pallas_call(
        _attn_b_kernel,
        grid=(B, nt),
        in_specs=[pl.BlockSpec(memory_space=pltpu.SMEM),
                  pl.BlockSpec((1, BRANCH_WIDTH, TILE), lambda b, i: (b, 0, i)),
                  pl.BlockSpec((1, S, LANES), lambda b, i: (b, 0, 0)),
                  pl.BlockSpec((1, nt, B_KV_HEADS * C_VT_ROWS, TILE), lambda b, i: (b, 0, 0, 0)),
                  pl.BlockSpec(bias.shape, lambda b, i: (0, 0, 0, 0))],
        out_specs=pl.BlockSpec((1, TILE, BRANCH_WIDTH), lambda b, i: (b, i, 0)),
        out_shape=jax.ShapeDtypeStruct((B, S, BRANCH_WIDTH), BF16),
        scratch_shapes=[pltpu.VMEM((BRANCH_WIDTH, TILE), F32)],
        compiler_params=pltpu.CompilerParams(dimension_semantics=("arbitrary", "arbitrary"),
                                             vmem_limit_bytes=_vmem_limit(vmem)),
        name="attn_b",
    )(sinks, bqT, bk, bvT, bias)


def _attn_c_kernel(qT_ref, k_ref, vT_ref, iqT_ref, ik_ref, iwT_ref, bias_ref, o_ref,
                   hi_ref, lo_ref, mb_ref, acc_ref, oT_ref, *, top_k):
    i = pl.program_id(1)
    n_t = i + 1
    krow = lax.broadcasted_iota(I32, (TILE, TILE), 0)
    qcol = lax.broadcasted_iota(I32, (TILE, TILE), 1)
    allowed = (krow // CHUNK) <= (qcol // CHUNK)

    def idx_keys(j, diag):
        ikt = ik_ref[0, pl.ds(pl.multiple_of(j * TILE, TILE), TILE), :]
        lgs = [_dot(ikt, iqT_ref[0, hh * IDX_DIM:(hh + 1) * IDX_DIM, :]) for hh in range(IDX_HEADS)]
        sc = jnp.zeros((TILE, TILE), F32)
        for hh in range(IDX_HEADS):
            sc = sc + jnp.maximum(lgs[hh], 0.0) * iwT_ref[0, hh:hh + 1, :]
        if diag:
            sc = jnp.where(allowed, sc, NEG)
        bits = lax.bitcast_convert_type(sc, I32)
        hi_b = (bits >> 16).astype(I16)
        lo_b = bits.astype(I16)
        sign = jnp.where(hi_b < 0, jnp.int16(-1), jnp.int16(0))
        hi_ref[j] = hi_b ^ (sign & 0x7FFF)
        lo_ref[j] = lo_b ^ sign ^ I16_MIN

    def fill_pair(p, carry):
        idx_keys(2 * p, False)
        idx_keys(2 * p + 1, False)
        return carry

    lax.fori_loop(0, i // 2, fill_pair, 0)

    @pl.when(i % 2 == 1)
    def _():
        idx_keys(i - 1, False)

    idx_keys(i, True)

    n_pair = (n_t + 1) // 2

    @pl.when(n_t % 2 == 1)
    def _():
        hi_ref[n_t] = jnp.full((TILE, TILE), I16_MIN, I16)
        lo_ref[n_t] = jnp.full((TILE, TILE), I16_MIN, I16)

    groups = TILE // BF16_ROWS

    def rows16(ref, j):
        return ref[j].reshape(groups, BF16_ROWS, TILE)

    def bcast16(v):
        return jnp.broadcast_to(v, (BF16_ROWS, TILE)).astype(I16)

    def count(pred):
        def body(p, cs):
            cs = list(cs)
            for u in (0, 1):
                j = 2 * p + u
                hi, lo = rows16(hi_ref, j), rows16(lo_ref, j)
                for r in range(groups):
                    cs[r % len(cs)] = cs[r % len(cs)] + pred(hi[r], lo[r], j, r).astype(I16)
            return tuple(cs)
        cs = lax.fori_loop(0, n_pair, body, (jnp.zeros((BF16_ROWS, TILE), I16),) * 4)
        c = (cs[0] + cs[1]) + (cs[2] + cs[3])
        return jnp.sum(c.astype(I32), axis=0, keepdims=True)

    def search(n_bits, accept):
        def step(b, t):
            cand = t + lax.shift_left(jnp.int32(1), n_bits - 1 - b)
            return jnp.where(accept(cand), cand, t)
        return lax.fori_loop(0, n_bits, step, jnp.full((1, TILE), I16_MIN, I32))

    def hi_accept(cand):
        c16 = bcast16(cand)
        return count(lambda hi, lo, j, r: hi >= c16) >= top_k

    t_hi = search(16, hi_accept)
    th16 = bcast16(t_hi)
    cnt_above = count(lambda hi, lo, j, r: hi > th16)
    r_lo = top_k - cnt_above

    def bucket_only(p, carry):
        for u in (0, 1):
            j = 2 * p + u
            hi, lo = rows16(hi_ref, j), rows16(lo_ref, j)
            for r in range(groups):
                lo_ref[j, r * BF16_ROWS:(r + 1) * BF16_ROWS, :] = jnp.where(hi[r] == th16, lo[r], I16_MIN)
        return carry

    lax.fori_loop(0, n_pair, bucket_only, 0)

    def lo_accept(cand):
        c16 = bcast16(cand)
        return count(lambda hi, lo, j, r: lo >= c16) >= r_lo

    t_lo = search(16, lo_accept)
    tl16 = bcast16(t_lo)
    cnt_gt = count(lambda hi, lo, j, r: lo > tl16)
    r_eq = (r_lo - cnt_gt).astype(F32)

    tri = (krow >= qcol).astype(BF16)
    one, zero, neg16 = jnp.ones((), BF16), jnp.zeros((), BF16), jnp.full((), NEG, BF16)

    def mask_tile(j, ties_before, diag):
        hi, lo = rows16(hi_ref, j), rows16(lo_ref, j)
        eqs = [(lo[r] == tl16) & (hi[r] == th16) for r in range(groups)]
        ties = _dot(tri, jnp.concatenate([jnp.where(e, one, zero) for e in eqs], axis=0)) + ties_before
        over = (ties - r_eq).astype(BF16)
        rows = []
        for r in range(groups):
            kept_tie = eqs[r] & (over[r * BF16_ROWS:(r + 1) * BF16_ROWS, :] <= zero)
            rows.append(jnp.where((hi[r] > th16) | (lo[r] > tl16) | kept_tie, zero, neg16))
        mb = jnp.concatenate(rows, axis=0).astype(F32)
        if diag:
            mb = jnp.where(allowed, mb, NEG)
        mb_ref[j] = mb
        return ties[TILE - 1:TILE, :]

    ties_before = lax.fori_loop(0, i // 2, lambda p, c: mask_tile(2 * p + 1, mask_tile(2 * p, c, False), False),
                                jnp.zeros((1, TILE), F32))
    ties_before = lax.cond(i % 2 == 1, lambda c: mask_tile(i - 1, c, False), lambda c: c, ties_before)
    mask_tile(i, ties_before, True)

    acc_ref[...] = jnp.zeros_like(acc_ref)

    def scores(j, d, h):
        kt = k_ref[0, pl.ds(pl.multiple_of(j * TILE, TILE), TILE), :]
        s = _dot(kt, qT_ref[0, h * HEAD_DIM:(h + 1) * HEAD_DIM, :]) + mb_ref[j]
        return s if d is None else s + bias_ref[h, d]

    _sweep_key_tiles(i, C_HEADS, scores, lambda j, h: vT_ref[0, j], acc_ref)

    for h in range(C_HEADS):
        a = acc_ref[h]
        oT_ref[h * HEAD_DIM:(h + 1) * HEAD_DIM, :] = a[:HEAD_DIM, :] * (1.0 / a[HEAD_DIM:HEAD_DIM + 1, :])
    o_ref[0] = oT_ref[...].T.astype(BF16)


def _attn_c(cqT, ck, cvT, iqT, ik, iwT, bias, top_k):
    B, _, S = cqT.shape
    nt = S // TILE
    vmem = 2 * (BRANCH_WIDTH * TILE * 2 + 2 * S * LANES * 2 + S * HEAD_DIM * 2 + LANES * TILE * 2 + 8 * TILE * 4
                + bias.size * 4 + TILE * BRANCH_WIDTH * 2) + 2 * S * TILE * 4 + 48 * TILE * TILE * 4
    return pl.pallas_call(
        functools.partial(_attn_c_kernel, top_k=top_k),
        grid=(B, nt),
        in_specs=[pl.BlockSpec((1, BRANCH_WIDTH, TILE), lambda b, i: (b, 0, i)),
                  pl.BlockSpec((1, S, HEAD_DIM), lambda b, i: (b, 0, 0)),
                  pl.BlockSpec((1, nt, C_VT_ROWS, TILE), lambda b, i: (b, 0, 0, 0)),
                  pl.BlockSpec((1, LANES, TILE), lambda b, i: (b, 0, i)),
                  pl.BlockSpec((1, S, IDX_DIM), lambda b, i: (b, 0, 0)),
                  pl.BlockSpec((1, IDX_HEADS, TILE), lambda b, i: (b, 0, i)),
                  pl.BlockSpec(bias.shape, lambda b, i: (0, 0, 0, 0))],
        out_specs=pl.BlockSpec((1, TILE, BRANCH_WIDTH), lambda b, i: (b, i, 0)),
        out_shape=jax.ShapeDtypeStruct((B, S, BRANCH_WIDTH), BF16),
        scratch_shapes=[pltpu.VMEM((nt + nt % 2, TILE, TILE), I16), pltpu.VMEM((nt + nt % 2, TILE, TILE), I16),
                        pltpu.VMEM((nt, TILE, TILE), F32),
                        pltpu.VMEM((C_HEADS, C_VT_ROWS, TILE), F32), pltpu.VMEM((BRANCH_WIDTH, TILE), F32)],
        compiler_params=pltpu.CompilerParams(dimension_semantics=("arbitrary", "arbitrary"),
                                             vmem_limit_bytes=_vmem_limit(vmem)),
        name="attn_c",
    )(cqT, ck, cvT, iqT, ik, iwT, bias)


def _merge_kernel(x_ref, oa_ref, ob_ref, oc_ref, gmix_ref, wg_ref, bg_ref, wb_ref, wo_ref, gffn_ref,
                  wrh_ref, wrl_ref, br_ref, x1_ref, h2_ref, gate_ref):
    x = x_ref[...]
    hb = _rms(x, gmix_ref[...]).astype(BF16)
    z = None
    for n, o_ref in enumerate((oa_ref, ob_ref, oc_ref)):
        cs = slice(n * D_MODEL, (n + 1) * D_MODEL)
        gate = _sigmoid(_dot(hb, wg_ref[:, cs]) + bg_ref[:, cs])
        y = _dot(o_ref[...], wb_ref[n])
        z = gate * y if z is None else z + gate * y
    x1 = x + _dot(z.astype(BF16), wo_ref[...])
    x1_ref[...] = x1
    h2 = _rms(x1, gffn_ref[...])
    h2_ref[...] = h2.astype(BF16)

    hi, lo = _split_bf16(h2)
    lg = _dot(hi, wrh_ref[...]) + _dot(lo, wrh_ref[...]) + _dot(hi, wrl_ref[...]) + br_ref[...]
    col = lax.broadcasted_iota(I32, lg.shape, 1).astype(F32)
    big = float(4 * LANES)
    is_g = (col >= N_EXPERTS) & (col < N_EXPERTS + N_GROUPS)
    gl = jnp.where(is_g, lg, -jnp.inf)
    gmax = jnp.max(gl, axis=-1, keepdims=True)
    p_group = 1.0 / jnp.sum(jnp.exp(gl - gmax), axis=-1, keepdims=True)
    g_sel = jnp.min(jnp.where(gl == gmax, col, big), axis=-1, keepdims=True) - N_EXPERTS
    in_g = (col >= g_sel * EXPERTS_PER_GROUP) & (col < (g_sel + 1) * EXPERTS_PER_GROUP)
    el = jnp.where(in_g, lg, -jnp.inf)
    e1 = jnp.max(el, axis=-1, keepdims=True)
    i1 = jnp.min(jnp.where(el == e1, col, big), axis=-1, keepdims=True)
    el2 = jnp.where(col == i1, -jnp.inf, el)
    e2 = jnp.max(el2, axis=-1, keepdims=True)
    i2 = jnp.min(jnp.where(el2 == e2, col, big), axis=-1, keepdims=True)
    t2 = jnp.exp(e2 - e1)
    w1 = p_group / (1.0 + t2)
    w2 = w1 * t2
    gates = jnp.where(col == i1, w1, 0.0) + jnp.where(col == i2, w2, 0.0)
    gate_ref[...] = gates[:, :N_EXPERTS]


def _merge(x2, oa, ob, oc, gmix, wg, bg, wb, wo, gffn, wrh, wrl, br, tm):
    T, D = x2.shape
    full = lambda a: pl.BlockSpec(a.shape, lambda t: (0,) * a.ndim)
    rowb = lambda c: pl.BlockSpec((tm, c), lambda t: (t, 0))
    vmem = 2 * (tm * D * 4 * 2 + 3 * tm * BRANCH_WIDTH * 2 + tm * D * 2 + wg.size * 2 + wb.size * 2 + wo.size * 2
                + 2 * D * LANES * 2) + 10 * tm * D * 4
    return pl.pallas_call(
        _merge_kernel,
        grid=(T // tm,),
        in_specs=[rowb(D), rowb(BRANCH_WIDTH), rowb(BRANCH_WIDTH), rowb(BRANCH_WIDTH), full(gmix), full(wg), full(bg), full(wb),
                  full(wo), full(gffn), full(wrh), full(wrl), full(br)],
        out_specs=[rowb(D), rowb(D), rowb(N_EXPERTS)],
        out_shape=[jax.ShapeDtypeStruct((T, D), F32), jax.ShapeDtypeStruct((T, D), BF16),
                   jax.ShapeDtypeStruct((T, N_EXPERTS), F32)],
        compiler_params=pltpu.CompilerParams(dimension_semantics=("arbitrary",),
                                             vmem_limit_bytes=_vmem_limit(vmem)),
        name="merge",
    )(x2, oa, ob, oc, gmix, wg, bg, wb, wo, gffn, wrh, wrl, br)


def _moe_kernel(x1_ref, h2_ref, gate_ref, wgu_ref, wd_ref, o_ref):
    g = pl.program_id(1)
    h2 = h2_ref[...]
    gate = gate_ref[...]
    lane = lax.broadcasted_iota(I32, gate.shape, 1)
    hids = []
    for u in range(EXPERTS_PER_GROUP):
        gu = _dot(h2, wgu_ref[u])
        a = gu[:, :EXPERT_FF]
        w = jnp.sum(jnp.where(lane == g * EXPERTS_PER_GROUP + u, gate, 0.0), axis=-1, keepdims=True)
        hids.append((a * _sigmoid(a) * gu[:, EXPERT_FF:] * w).astype(BF16))
    y = _dot(jnp.concatenate(hids, axis=1), wd_ref[...])

    @pl.when(g == 0)
    def _():
        o_ref[...] = x1_ref[...] + y

    @pl.when(g > 0)
    def _():
        o_ref[...] += y


def _moe(x1, h2, gate, wgu, wd, tm):
    T, D = x1.shape
    n_e, ff = EXPERTS_PER_GROUP, EXPERT_FF
    vmem = 2 * (tm * D * 4 * 2 + tm * D * 2 + tm * LANES * 4 + n_e * D * 2 * ff * 2 + n_e * ff * D * 2) \
        + 3 * n_e * tm * 2 * ff * 4 + 2 * tm * D * 4
    return pl.pallas_call(
        _moe_kernel,
        grid=(T // tm, N_GROUPS),
        in_specs=[pl.BlockSpec((tm, D), lambda t, g: (t, 0)),
                  pl.BlockSpec((tm, D), lambda t, g: (t, 0)),
                  pl.BlockSpec((tm, N_EXPERTS), lambda t, g: (t, 0)),
                  pl.BlockSpec((n_e, D, 2 * ff), lambda t, g: (g, 0, 0)),
                  pl.BlockSpec((n_e * ff, D), lambda t, g: (g, 0))],
        out_specs=pl.BlockSpec((tm, D), lambda t, g: (t, 0)),
        out_shape=jax.ShapeDtypeStruct((T, D), F32),
        compiler_params=pltpu.CompilerParams(dimension_semantics=("arbitrary", "arbitrary"),
                                             vmem_limit_bytes=_vmem_limit(vmem)),
        name="moe",
    )(x1, h2, gate, wgu, wd)


def _t5_bucket_np(rel):
    half = NUM_BUCKETS // 2
    max_exact = half // 2
    n = np.abs(rel)
    n_f = np.maximum(n, 1).astype(np.float32)
    large = max_exact + (np.log(n_f / np.float32(max_exact)) / np.float32(math.log(MAX_DISTANCE / max_exact))
                         * np.float32(half - max_exact)).astype(np.int32)
    large = np.minimum(large, half - 1)
    return np.where(rel > 0, half, 0) + np.where(n < max_exact, n, large)


def _toeplitz_kernel(v_ref, o_ref):
    v = v_ref[0]
    for d in range(NEAR_TILES):
        x = jnp.broadcast_to(v[d:d + 1, :], (TILE, 2 * TILE))
        o_ref[0, d] = pltpu.roll(x, 0, 1, stride=1, stride_axis=0)[:, :TILE]


def _toeplitz(vals):
    n_heads = vals.shape[0]
    return pl.pallas_call(
        _toeplitz_kernel,
        grid=(n_heads,),
        in_specs=[pl.BlockSpec((1, NEAR_TILES, 2 * TILE), lambda h: (h, 0, 0))],
        out_specs=pl.BlockSpec((1, NEAR_TILES, TILE, TILE), lambda h: (h, 0, 0, 0)),
        out_shape=jax.ShapeDtypeStruct((n_heads, NEAR_TILES, TILE, TILE), F32),
        name="bias_tiles",
    )(vals)


def _bias_tables(rel_bias):
    u = np.arange(2 * TILE)
    off = np.where(u < TILE, u, u - 2 * TILE)
    rel = np.stack([-off - TILE * d for d in range(NEAR_TILES)])
    onehot = (_t5_bucket_np(rel)[..., None] == np.arange(NUM_BUCKETS)).astype(np.float32)
    vals = jnp.einsum("dub,bh->hdu", jnp.asarray(onehot), rel_bias, precision=lax.Precision.HIGHEST)
    tab = _toeplitz(vals)

    far = rel_bias[NUM_BUCKETS // 2 - 1]
    kk = np.arange(TILE)[:, None]
    qc = np.arange(TILE)[None, :] // CHUNK
    kc = np.stack([(kk // CHUNK) - (TILE // CHUNK) * d for d in range(NEAR_TILES)]) + 0 * qc
    a_ok = jnp.asarray(kc <= qc)
    ta = jnp.where(a_ok[None], (tab[:A_HEADS] - far[:A_HEADS, None, None, None]) * LOG2E, NEG)
    b_ok = jnp.asarray((qc - kc[:2] >= 0) & (qc - kc[:2] <= W_CHUNKS))
    tb = jnp.where(b_ok[None], tab[A_HEADS:A_HEADS + B_HEADS, :2] * LOG2E, NEG)
    tb = jnp.concatenate([tb, jnp.full((B_HEADS, 1, TILE, TILE), NEG, F32)], axis=1)
    tc = (tab[A_HEADS + B_HEADS:] - far[A_HEADS + B_HEADS:, None, None, None]) * LOG2E
    return ta.astype(F32), tb.astype(F32), tc.astype(F32)


def _proj_weight_kernel(w_ref, o_ref):
    o_ref[...] = jnp.zeros_like(o_ref)
    o_ref[:, :_IN_COLS] = w_ref[0].astype(BF16)


def _proj_weight(w_in, l):
    _, d, n = w_in.shape
    assert n == _IN_COLS
    rows = 128
    return pl.pallas_call(
        _proj_weight_kernel,
        grid=(d // rows,),
        in_specs=[pl.BlockSpec((1, rows, n), lambda r: (l, r, 0))],
        out_specs=pl.BlockSpec((rows, _W_COLS), lambda r: (r, 0)),
        out_shape=jax.ShapeDtypeStruct((d, _W_COLS), BF16),
        name="proj_weight",
    )(w_in)


def kernel(x, rel_bias, norm_mix_g, w_in, qk_norm_g, diff_lambda, diff_subln_g, sinks, w_branch, w_gate, b_gate,
           w_out, norm_ffn_g, w_router_group, b_router_group, w_router_expert, b_router_expert, w_ff_gate,
           w_ff_up, w_ff_down):
    B, S, D = x.shape
    assert D == D_MODEL and S % TILE == 0
    T = B * S
    top_k = min(TOPK_MAX, S // 4)
    tm_proj = 1024 if S % 1024 == 0 else TILE
    tm_merge = 512 if T % 512 == 0 else TILE
    tm_moe = 1024 if T % 1024 == 0 else TILE

    bias_a, bias_b, bias_c = _bias_tables(rel_bias)
    seg = jnp.asarray(np.kron(np.eye(BRANCH_WIDTH // HEAD_DIM), np.ones((HEAD_DIM, HEAD_DIM))), BF16)
    q_scale = HEAD_DIM ** -0.5

    for l in range(DEPTH):
        lambda_init = 0.8 - 0.6 * math.exp(-0.3 * l)
        qg = qk_norm_g[l]
        tile8 = lambda g: jnp.tile(g, BRANCH_WIDTH // HEAD_DIM)
        gains = ((tile8(qg[0, 0]) * (q_scale * LOG2E))[:, None], tile8(qg[0, 1])[None, :],
                 (tile8(qg[1, 0]) * (q_scale * LOG2E))[:, None], jnp.tile(qg[1, 1], B_KV_HEADS)[None, :],
                 (tile8(qg[2, 0]) * (q_scale * LOG2E))[:, None], qg[2, 1][None, :])
        (aqT, ak, avT, bqT, bk, bvT, cqT, ck, cvT, iqT, ik, iwT) = _proj(
            x, norm_mix_g[l][None, :], _proj_weight(w_in, l), seg, gains, tm_proj)

        oa = _attn_a(aqT, ak, avT, bias_a, diff_lambda[l], diff_subln_g[l][None, :], lambda_init)
        ob = _attn_b(bqT, bk, bvT, bias_b, sinks[l])
        oc = _attn_c(cqT, ck, cvT, iqT, ik, iwT, bias_c, top_k)

        w_r = jnp.concatenate([w_router_expert[l], w_router_group[l],
                               jnp.zeros((D, LANES - N_EXPERTS - N_GROUPS), F32)], axis=1)
        b_r = jnp.concatenate([b_router_expert[l], b_router_group[l],
                               jnp.zeros((LANES - N_EXPERTS - N_GROUPS,), F32)])[None, :]
        wrh = w_r.astype(BF16)
        wrl = (w_r - wrh.astype(F32)).astype(BF16)
        x1, h2, gate = _merge(
            x.reshape(T, D), oa.reshape(T, BRANCH_WIDTH), ob.reshape(T, BRANCH_WIDTH), oc.reshape(T, BRANCH_WIDTH),
            norm_mix_g[l][None, :], w_gate[l].astype(BF16), b_gate[l][None, :], w_branch[l].astype(BF16),
            w_out[l].astype(BF16), norm_ffn_g[l][None, :], wrh, wrl, b_r, tm_merge)

        wgu = jnp.concatenate([w_ff_gate[l], w_ff_up[l]], axis=-1).astype(BF16)
        wd = w_ff_down[l].astype(BF16).reshape(N_EXPERTS * EXPERT_FF, D)
        x = _moe(x1, h2, gate, wgu, wd, tm_moe).reshape(B, S, D)
    return x
```

```python
import functools
import math

import numpy as np
import jax
import jax.numpy as jnp
from jax import lax
from jax.experimental import pallas as pl
from jax.experimental.pallas import tpu as pltpu

F32 = jnp.float32
BF16 = jnp.bfloat16
I32 = jnp.int32
I16 = jnp.int16

D_MODEL = 1024
DEPTH = 2
CHUNK = 64
HEAD_DIM = 64
A_HEADS = 4
A_V_DIM = 2 * HEAD_DIM
B_HEADS = 8
B_KV_HEADS = 2
B_GROUP = B_HEADS // B_KV_HEADS
W_CHUNKS = 2
C_HEADS = 8
IDX_HEADS = 4
IDX_DIM = 32
TOPK_MAX = 256
NUM_BUCKETS = 32
MAX_DISTANCE = 1024
N_BRANCH = 3
BRANCH_WIDTH = 512
N_GROUPS = 4
EXPERTS_PER_GROUP = 4
N_EXPERTS = N_GROUPS * EXPERTS_PER_GROUP
EXPERT_FF = 256
EPS = 1e-6
NEG = -1e30
I16_MIN = -(2 ** 15)

LANES = 128
SUBLANES = 8
TILE = 256
NEAR_TILES = 4
BF16_ROWS = 16
A_VT_ROWS = A_V_DIM + BF16_ROWS
C_VT_ROWS = HEAD_DIM + BF16_ROWS
LOG2E = 1.4426950408889634
SKEW = 5
VMEM_CAP = 60000 * 1024

_C_AQ, _C_AK, _C_AV, _C_BQ, _C_BK, _C_BV, _C_CQ = 0, 512, 1024, 1536, 2048, 2176, 2304
_C_CKV, _C_IQ, _C_IKW = 2816, 2944, 3072
_IN_COLS = 3108
_W_COLS = 3200


def _dot(a, b):
    return jnp.dot(a, b, preferred_element_type=F32)


def _split_bf16(a):
    hi = a.astype(BF16)
    lo = (a - hi.astype(F32)).astype(BF16)
    return hi, lo


def _sigmoid(x):
    return 1.0 / (1.0 + jnp.exp(-x))


def _rms(x, g):
    return x * lax.rsqrt(jnp.mean(x * x, axis=-1, keepdims=True) + EPS) * g


def _colmax8(s):
    r, c = s.shape
    return jnp.max(s.reshape(r // SUBLANES, SUBLANES, c), axis=0)


def _colsum8(s):
    r, c = s.shape
    return jnp.sum(s.reshape(r // SUBLANES, SUBLANES, c), axis=0)


def _vmem_limit(nbytes):
    return int(min(VMEM_CAP, nbytes))


def _proj_kernel(x_ref, g_ref, w_ref, seg_ref, gaq_ref, gak_ref, gbq_ref, gbk_ref, gcq_ref, gck_ref,
                 aqT_ref, ak_ref, avT_ref, bqT_ref, bk_ref, bvT_ref, cqT_ref, ck_ref, cvT_ref,
                 iqT_ref, ik_ref, iwT_ref, *, tm, iw_scale):
    hb = _rms(x_ref[0], g_ref[...]).astype(BF16)
    seg = seg_ref[...]
    n_sub = tm // TILE

    def grp(a, n):
        return _dot(hb, w_ref[:, a:a + n])

    def segnorm(t, g):
        n = t.shape[1]
        ssq = _dot((t * t).astype(BF16), seg[:n, :n])
        return t * lax.rsqrt(ssq * (1.0 / HEAD_DIM) + EPS) * g

    def segnorm_t(t, gcol):
        n = t.shape[0] // HEAD_DIM
        t3 = t.reshape(n, HEAD_DIM, tm)
        ssq = jnp.sum(t3 * t3, axis=1, keepdims=True)
        return (t3 * lax.rsqrt(ssq * (1.0 / HEAD_DIM) + EPS)).reshape(t.shape) * gcol

    ones_rows = (lax.broadcasted_iota(I32, (BF16_ROWS, tm), 0) == 0).astype(F32)

    def put_slabs(ref, tT):
        for s in range(n_sub):
            ref[0, s] = tT[:, s * TILE:(s + 1) * TILE].astype(BF16)

    aqT_ref[0] = segnorm_t(grp(_C_AQ, BRANCH_WIDTH).T, gaq_ref[...]).astype(BF16)
    ak_ref[0] = segnorm(grp(_C_AK, BRANCH_WIDTH), gak_ref[...]).astype(BF16)
    avT = grp(_C_AV, BRANCH_WIDTH).T
    put_slabs(avT_ref, jnp.concatenate(
        [p for h in range(A_HEADS) for p in (avT[h * A_V_DIM:(h + 1) * A_V_DIM, :], ones_rows)], axis=0))
    bqT_ref[0] = segnorm_t(grp(_C_BQ, BRANCH_WIDTH).T, gbq_ref[...]).astype(BF16)
    cqT_ref[0] = segnorm_t(grp(_C_CQ, BRANCH_WIDTH).T, gcq_ref[...]).astype(BF16)
    bk_ref[0] = segnorm(grp(_C_BK, LANES), gbk_ref[...]).astype(BF16)
    bvT = grp(_C_BV, LANES).T
    put_slabs(bvT_ref, jnp.concatenate(
        [p for g in range(B_KV_HEADS) for p in (bvT[g * HEAD_DIM:(g + 1) * HEAD_DIM, :], ones_rows)], axis=0))
    iqT_ref[0] = grp(_C_IQ, LANES).T.astype(BF16)
    ckv = grp(_C_CKV, LANES)
    ck = ckv[:, :HEAD_DIM]
    ssq = jnp.sum(ck * ck, axis=-1, keepdims=True)
    ck_ref[0] = (ck * lax.rsqrt(ssq * (1.0 / HEAD_DIM) + EPS) * gck_ref[...]).astype(BF16)
    put_slabs(cvT_ref, jnp.concatenate([ckv.T[HEAD_DIM:, :], ones_rows], axis=0))
    ikw = grp(_C_IKW, LANES)
    ik_ref[0] = ikw[:, :IDX_DIM].astype(BF16)
    iwT_ref[0] = ikw.T[IDX_DIM:IDX_DIM + IDX_HEADS, :] * iw_scale


def _proj(x, g, w, seg, gains, tm):
    B, S, D = x.shape
    nt = S // TILE
    n_sub = tm // TILE
    full = lambda shape: pl.BlockSpec(shape, lambda b, t: (0,) * len(shape))
    out_shape = [
        jax.ShapeDtypeStruct((B, BRANCH_WIDTH, S), BF16),
        jax.ShapeDtypeStruct((B, S, BRANCH_WIDTH), BF16),
        jax.ShapeDtypeStruct((B, nt, A_HEADS * A_VT_ROWS, TILE), BF16),
        jax.ShapeDtypeStruct((B, BRANCH_WIDTH, S), BF16),
        jax.ShapeDtypeStruct((B, S, LANES), BF16),
        jax.ShapeDtypeStruct((B, nt, B_KV_HEADS * C_VT_ROWS, TILE), BF16),
        jax.ShapeDtypeStruct((B, BRANCH_WIDTH, S), BF16),
        jax.ShapeDtypeStruct((B, S, HEAD_DIM), BF16),
        jax.ShapeDtypeStruct((B, nt, C_VT_ROWS, TILE), BF16),
        jax.ShapeDtypeStruct((B, LANES, S), BF16),
        jax.ShapeDtypeStruct((B, S, IDX_DIM), BF16),
        jax.ShapeDtypeStruct((B, IDX_HEADS, S), F32),
    ]
    colT = lambda r: pl.BlockSpec((1, r, tm), lambda b, t: (b, 0, t))
    row = lambda c: pl.BlockSpec((1, tm, c), lambda b, t: (b, t, 0))
    slab = lambda r: pl.BlockSpec((1, n_sub, r, TILE), lambda b, t: (b, t, 0, 0))
    out_specs = [colT(BRANCH_WIDTH), row(BRANCH_WIDTH), slab(A_HEADS * A_VT_ROWS), colT(BRANCH_WIDTH),
                 row(LANES),
                 slab(B_KV_HEADS * C_VT_ROWS), colT(BRANCH_WIDTH), row(HEAD_DIM), slab(C_VT_ROWS), colT(LANES), row(IDX_DIM),
                 pl.BlockSpec((1, IDX_HEADS, tm), lambda b, t: (b, 0, t))]
    in_specs = [pl.BlockSpec((1, tm, D), lambda b, t: (b, t, 0)), full((1, D)), full((D, _W_COLS)),
                full((BRANCH_WIDTH, BRANCH_WIDTH))] + [full(gn.shape) for gn in gains]
    vmem = 2 * (tm * D * 4 + D * _W_COLS * 2 + BRANCH_WIDTH * BRANCH_WIDTH * 2 + tm * 3400 * 2) + 24 * tm * BRANCH_WIDTH * 4
    return pl.pallas_call(
        functools.partial(_proj_kernel, tm=tm, iw_scale=IDX_HEADS ** -0.5 * IDX_DIM ** -0.5),
        grid=(B, S // tm), in_specs=in_specs, out_specs=out_specs, out_shape=out_shape,
        compiler_params=pltpu.CompilerParams(dimension_semantics=("arbitrary", "arbitrary"),
                                             vmem_limit_bytes=_vmem_limit(vmem)),
        name="proj",
    )(x, g, w, seg, *gains)


def _online_step(s, m_old, vt, acc_ref, ch):
    if s.dtype == BF16:
        r, c = s.shape
        cm = jnp.max(s.reshape(r // BF16_ROWS, BF16_ROWS, c), axis=0).astype(F32)
        m_new = jnp.maximum(m_old, jnp.max(cm, axis=0, keepdims=True))
        e = jnp.exp2(s - m_new.astype(BF16))
    else:
        m_new = jnp.maximum(m_old, jnp.max(_colmax8(s), axis=0, keepdims=True))
        e = jnp.exp2(s - m_new).astype(BF16)
    alpha = jnp.exp2(m_old - m_new)
    acc_ref[ch] = acc_ref[ch] * alpha + _dot(vt, e)
    return m_new


def _sweep_key_tiles(i, n_chain, scores, vt, acc_ref):
    n_far = jnp.maximum(i - (NEAR_TILES - 1), 0)

    def step(tiles, ms):
        ms = list(ms)
        chains = [(j, d, ch) for (j, d) in tiles for ch in range(n_chain)]
        pending = [scores(*c) for c in chains[:SKEW]]
        for n, (j, d, ch) in enumerate(chains):
            s = pending.pop(0)
            if n + SKEW < len(chains):
                pending.append(scores(*chains[n + SKEW]))
            ms[ch] = _online_step(s, ms[ch], vt(j, ch), acc_ref, ch)
        return tuple(ms)

    def near_pairs(ms):
        ms = step([(i - 3, 3), (i - 2, 2)], ms)
        return step([(i - 1, 1), (i, 0)], ms)

    def near_singles(ms):
        return lax.fori_loop(0, i + 1, lambda t, ms: step([(t, i - t)], ms), ms)

    assert NEAR_TILES == 4
    ms = tuple(jnp.full((1, TILE), -jnp.inf, F32) for _ in range(n_chain))
    ms = lax.fori_loop(0, n_far // 4, lambda p, ms: step([(4 * p + u, None) for u in range(4)], ms), ms)
    rest = n_far - n_far % 4
    ms = lax.cond(n_far % 4 >= 2, lambda ms: step([(rest, None), (rest + 1, None)], ms), lambda ms: ms, ms)
    ms = lax.cond(n_far % 2 == 1, lambda ms: step([(n_far - 1, None)], ms), lambda ms: ms, ms)
    lax.cond(i >= NEAR_TILES - 1, near_pairs, near_singles, ms)


def _attn_a_kernel(qT_ref, k_ref, vT_ref, bias_ref, lam_ref, sub_ref, o_ref, q2_ref, acc_ref, *, lambda_init):
    i = pl.program_id(1)
    lp = lam_ref[...]
    lam = (jnp.exp(jnp.sum(lp[0:1] * lp[1:2], axis=-1, keepdims=True))
           - jnp.exp(jnp.sum(lp[2:3] * lp[3:4], axis=-1, keepdims=True)) + lambda_init)
    row = lax.broadcasted_iota(I32, (2 * HEAD_DIM, TILE), 0)
    n_chain = 2 * A_HEADS

    for h in range(A_HEADS):
        qh = qT_ref[0, h * A_V_DIM:(h + 1) * A_V_DIM, :]
        zero = jnp.zeros_like(qh)
        q2_ref[2 * h] = jnp.where(row < HEAD_DIM, qh, zero)
        q2_ref[2 * h + 1] = jnp.where(row >= HEAD_DIM, qh, zero)
    acc_ref[...] = jnp.zeros_like(acc_ref)

    def scores(j, d, ch):
        h = ch // 2
        rows = pl.ds(pl.multiple_of(j * TILE, TILE), TILE)
        s = _dot(k_ref[0, rows, h * A_V_DIM:(h + 1) * A_V_DIM], q2_ref[ch])
        return s if d is None else s + bias_ref[h, d]

    def vt(j, ch):
        h = ch // 2
        return vT_ref[0, j, h * A_VT_ROWS:(h + 1) * A_VT_ROWS, :]

    _sweep_key_tiles(i, n_chain, scores, vt, acc_ref)

    for h in range(A_HEADS):
        a0 = acc_ref[2 * h]
        a1 = acc_ref[2 * h + 1]
        r0 = 1.0 / a0[A_V_DIM:A_V_DIM + 1, :]
        r1 = 1.0 / a1[A_V_DIM:A_V_DIM + 1, :]
        outT = a0[:A_V_DIM, :] * r0 - lam * (a1[:A_V_DIM, :] * r1)
        out = _rms(outT.T, sub_ref[...]) * (1.0 - lambda_init)
        o_ref[0, :, h * A_V_DIM:(h + 1) * A_V_DIM] = out.astype(BF16)


def _attn_a(aqT, ak, avT, bias, lam_par, subln_g, lambda_init):
    B, _, S = aqT.shape
    nt = S // TILE
    vt_rows = A_HEADS * A_VT_ROWS
    vmem = 2 * (BRANCH_WIDTH * TILE * 2 + S * BRANCH_WIDTH * 2 + S * vt_rows * 2 + bias.size * 4 + TILE * BRANCH_WIDTH * 2) \
        + 8 * A_V_DIM * TILE * 2 + 8 * A_VT_ROWS * TILE * 4 + 48 * TILE * TILE * 4
    return pl.pallas_call(
        functools.partial(_attn_a_kernel, lambda_init=lambda_init),
        grid=(B, nt),
        in_specs=[pl.BlockSpec((1, BRANCH_WIDTH, TILE), lambda b, i: (b, 0, i)),
                  pl.BlockSpec((1, S, BRANCH_WIDTH), lambda b, i: (b, 0, 0)),
                  pl.BlockSpec((1, nt, vt_rows, TILE), lambda b, i: (b, 0, 0, 0)),
                  pl.BlockSpec(bias.shape, lambda b, i: (0, 0, 0, 0)),
                  pl.BlockSpec((4, HEAD_DIM), lambda b, i: (0, 0)),
                  pl.BlockSpec((1, A_V_DIM), lambda b, i: (0, 0))],
        out_specs=pl.BlockSpec((1, TILE, BRANCH_WIDTH), lambda b, i: (b, i, 0)),
        out_shape=jax.ShapeDtypeStruct((B, S, BRANCH_WIDTH), BF16),
        scratch_shapes=[pltpu.VMEM((2 * A_HEADS, A_V_DIM, TILE), BF16),
                        pltpu.VMEM((2 * A_HEADS, A_VT_ROWS, TILE), F32)],
        compiler_params=pltpu.CompilerParams(dimension_semantics=("arbitrary", "arbitrary"),
                                             vmem_limit_bytes=_vmem_limit(vmem)),
        name="attn_a",
    )(aqT, ak, avT, bias, lam_par, subln_g)


def _attn_b_kernel(sink_ref, qT_ref, k_ref, vT_ref, bias_ref, o_ref, oT_ref):
    i = pl.program_id(1)
    jp = jnp.maximum(i - 1, 0)
    p_idx = jnp.where(i > 0, 1, 2)
    cur = pl.ds(pl.multiple_of(i * TILE, TILE), TILE)
    prev = pl.ds(pl.multiple_of(jp * TILE, TILE), TILE)

    def scores(h):
        qh = qT_ref[0, h * HEAD_DIM:(h + 1) * HEAD_DIM, :]
        zero = jnp.zeros_like(qh)
        q2 = jnp.concatenate([qh, zero] if h < B_GROUP else [zero, qh], axis=0)
        return (_dot(k_ref[0, cur, :], q2) + bias_ref[h, 0], _dot(k_ref[0, prev, :], q2) + bias_ref[h, p_idx])

    def finish(h, s):
        sc, sp = s
        gs = slice((h // B_GROUP) * C_VT_ROWS, (h // B_GROUP + 1) * C_VT_ROWS)
        sink = sink_ref[h] * LOG2E
        m = jnp.max(jnp.maximum(_colmax8(sc), _colmax8(sp)), axis=0, keepdims=True)
        m = jnp.maximum(m, sink)
        ec = jnp.exp2(sc - m).astype(BF16)
        ep = jnp.exp2(sp - m).astype(BF16)
        outT = _dot(vT_ref[0, i, gs, :], ec) + _dot(vT_ref[0, jp, gs, :], ep)
        den = outT[HEAD_DIM:HEAD_DIM + 1, :] + jnp.exp2(sink - m)
        oT_ref[h * HEAD_DIM:(h + 1) * HEAD_DIM, :] = outT[:HEAD_DIM, :] * (1.0 / den)

    pending = [scores(h) for h in range(SKEW)]
    for h in range(B_HEADS):
        s = pending.pop(0)
        if h + SKEW < B_HEADS:
            pending.append(scores(h + SKEW))
        finish(h, s)
    o_ref[0] = oT_ref[...].T.astype(BF16)


def _attn_b(bqT, bk, bvT, bias, sinks):
    B, _, S = bqT.shape
    nt = S // TILE
    vmem = 2 * (BRANCH_WIDTH * TILE * 2 + 2 * S * LANES * 2 + S * LANES * 2 + bias.size * 4 + TILE * BRANCH_WIDTH * 2) \
        + 24 * TILE * TILE * 4
    return pl.pallas_call(
        _attn_b_kernel,
        grid=(B, nt),
        in_specs=[pl.BlockSpec(memory_space=pltpu.SMEM),
                  pl.BlockSpec((1, BRANCH_WIDTH, TILE), lambda b, i: (b, 0, i)),
                  pl.BlockSpec((1, S, LANES), lambda b, i: (b, 0, 0)),
                  pl.BlockSpec((1, nt, B_KV_HEADS * C_VT_ROWS, TILE), lambda b, i: (b, 0, 0, 0)),
                  pl.BlockSpec(bias.shape, lambda b, i: (0, 0, 0, 0))],
        out_specs=pl.BlockSpec((1, TILE, BRANCH_WIDTH), lambda b, i: (b, i, 0)),
        out_shape=jax.ShapeDtypeStruct((B, S, BRANCH_WIDTH), BF16),
        scratch_shapes=[pltpu.VMEM((BRANCH_WIDTH, TILE), F32)],
        compiler_params=pltpu.CompilerParams(dimension_semantics=("arbitrary", "arbitrary"),
                                             vmem_limit_bytes=_vmem_limit(vmem)),
        name="attn_b",
    )(sinks, bqT, bk, bvT, bias)


def _attn_c_kernel(qT_ref, k_ref, vT_ref, iqT_ref, ik_ref, iwT_ref, bias_ref, o_ref,
                   hi_ref, lo_ref, mb_ref, acc_ref, oT_ref, *, top_k):
    i = pl.program_id(1)
    n_t = i + 1
    krow = lax.broadcasted_iota(I32, (TILE, TILE), 0)
    qcol = lax.broadcasted_iota(I32, (TILE, TILE), 1)
    allowed = (krow // CHUNK) <= (qcol // CHUNK)

    def idx_keys(j, diag):
        ikt = ik_ref[0, pl.ds(pl.multiple_of(j * TILE, TILE), TILE), :]
        lgs = [_dot(ikt, iqT_ref[0, hh * IDX_DIM:(hh + 1) * IDX_DIM, :]) for hh in range(IDX_HEADS)]
        sc = jnp.zeros((TILE, TILE), F32)
        for hh in range(IDX_HEADS):
            sc = sc + jnp.maximum(lgs[hh], 0.0) * iwT_ref[0, hh:hh + 1, :]
        if diag:
            sc = jnp.where(allowed, sc, NEG)
        bits = lax.bitcast_convert_type(sc, I32)
        hi_b = (bits >> 16).astype(I16)
        lo_b = bits.astype(I16)
        sign = jnp.where(hi_b < 0, jnp.int16(-1), jnp.int16(0))
        hi_ref[j] = hi_b ^ (sign & 0x7FFF)
        lo_ref[j] = lo_b ^ sign ^ I16_MIN

    def fill_pair(p, carry):
        idx_keys(2 * p, False)
        idx_keys(2 * p + 1, False)
        return carry

    lax.fori_loop(0, i // 2, fill_pair, 0)

    @pl.when(i % 2 == 1)
    def _():
        idx_keys(i - 1, False)

    idx_keys(i, True)

    n_pair = (n_t + 1) // 2

    @pl.when(n_t % 2 == 1)
    def _():
        hi_ref[n_t] = jnp.full((TILE, TILE), I16_MIN, I16)
        lo_ref[n_t] = jnp.full((TILE, TILE), I16_MIN, I16)

    groups = TILE // BF16_ROWS

    def rows16(ref, j):
        return ref[j].reshape(groups, BF16_ROWS, TILE)

    def bcast16(v):
        return jnp.broadcast_to(v, (BF16_ROWS, TILE)).astype(I16)

    def count(pred):
        def body(p, cs):
            cs = list(cs)
            for u in (0, 1):
                j = 2 * p + u
                hi, lo = rows16(hi_ref, j), rows16(lo_ref, j)
                for r in range(groups):
                    cs[r % len(cs)] = cs[r % len(cs)] + pred(hi[r], lo[r], j, r).astype(I16)
            return tuple(cs)
        cs = lax.fori_loop(0, n_pair, body, (jnp.zeros((BF16_ROWS, TILE), I16),) * 4)
        c = (cs[0] + cs[1]) + (cs[2] + cs[3])
        return jnp.sum(c.astype(I32), axis=0, keepdims=True)

    def search(n_bits, accept):
        def step(b, t):
            cand = t + lax.shift_left(jnp.int32(1), n_bits - 1 - b)
            return jnp.where(accept(cand), cand, t)
        return lax.fori_loop(0, n_bits, step, jnp.full((1, TILE), I16_MIN, I32))

    def hi_accept(cand):
        c16 = bcast16(cand)
        return count(lambda hi, lo, j, r: hi >= c16) >= top_k

    t_hi = search(16, hi_accept)
    th16 = bcast16(t_hi)
    cnt_above = count(lambda hi, lo, j, r: hi > th16)
    r_lo = top_k - cnt_above

    def bucket_only(p, carry):
        for u in (0, 1):
            j = 2 * p + u
            hi, lo = rows16(hi_ref, j), rows16(lo_ref, j)
            for r in range(groups):
                lo_ref[j, r * BF16_ROWS:(r + 1) * BF16_ROWS, :] = jnp.where(hi[r] == th16, lo[r], I16_MIN)
        return carry

    lax.fori_loop(0, n_pair, bucket_only, 0)

    def lo_accept(cand):
        c16 = bcast16(cand)
        return count(lambda hi, lo, j, r: lo >= c16) >= r_lo

    t_lo = search(16, lo_accept)
    tl16 = bcast16(t_lo)
    cnt_gt = count(lambda hi, lo, j, r: lo > tl16)
    r_eq = (r_lo - cnt_gt).astype(F32)

    tri = (krow >= qcol).astype(BF16)
    one, zero, neg16 = jnp.ones((), BF16), jnp.zeros((), BF16), jnp.full((), NEG, BF16)

    def mask_tile(j, ties_before, diag):
        hi, lo = rows16(hi_ref, j), rows16(lo_ref, j)
        eqs = [(lo[r] == tl16) & (hi[r] == th16) for r in range(groups)]
        ties = _dot(tri, jnp.concatenate([jnp.where(e, one, zero) for e in eqs], axis=0)) + ties_before
        over = (ties - r_eq).astype(BF16)
        rows = []
        for r in range(groups):
            kept_tie = eqs[r] & (over[r * BF16_ROWS:(r + 1) * BF16_ROWS, :] <= zero)
            rows.append(jnp.where((hi[r] > th16) | (lo[r] > tl16) | kept_tie, zero, neg16))
        mb = jnp.concatenate(rows, axis=0)
        if diag:
            mb = jnp.where(allowed, mb.astype(F32), NEG).astype(BF16)
        mb_ref[j] = mb
        return ties[TILE - 1:TILE, :]

    ties_before = lax.fori_loop(0, i // 2, lambda p, c: mask_tile(2 * p + 1, mask_tile(2 * p, c, False), False),
                                jnp.zeros((1, TILE), F32))
    ties_before = lax.cond(i % 2 == 1, lambda c: mask_tile(i - 1, c, False), lambda c: c, ties_before)
    mask_tile(i, ties_before, True)

    acc_ref[...] = jnp.zeros_like(acc_ref)

    def scores(j, d, h):
        kt = k_ref[0, pl.ds(pl.multiple_of(j * TILE, TILE), TILE), :]
        s = _dot(kt, qT_ref[0, h * HEAD_DIM:(h + 1) * HEAD_DIM, :]).astype(BF16) + mb_ref[j]
        return s if d is None else s + bias_ref[h, d]

    _sweep_key_tiles(i, C_HEADS, scores, lambda j, h: vT_ref[0, j], acc_ref)

    for h in range(C_HEADS):
        a = acc_ref[h]
        oT_ref[h * HEAD_DIM:(h + 1) * HEAD_DIM, :] = a[:HEAD_DIM, :] * (1.0 / a[HEAD_DIM:HEAD_DIM + 1, :])
    o_ref[0] = oT_ref[...].T.astype(BF16)


def _attn_c(cqT, ck, cvT, iqT, ik, iwT, bias, top_k):
    B, _, S = cqT.shape
    nt = S // TILE
    vmem = 2 * (BRANCH_WIDTH * TILE * 2 + 2 * S * LANES * 2 + S * HEAD_DIM * 2 + LANES * TILE * 2 + 8 * TILE * 4
                + bias.size * 4 + TILE * BRANCH_WIDTH * 2) + 2 * S * TILE * 4 + 48 * TILE * TILE * 4
    return pl.pallas_call(
        functools.partial(_attn_c_kernel, top_k=top_k),
        grid=(B, nt),
        in_specs=[pl.BlockSpec((1, BRANCH_WIDTH, TILE), lambda b, i: (b, 0, i)),
                  pl.BlockSpec((1, S, HEAD_DIM), lambda b, i: (b, 0, 0)),
                  pl.BlockSpec((1, nt, C_VT_ROWS, TILE), lambda b, i: (b, 0, 0, 0)),
                  pl.BlockSpec((1, LANES, TILE), lambda b, i: (b, 0, i)),
                  pl.BlockSpec((1, S, IDX_DIM), lambda b, i: (b, 0, 0)),
                  pl.BlockSpec((1, IDX_HEADS, TILE), lambda b, i: (b, 0, i)),
                  pl.BlockSpec(bias.shape, lambda b, i: (0, 0, 0, 0))],
        out_specs=pl.BlockSpec((1, TILE, BRANCH_WIDTH), lambda b, i: (b, i, 0)),
        out_shape=jax.ShapeDtypeStruct((B, S, BRANCH_WIDTH), BF16),
        scratch_shapes=[pltpu.VMEM((nt + nt % 2, TILE, TILE), I16), pltpu.VMEM((nt + nt % 2, TILE, TILE), I16),
                        pltpu.VMEM((nt, TILE, TILE), BF16),
                        pltpu.VMEM((C_HEADS, C_VT_ROWS, TILE), F32), pltpu.VMEM((BRANCH_WIDTH, TILE), F32)],
        compiler_params=pltpu.CompilerParams(dimension_semantics=("arbitrary", "arbitrary"),
                                             vmem_limit_bytes=_vmem_limit(vmem)),
        name="attn_c",
    )(cqT, ck, cvT, iqT, ik, iwT, bias)


def _merge_kernel(x_ref, oa_ref, ob_ref, oc_ref, gmix_ref, wg_ref, bg_ref, wb_ref, wo_ref, gffn_ref,
                  wrh_ref, wrl_ref, br_ref, x1_ref, h2_ref, gate_ref):
    x = x_ref[...]
    hb = _rms(x, gmix_ref[...]).astype(BF16)
    z = None
    for n, o_ref in enumerate((oa_ref, ob_ref, oc_ref)):
        cs = slice(n * D_MODEL, (n + 1) * D_MODEL)
        gate = _sigmoid(_dot(hb, wg_ref[:, cs]) + bg_ref[:, cs])
        y = _dot(o_ref[...], wb_ref[n])
        z = gate * y if z is None else z + gate * y
    x1 = x + _dot(z.astype(BF16), wo_ref[...])
    x1_ref[...] = x1
    h2 = _rms(x1, gffn_ref[...])
    h2_ref[...] = h2.astype(BF16)

    hi, lo = _split_bf16(h2)
    lg = _dot(hi, wrh_ref[...]) + _dot(lo, wrh_ref[...]) + _dot(hi, wrl_ref[...]) + br_ref[...]
    col = lax.broadcasted_iota(I32, lg.shape, 1).astype(F32)
    big = float(4 * LANES)
    is_g = (col >= N_EXPERTS) & (col < N_EXPERTS + N_GROUPS)
    gl = jnp.where(is_g, lg, -jnp.inf)
    gmax = jnp.max(gl, axis=-1, keepdims=True)
    p_group = 1.0 / jnp.sum(jnp.exp(gl - gmax), axis=-1, keepdims=True)
    g_sel = jnp.min(jnp.where(gl == gmax, col, big), axis=-1, keepdims=True) - N_EXPERTS
    in_g = (col >= g_sel * EXPERTS_PER_GROUP) & (col < (g_sel + 1) * EXPERTS_PER_GROUP)
    el = jnp.where(in_g, lg, -jnp.inf)
    e1 = jnp.max(el, axis=-1, keepdims=True)
    i1 = jnp.min(jnp.where(el == e1, col, big), axis=-1, keepdims=True)
    el2 = jnp.where(col == i1, -jnp.inf, el)
    e2 = jnp.max(el2, axis=-1, keepdims=True)
    i2 = jnp.min(jnp.where(el2 == e2, col, big), axis=-1, keepdims=True)
    t2 = jnp.exp(e2 - e1)
    w1 = p_group / (1.0 + t2)
    w2 = w1 * t2
    gates = jnp.where(col == i1, w1, 0.0) + jnp.where(col == i2, w2, 0.0)
    gate_ref[...] = gates[:, :N_EXPERTS]


def _merge(x2, oa, ob, oc, gmix, wg, bg, wb, wo, gffn, wrh, wrl, br, tm):
    T, D = x2.shape
    full = lambda a: pl.BlockSpec(a.shape, lambda t: (0,) * a.ndim)
    rowb = lambda c: pl.BlockSpec((tm, c), lambda t: (t, 0))
    vmem = 2 * (tm * D * 4 * 2 + 3 * tm * BRANCH_WIDTH * 2 + tm * D * 2 + wg.size * 2 + wb.size * 2 + wo.size * 2
                + 2 * D * LANES * 2) + 10 * tm * D * 4
    return pl.pallas_call(
        _merge_kernel,
        grid=(T // tm,),
        in_specs=[rowb(D), rowb(BRANCH_WIDTH), rowb(BRANCH_WIDTH), rowb(BRANCH_WIDTH), full(gmix), full(wg), full(bg), full(wb),
                  full(wo), full(gffn), full(wrh), full(wrl), full(br)],
        out_specs=[rowb(D), rowb(D), rowb(N_EXPERTS)],
        out_shape=[jax.ShapeDtypeStruct((T, D), F32), jax.ShapeDtypeStruct((T, D), BF16),
                   jax.ShapeDtypeStruct((T, N_EXPERTS), F32)],
        compiler_params=pltpu.CompilerParams(dimension_semantics=("arbitrary",),
                                             vmem_limit_bytes=_vmem_limit(vmem)),
        name="merge",
    )(x2, oa, ob, oc, gmix, wg, bg, wb, wo, gffn, wrh, wrl, br)


def _moe_kernel(x1_ref, h2_ref, gate_ref, wgu_ref, wd_ref, o_ref):
    g = pl.program_id(1)
    h2 = h2_ref[...]
    gate = gate_ref[...]
    lane = lax.broadcasted_iota(I32, gate.shape, 1)
    hids = []
    for u in range(EXPERTS_PER_GROUP):
        gu = _dot(h2, wgu_ref[u])
        a = gu[:, :EXPERT_FF]
        w = jnp.sum(jnp.where(lane == g * EXPERTS_PER_GROUP + u, gate, 0.0), axis=-1, keepdims=True)
        hids.append((a * _sigmoid(a) * gu[:, EXPERT_FF:] * w).astype(BF16))
    y = _dot(jnp.concatenate(hids, axis=1), wd_ref[...])

    @pl.when(g == 0)
    def _():
        o_ref[...] = x1_ref[...] + y

    @pl.when(g > 0)
    def _():
        o_ref[...] += y


def _moe(x1, h2, gate, wgu, wd, tm):
    T, D = x1.shape
    n_e, ff = EXPERTS_PER_GROUP, EXPERT_FF
    vmem = 2 * (tm * D * 4 * 2 + tm * D * 2 + tm * LANES * 4 + n_e * D * 2 * ff * 2 + n_e * ff * D * 2) \
        + 3 * n_e * tm * 2 * ff * 4 + 2 * tm * D * 4
    return pl.pallas_call(
        _moe_kernel,
        grid=(T // tm, N_GROUPS),
        in_specs=[pl.BlockSpec((tm, D), lambda t, g: (t, 0)),
                  pl.BlockSpec((tm, D), lambda t, g: (t, 0)),
                  pl.BlockSpec((tm, N_EXPERTS), lambda t, g: (t, 0)),
                  pl.BlockSpec((n_e, D, 2 * ff), lambda t, g: (g, 0, 0)),
                  pl.BlockSpec((n_e * ff, D), lambda t, g: (g, 0))],
        out_specs=pl.BlockSpec((tm, D), lambda t, g: (t, 0)),
        out_shape=jax.ShapeDtypeStruct((T, D), F32),
        compiler_params=pltpu.CompilerParams(dimension_semantics=("arbitrary", "arbitrary"),
                                             vmem_limit_bytes=_vmem_limit(vmem)),
        name="moe",
    )(x1, h2, gate, wgu, wd)


def _t5_bucket_np(rel):
    half = NUM_BUCKETS // 2
    max_exact = half // 2
    n = np.abs(rel)
    n_f = np.maximum(n, 1).astype(np.float32)
    large = max_exact + (np.log(n_f / np.float32(max_exact)) / np.float32(math.log(MAX_DISTANCE / max_exact))
                         * np.float32(half - max_exact)).astype(np.int32)
    large = np.minimum(large, half - 1)
    return np.where(rel > 0, half, 0) + np.where(n < max_exact, n, large)


def _toeplitz_kernel(v_ref, o_ref):
    v = v_ref[0]
    for d in range(NEAR_TILES):
        x = jnp.broadcast_to(v[d:d + 1, :], (TILE, 2 * TILE))
        o_ref[0, d] = pltpu.roll(x, 0, 1, stride=1, stride_axis=0)[:, :TILE]


def _toeplitz(vals):
    n_heads = vals.shape[0]
    return pl.pallas_call(
        _toeplitz_kernel,
        grid=(n_heads,),
        in_specs=[pl.BlockSpec((1, NEAR_TILES, 2 * TILE), lambda h: (h, 0, 0))],
        out_specs=pl.BlockSpec((1, NEAR_TILES, TILE, TILE), lambda h: (h, 0, 0, 0)),
        out_shape=jax.ShapeDtypeStruct((n_heads, NEAR_TILES, TILE, TILE), F32),
        name="bias_tiles",
    )(vals)


def _bias_tables(rel_bias):
    u = np.arange(2 * TILE)
    off = np.where(u < TILE, u, u - 2 * TILE)
    rel = np.stack([-off - TILE * d for d in range(NEAR_TILES)])
    onehot = (_t5_bucket_np(rel)[..., None] == np.arange(NUM_BUCKETS)).astype(np.float32)
    vals = jnp.einsum("dub,bh->hdu", jnp.asarray(onehot), rel_bias, precision=lax.Precision.HIGHEST)
    tab = _toeplitz(vals)

    far = rel_bias[NUM_BUCKETS // 2 - 1]
    kk = np.arange(TILE)[:, None]
    qc = np.arange(TILE)[None, :] // CHUNK
    kc = np.stack([(kk // CHUNK) - (TILE // CHUNK) * d for d in range(NEAR_TILES)]) + 0 * qc
    a_ok = jnp.asarray(kc <= qc)
    ta = jnp.where(a_ok[None], (tab[:A_HEADS] - far[:A_HEADS, None, None, None]) * LOG2E, NEG)
    b_ok = jnp.asarray((qc - kc[:2] >= 0) & (qc - kc[:2] <= W_CHUNKS))
    tb = jnp.where(b_ok[None], tab[A_HEADS:A_HEADS + B_HEADS, :2] * LOG2E, NEG)
    tb = jnp.concatenate([tb, jnp.full((B_HEADS, 1, TILE, TILE), NEG, F32)], axis=1)
    tc = (tab[A_HEADS + B_HEADS:] - far[A_HEADS + B_HEADS:, None, None, None]) * LOG2E
    return ta.astype(F32), tb.astype(F32), tc.astype(BF16)


def _proj_weight_kernel(w_ref, o_ref):
    o_ref[...] = jnp.zeros_like(o_ref)
    o_ref[:, :_IN_COLS] = w_ref[0].astype(BF16)


def _proj_weight(w_in, l):
    _, d, n = w_in.shape
    assert n == _IN_COLS
    rows = 128
    return pl.pallas_call(
        _proj_weight_kernel,
        grid=(d // rows,),
        in_specs=[pl.BlockSpec((1, rows, n), lambda r: (l, r, 0))],
        out_specs=pl.BlockSpec((rows, _W_COLS), lambda r: (r, 0)),
        out_shape=jax.ShapeDtypeStruct((d, _W_COLS), BF16),
        name="proj_weight",
    )(w_in)


def kernel(x, rel_bias, norm_mix_g, w_in, qk_norm_g, diff_lambda, diff_subln_g, sinks, w_branch, w_gate, b_gate,
           w_out, norm_ffn_g, w_router_group, b_router_group, w_router_expert, b_router_expert, w_ff_gate,
           w_ff_up, w_ff_down):
    B, S, D = x.shape
    assert D == D_MODEL and S % TILE == 0
    T = B * S
    top_k = min(TOPK_MAX, S // 4)
    tm_proj = 1024 if S % 1024 == 0 else TILE
    tm_merge = 512 if T % 512 == 0 else TILE
    tm_moe = 1024 if T % 1024 == 0 else TILE

    bias_a, bias_b, bias_c = _bias_tables(rel_bias)
    seg = jnp.asarray(np.kron(np.eye(BRANCH_WIDTH // HEAD_DIM), np.ones((HEAD_DIM, HEAD_DIM))), BF16)
    q_scale = HEAD_DIM ** -0.5

    for l in range(DEPTH):
        lambda_init = 0.8 - 0.6 * math.exp(-0.3 * l)
        qg = qk_norm_g[l]
        tile8 = lambda g: jnp.tile(g, BRANCH_WIDTH // HEAD_DIM)
        gains = ((tile8(qg[0, 0]) * (q_scale * LOG2E))[:, None], tile8(qg[0, 1])[None, :],
                 (tile8(qg[1, 0]) * (q_scale * LOG2E))[:, None], jnp.tile(qg[1, 1], B_KV_HEADS)[None, :],
                 (tile8(qg[2, 0]) * (q_scale * LOG2E))[:, None], qg[2, 1][None, :])
        (aqT, ak, avT, bqT, bk, bvT, cqT, ck, cvT, iqT, ik, iwT) = _proj(
            x, norm_mix_g[l][None, :], _proj_weight(w_in, l), seg, gains, tm_proj)

        oa = _attn_a(aqT, ak, avT, bias_a, diff_lambda[l], diff_subln_g[l][None, :], lambda_init)
        ob = _attn_b(bqT, bk, bvT, bias_b, sinks[l])
        oc = _attn_c(cqT, ck, cvT, iqT, ik, iwT, bias_c, top_k)

        w_r = jnp.concatenate([w_router_expert[l], w_router_group[l],
                               jnp.zeros((D, LANES - N_EXPERTS - N_GROUPS), F32)], axis=1)
        b_r = jnp.concatenate([b_router_expert[l], b_router_group[l],
                               jnp.zeros((LANES - N_EXPERTS - N_GROUPS,), F32)])[None, :]
        wrh = w_r.astype(BF16)
        wrl = (w_r - wrh.astype(F32)).astype(BF16)
        x1, h2, gate = _merge(
            x.reshape(T, D), oa.reshape(T, BRANCH_WIDTH), ob.reshape(T, BRANCH_WIDTH), oc.reshape(T, BRANCH_WIDTH),
            norm_mix_g[l][None, :], w_gate[l].astype(BF16), b_gate[l][None, :], w_branch[l].astype(BF16),
            w_out[l].astype(BF16), norm_ffn_g[l][None, :], wrh, wrl, b_r, tm_merge)

        wgu = jnp.concatenate([w_ff_gate[l], w_ff_up[l]], axis=-1).astype(BF16)
        wd = w_ff_down[l].astype(BF16).reshape(N_EXPERTS * EXPERT_FF, D)
        x = _moe(x1, h2, gate, wgu, wd, tm_moe).reshape(B, S, D)
    return x
```

```python
import functools
import math

import numpy as np
import jax
import jax.numpy as jnp
from jax import lax
from jax.experimental import pallas as pl
from jax.experimental.pallas import tpu as pltpu

F32 = jnp.float32
BF16 = jnp.bfloat16
I32 = jnp.int32
I16 = jnp.int16

D_MODEL = 1024
DEPTH = 2
CHUNK = 64
HEAD_DIM = 64
A_HEADS = 4
A_V_DIM = 2 * HEAD_DIM
B_HEADS = 8
B_KV_HEADS = 2
B_GROUP = B_HEADS // B_KV_HEADS
W_CHUNKS = 2
C_HEADS = 8
IDX_HEADS = 4
IDX_DIM = 32
TOPK_MAX = 256
NUM_BUCKETS = 32
MAX_DISTANCE = 1024
N_BRANCH = 3
BRANCH_WIDTH = 512
N_GROUPS = 4
EXPERTS_PER_GROUP = 4
N_EXPERTS = N_GROUPS * EXPERTS_PER_GROUP
EXPERT_FF = 256
EPS = 1e-6
NEG = -1e30
I16_MIN = -(2 ** 15)

LANES = 128
SUBLANES = 8
TILE = 256
NEAR_TILES = 4
BF16_ROWS = 16
A_VT_ROWS = A_V_DIM + BF16_ROWS
C_VT_ROWS = HEAD_DIM + BF16_ROWS
LOG2E = 1.4426950408889634
SKEW = 5
VMEM_CAP = 60000 * 1024

_C_AQ, _C_AK, _C_AV, _C_BQ, _C_BK, _C_BV, _C_CQ = 0, 512, 1024, 1536, 2048, 2176, 2304
_C_CKV, _C_IQ, _C_IKW = 2816, 2944, 3072
_IN_COLS = 3108
_W_COLS = 3200


def _dot(a, b):
    return jnp.dot(a, b, preferred_element_type=F32)


def _split_bf16(a):
    hi = a.astype(BF16)
    lo = (a - hi.astype(F32)).astype(BF16)
    return hi, lo


def _sigmoid(x):
    return 1.0 / (1.0 + jnp.exp(-x))


def _rms(x, g):
    return x * lax.rsqrt(jnp.mean(x * x, axis=-1, keepdims=True) + EPS) * g


def _colmax8(s):
    r, c = s.shape
    return jnp.max(s.reshape(r // SUBLANES, SUBLANES, c), axis=0)


def _colsum8(s):
    r, c = s.shape
    return jnp.sum(s.reshape(r // SUBLANES, SUBLANES, c), axis=0)


def _vmem_limit(nbytes):
    return int(min(VMEM_CAP, nbytes))


def _proj_kernel(x_ref, g_ref, w_ref, seg_ref, gaq_ref, gak_ref, gbq_ref, gbk_ref, gcq_ref, gck_ref,
                 aqT_ref, ak_ref, avT_ref, bqT_ref, bk_ref, bvT_ref, cqT_ref, ck_ref, cvT_ref,
                 iqT_ref, ik_ref, iwT_ref, *, tm, iw_scale):
    hb = _rms(x_ref[0], g_ref[...]).astype(BF16)
    seg = seg_ref[...]
    n_sub = tm // TILE

    def grp(a, n):
        return _dot(hb, w_ref[:, a:a + n])

    def segnorm(t, g):
        n = t.shape[1]
        ssq = _dot((t * t).astype(BF16), seg[:n, :n])
        return t * lax.rsqrt(ssq * (1.0 / HEAD_DIM) + EPS) * g

    def segnorm_t(t, gcol):
        n = t.shape[0] // HEAD_DIM
        t3 = t.reshape(n, HEAD_DIM, tm)
        ssq = jnp.sum(t3 * t3, axis=1, keepdims=True)
        return (t3 * lax.rsqrt(ssq * (1.0 / HEAD_DIM) + EPS)).reshape(t.shape) * gcol

    ones_rows = (lax.broadcasted_iota(I32, (BF16_ROWS, tm), 0) == 0).astype(F32)

    def put_slabs(ref, tT):
        for s in range(n_sub):
            ref[0, s] = tT[:, s * TILE:(s + 1) * TILE].astype(BF16)

    aqT_ref[0] = segnorm_t(grp(_C_AQ, BRANCH_WIDTH).T, gaq_ref[...]).astype(BF16)
    ak_ref[0] = segnorm(grp(_C_AK, BRANCH_WIDTH), gak_ref[...]).astype(BF16)
    avT = grp(_C_AV, BRANCH_WIDTH).T
    put_slabs(avT_ref, jnp.concatenate(
        [p for h in range(A_HEADS) for p in (avT[h * A_V_DIM:(h + 1) * A_V_DIM, :], ones_rows)], axis=0))
    bqT_ref[0] = segnorm_t(grp(_C_BQ, BRANCH_WIDTH).T, gbq_ref[...]).astype(BF16)
    cqT_ref[0] = segnorm_t(grp(_C_CQ, BRANCH_WIDTH).T, gcq_ref[...]).astype(BF16)
    bk_ref[0] = segnorm(grp(_C_BK, LANES), gbk_ref[...]).astype(BF16)
    bvT = grp(_C_BV, LANES).T
    put_slabs(bvT_ref, jnp.concatenate(
        [p for g in range(B_KV_HEADS) for p in (bvT[g * HEAD_DIM:(g + 1) * HEAD_DIM, :], ones_rows)], axis=0))
    iqT_ref[0] = grp(_C_IQ, LANES).T.astype(BF16)
    ckv = grp(_C_CKV, LANES)
    ck = ckv[:, :HEAD_DIM]
    ssq = jnp.sum(ck * ck, axis=-1, keepdims=True)
    ck_ref[0] = (ck * lax.rsqrt(ssq * (1.0 / HEAD_DIM) + EPS) * gck_ref[...]).astype(BF16)
    put_slabs(cvT_ref, jnp.concatenate([ckv.T[HEAD_DIM:, :], ones_rows], axis=0))
    ikw = grp(_C_IKW, LANES)
    ik_ref[0] = ikw[:, :IDX_DIM].astype(BF16)
    iwT_ref[0] = ikw.T[IDX_DIM:IDX_DIM + IDX_HEADS, :] * iw_scale


def _proj(x, g, w, seg, gains, tm):
    B, S, D = x.shape
    nt = S // TILE
    n_sub = tm // TILE
    full = lambda shape: pl.BlockSpec(shape, lambda b, t: (0,) * len(shape))
    out_shape = [
        jax.ShapeDtypeStruct((B, BRANCH_WIDTH, S), BF16),
        jax.ShapeDtypeStruct((B, S, BRANCH_WIDTH), BF16),
        jax.ShapeDtypeStruct((B, nt, A_HEADS * A_VT_ROWS, TILE), BF16),
        jax.ShapeDtypeStruct((B, BRANCH_WIDTH, S), BF16),
        jax.ShapeDtypeStruct((B, S, LANES), BF16),
        jax.ShapeDtypeStruct((B, nt, B_KV_HEADS * C_VT_ROWS, TILE), BF16),
        jax.ShapeDtypeStruct((B, BRANCH_WIDTH, S), BF16),
        jax.ShapeDtypeStruct((B, S, HEAD_DIM), BF16),
        jax.ShapeDtypeStruct((B, nt, C_VT_ROWS, TILE), BF16),
        jax.ShapeDtypeStruct((B, LANES, S), BF16),
        jax.ShapeDtypeStruct((B, S, IDX_DIM), BF16),
        jax.ShapeDtypeStruct((B, IDX_HEADS, S), F32),
    ]
    colT = lambda r: pl.BlockSpec((1, r, tm), lambda b, t: (b, 0, t))
    row = lambda c: pl.BlockSpec((1, tm, c), lambda b, t: (b, t, 0))
    slab = lambda r: pl.BlockSpec((1, n_sub, r, TILE), lambda b, t: (b, t, 0, 0))
    out_specs = [colT(BRANCH_WIDTH), row(BRANCH_WIDTH), slab(A_HEADS * A_VT_ROWS), colT(BRANCH_WIDTH),
                 row(LANES),
                 slab(B_KV_HEADS * C_VT_ROWS), colT(BRANCH_WIDTH), row(HEAD_DIM), slab(C_VT_ROWS), colT(LANES), row(IDX_DIM),
                 pl.BlockSpec((1, IDX_HEADS, tm), lambda b, t: (b, 0, t))]
    in_specs = [pl.BlockSpec((1, tm, D), lambda b, t: (b, t, 0)), full((1, D)), full((D, _W_COLS)),
                full((BRANCH_WIDTH, BRANCH_WIDTH))] + [full(gn.shape) for gn in gains]
    vmem = 2 * (tm * D * 4 + D * _W_COLS * 2 + BRANCH_WIDTH * BRANCH_WIDTH * 2 + tm * 3400 * 2) + 24 * tm * BRANCH_WIDTH * 4
    return pl.pallas_call(
        functools.partial(_proj_kernel, tm=tm, iw_scale=IDX_HEADS ** -0.5 * IDX_DIM ** -0.5),
        grid=(B, S // tm), in_specs=in_specs, out_specs=out_specs, out_shape=out_shape,
        compiler_params=pltpu.CompilerParams(dimension_semantics=("arbitrary", "arbitrary"),
                                             vmem_limit_bytes=_vmem_limit(vmem)),
        name="proj",
    )(x, g, w, seg, *gains)


def _online_step(s, m_old, vt, acc_ref, ch):
    if s.dtype == BF16:
        r, c = s.shape
        cm = jnp.max(s.reshape(r // BF16_ROWS, BF16_ROWS, c), axis=0).astype(F32)
        m_new = jnp.maximum(m_old, jnp.max(cm, axis=0, keepdims=True))
        e = jnp.exp2(s - m_new.astype(BF16))
    else:
        m_new = jnp.maximum(m_old, jnp.max(_colmax8(s), axis=0, keepdims=True))
        e = jnp.exp2(s - m_new).astype(BF16)
    alpha = jnp.exp2(m_old - m_new)
    acc_ref[ch] = acc_ref[ch] * alpha + _dot(vt, e)
    return m_new


def _sweep_key_tiles(i, n_chain, scores, vt, acc_ref):
    n_far = jnp.maximum(i - (NEAR_TILES - 1), 0)

    def step(tiles, ms):
        ms = list(ms)
        chains = [(j, d, ch) for (j, d) in tiles for ch in range(n_chain)]
        pending = [scores(*c) for c in chains[:SKEW]]
        for n, (j, d, ch) in enumerate(chains):
            s = pending.pop(0)
            if n + SKEW < len(chains):
                pending.append(scores(*chains[n + SKEW]))
            ms[ch] = _online_step(s, ms[ch], vt(j, ch), acc_ref, ch)
        return tuple(ms)

    def near_pairs(ms):
        ms = step([(i - 3, 3), (i - 2, 2)], ms)
        return step([(i - 1, 1), (i, 0)], ms)

    def near_singles(ms):
        return lax.fori_loop(0, i + 1, lambda t, ms: step([(t, i - t)], ms), ms)

    assert NEAR_TILES == 4
    ms = tuple(jnp.full((1, TILE), -jnp.inf, F32) for _ in range(n_chain))
    ms = lax.fori_loop(0, n_far // 4, lambda p, ms: step([(4 * p + u, None) for u in range(4)], ms), ms)
    rest = n_far - n_far % 4
    ms = lax.cond(n_far % 4 >= 2, lambda ms: step([(rest, None), (rest + 1, None)], ms), lambda ms: ms, ms)
    ms = lax.cond(n_far % 2 == 1, lambda ms: step([(n_far - 1, None)], ms), lambda ms: ms, ms)
    lax.cond(i >= NEAR_TILES - 1, near_pairs, near_singles, ms)


def _attn_a_kernel(qT_ref, k_ref, vT_ref, bias_ref, lam_ref, sub_ref, o_ref, q2_ref, acc_ref, *, lambda_init):
    i = pl.program_id(1)
    lp = lam_ref[...]
    lam = (jnp.exp(jnp.sum(lp[0:1] * lp[1:2], axis=-1, keepdims=True))
           - jnp.exp(jnp.sum(lp[2:3] * lp[3:4], axis=-1, keepdims=True)) + lambda_init)
    row = lax.broadcasted_iota(I32, (2 * HEAD_DIM, TILE), 0)
    n_chain = 2 * A_HEADS

    for h in range(A_HEADS):
        qh = qT_ref[0, h * A_V_DIM:(h + 1) * A_V_DIM, :]
        zero = jnp.zeros_like(qh)
        q2_ref[2 * h] = jnp.where(row < HEAD_DIM, qh, zero)
        q2_ref[2 * h + 1] = jnp.where(row >= HEAD_DIM, qh, zero)
    acc_ref[...] = jnp.zeros_like(acc_ref)

    def scores(j, d, ch):
        h = ch // 2
        rows = pl.ds(pl.multiple_of(j * TILE, TILE), TILE)
        s = _dot(k_ref[0, rows, h * A_V_DIM:(h + 1) * A_V_DIM], q2_ref[ch]).astype(BF16)
        return s if d is None else s + bias_ref[h, d]

    def vt(j, ch):
        h = ch // 2
        return vT_ref[0, j, h * A_VT_ROWS:(h + 1) * A_VT_ROWS, :]

    _sweep_key_tiles(i, n_chain, scores, vt, acc_ref)

    for h in range(A_HEADS):
        a0 = acc_ref[2 * h]
        a1 = acc_ref[2 * h + 1]
        r0 = 1.0 / a0[A_V_DIM:A_V_DIM + 1, :]
        r1 = 1.0 / a1[A_V_DIM:A_V_DIM + 1, :]
        outT = a0[:A_V_DIM, :] * r0 - lam * (a1[:A_V_DIM, :] * r1)
        out = _rms(outT.T, sub_ref[...]) * (1.0 - lambda_init)
        o_ref[0, :, h * A_V_DIM:(h + 1) * A_V_DIM] = out.astype(BF16)


def _attn_a(aqT, ak, avT, bias, lam_par, subln_g, lambda_init):
    B, _, S = aqT.shape
    nt = S // TILE
    vt_rows = A_HEADS * A_VT_ROWS
    vmem = 2 * (BRANCH_WIDTH * TILE * 2 + S * BRANCH_WIDTH * 2 + S * vt_rows * 2 + bias.size * 4 + TILE * BRANCH_WIDTH * 2) \
        + 8 * A_V_DIM * TILE * 2 + 8 * A_VT_ROWS * TILE * 4 + 48 * TILE * TILE * 4
    return pl.pallas_call(
        functools.partial(_attn_a_kernel, lambda_init=lambda_init),
        grid=(B, nt),
        in_specs=[pl.BlockSpec((1, BRANCH_WIDTH, TILE), lambda b, i: (b, 0, i)),
                  pl.BlockSpec((1, S, BRANCH_WIDTH), lambda b, i: (b, 0, 0)),
                  pl.BlockSpec((1, nt, vt_rows, TILE), lambda b, i: (b, 0, 0, 0)),
                  pl.BlockSpec(bias.shape, lambda b, i: (0, 0, 0, 0)),
                  pl.BlockSpec((4, HEAD_DIM), lambda b, i: (0, 0)),
                  pl.BlockSpec((1, A_V_DIM), lambda b, i: (0, 0))],
        out_specs=pl.BlockSpec((1, TILE, BRANCH_WIDTH), lambda b, i: (b, i, 0)),
        out_shape=jax.ShapeDtypeStruct((B, S, BRANCH_WIDTH), BF16),
        scratch_shapes=[pltpu.VMEM((2 * A_HEADS, A_V_DIM, TILE), BF16),
                        pltpu.VMEM((2 * A_HEADS, A_VT_ROWS, TILE), F32)],
        compiler_params=pltpu.CompilerParams(dimension_semantics=("arbitrary", "arbitrary"),
                                             vmem_limit_bytes=_vmem_limit(vmem)),
        name="attn_a",
    )(aqT, ak, avT, bias, lam_par, subln_g)


def _attn_b_kernel(sink_ref, qT_ref, k_ref, vT_ref, bias_ref, o_ref, oT_ref):
    i = pl.program_id(1)
    jp = jnp.maximum(i - 1, 0)
    p_idx = jnp.where(i > 0, 1, 2)
    cur = pl.ds(pl.multiple_of(i * TILE, TILE), TILE)
    prev = pl.ds(pl.multiple_of(jp * TILE, TILE), TILE)

    def scores(h):
        qh = qT_ref[0, h * HEAD_DIM:(h + 1) * HEAD_DIM, :]
        zero = jnp.zeros_like(qh)
        q2 = jnp.concatenate([qh, zero] if h < B_GROUP else [zero, qh], axis=0)
        return (_dot(k_ref[0, cur, :], q2) + bias_ref[h, 0], _dot(k_ref[0, prev, :], q2) + bias_ref[h, p_idx])

    def finish(h, s):
        sc, sp = s
        gs = slice((h // B_GROUP) * C_VT_ROWS, (h // B_GROUP + 1) * C_VT_ROWS)
        sink = sink_ref[h] * LOG2E
        m = jnp.max(jnp.maximum(_colmax8(sc), _colmax8(sp)), axis=0, keepdims=True)
        m = jnp.maximum(m, sink)
        ec = jnp.exp2(sc - m).astype(BF16)
        ep = jnp.exp2(sp - m).astype(BF16)
        outT = _dot(vT_ref[0, i, gs, :], ec) + _dot(vT_ref[0, jp, gs, :], ep)
        den = outT[HEAD_DIM:HEAD_DIM + 1, :] + jnp.exp2(sink - m)
        oT_ref[h * HEAD_DIM:(h + 1) * HEAD_DIM, :] = outT[:HEAD_DIM, :] * (1.0 / den)

    pending = [scores(h) for h in range(SKEW)]
    for h in range(B_HEADS):
        s = pending.pop(0)
        if h + SKEW < B_HEADS:
            pending.append(scores(h + SKEW))
        finish(h, s)
    o_ref[0] = oT_ref[...].T.astype(BF16)


def _attn_b(bqT, bk, bvT, bias, sinks):
    B, _, S = bqT.shape
    nt = S // TILE
    vmem = 2 * (BRANCH_WIDTH * TILE * 2 + 2 * S * LANES * 2 + S * LANES * 2 + bias.size * 4 + TILE * BRANCH_WIDTH * 2) \
        + 24 * TILE * TILE * 4
    return pl.pallas_call(
        _attn_b_kernel,
        grid=(B, nt),
        in_specs=[pl.BlockSpec(memory_space=pltpu.SMEM),
                  pl.BlockSpec((1, BRANCH_WIDTH, TILE), lambda b, i: (b, 0, i)),
                  pl.BlockSpec((1, S, LANES), lambda b, i: (b, 0, 0)),
                  pl.BlockSpec((1, nt, B_KV_HEADS * C_VT_ROWS, TILE), lambda b, i: (b, 0, 0, 0)),
                  pl.BlockSpec(bias.shape, lambda b, i: (0, 0, 0, 0))],
        out_specs=pl.BlockSpec((1, TILE, BRANCH_WIDTH), lambda b, i: (b, i, 0)),
        out_shape=jax.ShapeDtypeStruct((B, S, BRANCH_WIDTH), BF16),
        scratch_shapes=[pltpu.VMEM((BRANCH_WIDTH, TILE), F32)],
        compiler_params=pltpu.CompilerParams(dimension_semantics=("arbitrary", "arbitrary"),
                                             vmem_limit_bytes=_vmem_limit(vmem)),
        name="attn_b",
    )(sinks, bqT, bk, bvT, bias)


def _attn_c_kernel(qT_ref, k_ref, vT_ref, iqT_ref, ik_ref, iwT_ref, bias_ref, o_ref,
                   hi_ref, lo_ref, mb_ref, acc_ref, oT_ref, *, top_k):
    i = pl.program_id(1)
    n_t = i + 1
    krow = lax.broadcasted_iota(I32, (TILE, TILE), 0)
    qcol = lax.broadcasted_iota(I32, (TILE, TILE), 1)
    allowed = (krow // CHUNK) <= (qcol // CHUNK)

    def idx_keys(j, diag):
        ikt = ik_ref[0, pl.ds(pl.multiple_of(j * TILE, TILE), TILE), :]
        lgs = [_dot(ikt, iqT_ref[0, hh * IDX_DIM:(hh + 1) * IDX_DIM, :]) for hh in range(IDX_HEADS)]
        sc = jnp.zeros((TILE, TILE), F32)
        for hh in range(IDX_HEADS):
            sc = sc + jnp.maximum(lgs[hh], 0.0) * iwT_ref[0, hh:hh + 1, :]
        if diag:
            sc = jnp.where(allowed, sc, NEG)
        bits = lax.bitcast_convert_type(sc, I32)
        hi_b = (bits >> 16).astype(I16)
        lo_b = bits.astype(I16)
        sign = jnp.where(hi_b < 0, jnp.int16(-1), jnp.int16(0))
        hi_ref[j] = hi_b ^ (sign & 0x7FFF)
        lo_ref[j] = lo_b ^ sign ^ I16_MIN

    def fill_pair(p, carry):
        idx_keys(2 * p, False)
        idx_keys(2 * p + 1, False)
        return carry

    lax.fori_loop(0, i // 2, fill_pair, 0)

    @pl.when(i % 2 == 1)
    def _():
        idx_keys(i - 1, False)

    idx_keys(i, True)

    n_pair = (n_t + 1) // 2

    @pl.when(n_t % 2 == 1)
    def _():
        hi_ref[n_t] = jnp.full((TILE, TILE), I16_MIN, I16)
        lo_ref[n_t] = jnp.full((TILE, TILE), I16_MIN, I16)

    groups = TILE // BF16_ROWS

    def rows16(ref, j):
        return ref[j].reshape(groups, BF16_ROWS, TILE)

    def bcast16(v):
        return jnp.broadcast_to(v, (BF16_ROWS, TILE)).astype(I16)

    def count(pred):
        def body(p, cs):
            cs = list(cs)
            for u in (0, 1):
                j = 2 * p + u
                hi, lo = rows16(hi_ref, j), rows16(lo_ref, j)
                for r in range(groups):
                    cs[r % len(cs)] = cs[r % len(cs)] + pred(hi[r], lo[r], j, r).astype(I16)
            return tuple(cs)
        cs = lax.fori_loop(0, n_pair, body, (jnp.zeros((BF16_ROWS, TILE), I16),) * 4)
        c = (cs[0] + cs[1]) + (cs[2] + cs[3])
        return jnp.sum(c.astype(I32), axis=0, keepdims=True)

    def search(n_bits, accept):
        def step(b, t):
            cand = t + lax.shift_left(jnp.int32(1), n_bits - 1 - b)
            return jnp.where(accept(cand), cand, t)
        return lax.fori_loop(0, n_bits, step, jnp.full((1, TILE), I16_MIN, I32))

    def hi_accept(cand):
        c16 = bcast16(cand)
        return count(lambda hi, lo, j, r: hi >= c16) >= top_k

    t_hi = search(16, hi_accept)
    th16 = bcast16(t_hi)
    cnt_above = count(lambda hi, lo, j, r: hi > th16)
    r_lo = top_k - cnt_above

    def bucket_only(p, carry):
        for u in (0, 1):
            j = 2 * p + u
            hi, lo = rows16(hi_ref, j), rows16(lo_ref, j)
            for r in range(groups):
                lo_ref[j, r * BF16_ROWS:(r + 1) * BF16_ROWS, :] = jnp.where(hi[r] == th16, lo[r], I16_MIN)
        return carry

    lax.fori_loop(0, n_pair, bucket_only, 0)

    def lo_accept(cand):
        c16 = bcast16(cand)
        return count(lambda hi, lo, j, r: lo >= c16) >= r_lo

    t_lo = search(16, lo_accept)
    tl16 = bcast16(t_lo)
    cnt_gt = count(lambda hi, lo, j, r: lo > tl16)
    r_eq = (r_lo - cnt_gt).astype(F32)

    tri = (krow >= qcol).astype(BF16)
    one, zero, neg16 = jnp.ones((), BF16), jnp.zeros((), BF16), jnp.full((), NEG, BF16)

    def mask_tile(j, ties_before, diag):
        hi, lo = rows16(hi_ref, j), rows16(lo_ref, j)
        eqs = [(lo[r] == tl16) & (hi[r] == th16) for r in range(groups)]
        ties = _dot(tri, jnp.concatenate([jnp.where(e, one, zero) for e in eqs], axis=0)) + ties_before
        over = (ties - r_eq).astype(BF16)
        rows = []
        for r in range(groups):
            kept_tie = eqs[r] & (over[r * BF16_ROWS:(r + 1) * BF16_ROWS, :] <= zero)
            rows.append(jnp.where((hi[r] > th16) | (lo[r] > tl16) | kept_tie, zero, neg16))
        mb = jnp.concatenate(rows, axis=0)
        if diag:
            mb = jnp.where(allowed, mb.astype(F32), NEG).astype(BF16)
        mb_ref[j] = mb
        return ties[TILE - 1:TILE, :]

    ties_before = lax.fori_loop(0, i // 2, lambda p, c: mask_tile(2 * p + 1, mask_tile(2 * p, c, False), False),
                                jnp.zeros((1, TILE), F32))
    ties_before = lax.cond(i % 2 == 1, lambda c: mask_tile(i - 1, c, False), lambda c: c, ties_before)
    mask_tile(i, ties_before, True)

    acc_ref[...] = jnp.zeros_like(acc_ref)

    def scores(j, d, h):
        kt = k_ref[0, pl.ds(pl.multiple_of(j * TILE, TILE), TILE), :]
        s = _dot(kt, qT_ref[0, h * HEAD_DIM:(h + 1) * HEAD_DIM, :]).astype(BF16) + mb_ref[j]
        return s if d is None else s + bias_ref[h, d]

    _sweep_key_tiles(i, C_HEADS, scores, lambda j, h: vT_ref[0, j], acc_ref)

    for h in range(C_HEADS):
        a = acc_ref[h]
        oT_ref[h * HEAD_DIM:(h + 1) * HEAD_DIM, :] = a[:HEAD_DIM, :] * (1.0 / a[HEAD_DIM:HEAD_DIM + 1, :])
    o_ref[0] = oT_ref[...].T.astype(BF16)


def _attn_c(cqT, ck, cvT, iqT, ik, iwT, bias, top_k):
    B, _, S = cqT.shape
    nt = S // TILE
    vmem = 2 * (BRANCH_WIDTH * TILE * 2 + 2 * S * LANES * 2 + S * HEAD_DIM * 2 + LANES * TILE * 2 + 8 * TILE * 4
                + bias.size * 4 + TILE * BRANCH_WIDTH * 2) + 2 * S * TILE * 4 + 48 * TILE * TILE * 4
    return pl.pallas_call(
        functools.partial(_attn_c_kernel, top_k=top_k),
        grid=(B, nt),
        in_specs=[pl.BlockSpec((1, BRANCH_WIDTH, TILE), lambda b, i: (b, 0, i)),
                  pl.BlockSpec((1, S, HEAD_DIM), lambda b, i: (b, 0, 0)),
                  pl.BlockSpec((1, nt, C_VT_ROWS, TILE), lambda b, i: (b, 0, 0, 0)),
                  pl.BlockSpec((1, LANES, TILE), lambda b, i: (b, 0, i)),
                  pl.BlockSpec((1, S, IDX_DIM), lambda b, i: (b, 0, 0)),
                  pl.BlockSpec((1, IDX_HEADS, TILE), lambda b, i: (b, 0, i)),
                  pl.BlockSpec(bias.shape, lambda b, i: (0, 0, 0, 0))],
        out_specs=pl.BlockSpec((1, TILE, BRANCH_WIDTH), lambda b, i: (b, i, 0)),
        out_shape=jax.ShapeDtypeStruct((B, S, BRANCH_WIDTH), BF16),
        scratch_shapes=[pltpu.VMEM((nt + nt % 2, TILE, TILE), I16), pltpu.VMEM((nt + nt % 2, TILE, TILE), I16),
                        pltpu.VMEM((nt, TILE, TILE), BF16),
                        pltpu.VMEM((C_HEADS, C_VT_ROWS, TILE), F32), pltpu.VMEM((BRANCH_WIDTH, TILE), F32)],
        compiler_params=pltpu.CompilerParams(dimension_semantics=("arbitrary", "arbitrary"),
                                             vmem_limit_bytes=_vmem_limit(vmem)),
        name="attn_c",
    )(cqT, ck, cvT, iqT, ik, iwT, bias)


def _merge_kernel(x_ref, oa_ref, ob_ref, oc_ref, gmix_ref, wg_ref, bg_ref, wb_ref, wo_ref, gffn_ref,
                  wrh_ref, wrl_ref, br_ref, x1_ref, h2_ref, gate_ref):
    x = x_ref[...]
    hb = _rms(x, gmix_ref[...]).astype(BF16)
    z = None
    for n, o_ref in enumerate((oa_ref, ob_ref, oc_ref)):
        cs = slice(n * D_MODEL, (n + 1) * D_MODEL)
        gate = _sigmoid(_dot(hb, wg_ref[:, cs]) + bg_ref[:, cs])
        y = _dot(o_ref[...], wb_ref[n])
        z = gate * y if z is None else z + gate * y
    x1 = x + _dot(z.astype(BF16), wo_ref[...])
    x1_ref[...] = x1
    h2 = _rms(x1, gffn_ref[...])
    h2_ref[...] = h2.astype(BF16)

    hi, lo = _split_bf16(h2)
    lg = _dot(hi, wrh_ref[...]) + _dot(lo, wrh_ref[...]) + _dot(hi, wrl_ref[...]) + br_ref[...]
    col = lax.broadcasted_iota(I32, lg.shape, 1).astype(F32)
    big = float(4 * LANES)
    is_g = (col >= N_EXPERTS) & (col < N_EXPERTS + N_GROUPS)
    gl = jnp.where(is_g, lg, -jnp.inf)
    gmax = jnp.max(gl, axis=-1, keepdims=True)
    p_group = 1.0 / jnp.sum(jnp.exp(gl - gmax), axis=-1, keepdims=True)
    g_sel = jnp.min(jnp.where(gl == gmax, col, big), axis=-1, keepdims=True) - N_EXPERTS
    in_g = (col >= g_sel * EXPERTS_PER_GROUP) & (col < (g_sel + 1) * EXPERTS_PER_GROUP)
    el = jnp.where(in_g, lg, -jnp.inf)
    e1 = jnp.max(el, axis=-1, keepdims=True)
    i1 = jnp.min(jnp.where(el == e1, col, big), axis=-1, keepdims=True)
    el2 = jnp.where(col == i1, -jnp.inf, el)
    e2 = jnp.max(el2, axis=-1, keepdims=True)
    i2 = jnp.min(jnp.where(el2 == e2, col, big), axis=-1, keepdims=True)
    t2 = jnp.exp(e2 - e1)
    w1 = p_group / (1.0 + t2)
    w2 = w1 * t2
    gates = jnp.where(col == i1, w1, 0.0) + jnp.where(col == i2, w2, 0.0)
    gate_ref[...] = gates[:, :N_EXPERTS]


def _merge(x2, oa, ob, oc, gmix, wg, bg, wb, wo, gffn, wrh, wrl, br, tm):
    T, D = x2.shape
    full = lambda a: pl.BlockSpec(a.shape, lambda t: (0,) * a.ndim)
    rowb = lambda c: pl.BlockSpec((tm, c), lambda t: (t, 0))
    vmem = 2 * (tm * D * 4 * 2 + 3 * tm * BRANCH_WIDTH * 2 + tm * D * 2 + wg.size * 2 + wb.size * 2 + wo.size * 2
                + 2 * D * LANES * 2) + 10 * tm * D * 4
    return pl.pallas_call(
        _merge_kernel,
        grid=(T // tm,),
        in_specs=[rowb(D), rowb(BRANCH_WIDTH), rowb(BRANCH_WIDTH), rowb(BRANCH_WIDTH), full(gmix), full(wg), full(bg), full(wb),
                  full(wo), full(gffn), full(wrh), full(wrl), full(br)],
        out_specs=[rowb(D), rowb(D), rowb(N_EXPERTS)],
        out_shape=[jax.ShapeDtypeStruct((T, D), F32), jax.ShapeDtypeStruct((T, D), BF16),
                   jax.ShapeDtypeStruct((T, N_EXPERTS), F32)],
        compiler_params=pltpu.CompilerParams(dimension_semantics=("arbitrary",),
                                             vmem_limit_bytes=_vmem_limit(vmem)),
        name="merge",
    )(x2, oa, ob, oc, gmix, wg, bg, wb, wo, gffn, wrh, wrl, br)


def _moe_kernel(x1_ref, h2_ref, gate_ref, wgu_ref, wd_ref, o_ref):
    g = pl.program_id(1)
    h2 = h2_ref[...]
    gate = gate_ref[...]
    lane = lax.broadcasted_iota(I32, gate.shape, 1)
    hids = []
    for u in range(EXPERTS_PER_GROUP):
        gu = _dot(h2, wgu_ref[u])
        a = gu[:, :EXPERT_FF]
        w = jnp.sum(jnp.where(lane == g * EXPERTS_PER_GROUP + u, gate, 0.0), axis=-1, keepdims=True)
        hids.append((a * _sigmoid(a) * gu[:, EXPERT_FF:] * w).astype(BF16))
    y = _dot(jnp.concatenate(hids, axis=1), wd_ref[...])

    @pl.when(g == 0)
    def _():
        o_ref[...] = x1_ref[...] + y

    @pl.when(g > 0)
    def _():
        o_ref[...] += y


def _moe(x1, h2, gate, wgu, wd, tm):
    T, D = x1.shape
    n_e, ff = EXPERTS_PER_GROUP, EXPERT_FF
    vmem = 2 * (tm * D * 4 * 2 + tm * D * 2 + tm * LANES * 4 + n_e * D * 2 * ff * 2 + n_e * ff * D * 2) \
        + 3 * n_e * tm * 2 * ff * 4 + 2 * tm * D * 4
    return pl.pallas_call(
        _moe_kernel,
        grid=(T // tm, N_GROUPS),
        in_specs=[pl.BlockSpec((tm, D), lambda t, g: (t, 0)),
                  pl.BlockSpec((tm, D), lambda t, g: (t, 0)),
                  pl.BlockSpec((tm, N_EXPERTS), lambda t, g: (t, 0)),
                  pl.BlockSpec((n_e, D, 2 * ff), lambda t, g: (g, 0, 0)),
                  pl.BlockSpec((n_e * ff, D), lambda t, g: (g, 0))],
        out_specs=pl.BlockSpec((tm, D), lambda t, g: (t, 0)),
        out_shape=jax.ShapeDtypeStruct((T, D), F32),
        compiler_params=pltpu.CompilerParams(dimension_semantics=("arbitrary", "arbitrary"),
                                             vmem_limit_bytes=_vmem_limit(vmem)),
        name="moe",
    )(x1, h2, gate, wgu, wd)


def _t5_bucket_np(rel):
    half = NUM_BUCKETS // 2
    max_exact = half // 2
    n = np.abs(rel)
    n_f = np.maximum(n, 1).astype(np.float32)
    large = max_exact + (np.log(n_f / np.float32(max_exact)) / np.float32(math.log(MAX_DISTANCE / max_exact))
                         * np.float32(half - max_exact)).astype(np.int32)
    large = np.minimum(large, half - 1)
    return np.where(rel > 0, half, 0) + np.where(n < max_exact, n, large)


def _toeplitz_kernel(v_ref, o_ref):
    v = v_ref[0]
    for d in range(NEAR_TILES):
        x = jnp.broadcast_to(v[d:d + 1, :], (TILE, 2 * TILE))
        o_ref[0, d] = pltpu.roll(x, 0, 1, stride=1, stride_axis=0)[:, :TILE]


def _toeplitz(vals):
    n_heads = vals.shape[0]
    return pl.pallas_call(
        _toeplitz_kernel,
        grid=(n_heads,),
        in_specs=[pl.BlockSpec((1, NEAR_TILES, 2 * TILE), lambda h: (h, 0, 0))],
        out_specs=pl.BlockSpec((1, NEAR_TILES, TILE, TILE), lambda h: (h, 0, 0, 0)),
        out_shape=jax.ShapeDtypeStruct((n_heads, NEAR_TILES, TILE, TILE), F32),
        name="bias_tiles",
    )(vals)


def _bias_tables(rel_bias):
    u = np.arange(2 * TILE)
    off = np.where(u < TILE, u, u - 2 * TILE)
    rel = np.stack([-off - TILE * d for d in range(NEAR_TILES)])
    onehot = (_t5_bucket_np(rel)[..., None] == np.arange(NUM_BUCKETS)).astype(np.float32)
    vals = jnp.einsum("dub,bh->hdu", jnp.asarray(onehot), rel_bias, precision=lax.Precision.HIGHEST)
    tab = _toeplitz(vals)

    far = rel_bias[NUM_BUCKETS // 2 - 1]
    kk = np.arange(TILE)[:, None]
    qc = np.arange(TILE)[None, :] // CHUNK
    kc = np.stack([(kk // CHUNK) - (TILE // CHUNK) * d for d in range(NEAR_TILES)]) + 0 * qc
    a_ok = jnp.asarray(kc <= qc)
    ta = jnp.where(a_ok[None], (tab[:A_HEADS] - far[:A_HEADS, None, None, None]) * LOG2E, NEG)
    b_ok = jnp.asarray((qc - kc[:2] >= 0) & (qc - kc[:2] <= W_CHUNKS))
    tb = jnp.where(b_ok[None], tab[A_HEADS:A_HEADS + B_HEADS, :2] * LOG2E, NEG)
    tb = jnp.concatenate([tb, jnp.full((B_HEADS, 1, TILE, TILE), NEG, F32)], axis=1)
    tc = (tab[A_HEADS + B_HEADS:] - far[A_HEADS + B_HEADS:, None, None, None]) * LOG2E
    return ta.astype(BF16), tb.astype(F32), tc.astype(BF16)


def _proj_weight_kernel(w_ref, o_ref):
    o_ref[...] = jnp.zeros_like(o_ref)
    o_ref[:, :_IN_COLS] = w_ref[0].astype(BF16)


def _proj_weight(w_in, l):
    _, d, n = w_in.shape
    assert n == _IN_COLS
    rows = 128
    return pl.pallas_call(
        _proj_weight_kernel,
        grid=(d // rows,),
        in_specs=[pl.BlockSpec((1, rows, n), lambda r: (l, r, 0))],
        out_specs=pl.BlockSpec((rows, _W_COLS), lambda r: (r, 0)),
        out_shape=jax.ShapeDtypeStruct((d, _W_COLS), BF16),
        name="proj_weight",
    )(w_in)


def kernel(x, rel_bias, norm_mix_g, w_in, qk_norm_g, diff_lambda, diff_subln_g, sinks, w_branch, w_gate, b_gate,
           w_out, norm_ffn_g, w_router_group, b_router_group, w_router_expert, b_router_expert, w_ff_gate,
           w_ff_up, w_ff_down):
    B, S, D = x.shape
    assert D == D_MODEL and S % TILE == 0
    T = B * S
    top_k = min(TOPK_MAX, S // 4)
    tm_proj = 1024 if S % 1024 == 0 else TILE
    tm_merge = 512 if T % 512 == 0 else TILE
    tm_moe = 1024 if T % 1024 == 0 else TILE

    bias_a, bias_b, bias_c = _bias_tables(rel_bias)
    seg = jnp.asarray(np.kron(np.eye(BRANCH_WIDTH // HEAD_DIM), np.ones((HEAD_DIM, HEAD_DIM))), BF16)
    q_scale = HEAD_DIM ** -0.5

    for l in range(DEPTH):
        lambda_init = 0.8 - 0.6 * math.exp(-0.3 * l)
        qg = qk_norm_g[l]
        tile8 = lambda g: jnp.tile(g, BRANCH_WIDTH // HEAD_DIM)
        gains = ((tile8(qg[0, 0]) * (q_scale * LOG2E))[:, None], tile8(qg[0, 1])[None, :],
                 (tile8(qg[1, 0]) * (q_scale * LOG2E))[:, None], jnp.tile(qg[1, 1], B_KV_HEADS)[None, :],
                 (tile8(qg[2, 0]) * (q_scale * LOG2E))[:, None], qg[2, 1][None, :])
        (aqT, ak, avT, bqT, bk, bvT, cqT, ck, cvT, iqT, ik, iwT) = _proj(
            x, norm_mix_g[l][None, :], _proj_weight(w_in, l), seg, gains, tm_proj)

        oa = _attn_a(aqT, ak, avT, bias_a, diff_lambda[l], diff_subln_g[l][None, :], lambda_init)
        ob = _attn_b(bqT, bk, bvT, bias_b, sinks[l])
        oc = _attn_c(cqT, ck, cvT, iqT, ik, iwT, bias_c, top_k)

        w_r = jnp.concatenate([w_router_expert[l], w_router_group[l],
                               jnp.zeros((D, LANES - N_EXPERTS - N_GROUPS), F32)], axis=1)
        b_r = jnp.concatenate([b_router_expert[l], b_router_group[l],
                               jnp.zeros((LANES - N_EXPERTS - N_GROUPS,), F32)])[None, :]
        wrh = w_r.astype(BF16)
        wrl = (w_r - wrh.astype(F32)).astype(BF16)
        x1, h2, gate = _merge(
            x.reshape(T, D), oa.reshape(T, BRANCH_WIDTH), ob.reshape(T, BRANCH_WIDTH), oc.reshape(T, BRANCH_WIDTH),
            norm_mix_g[l][None, :], w_gate[l].astype(BF16), b_gate[l][None, :], w_branch[l].astype(BF16),
            w_out[l].astype(BF16), norm_ffn_g[l][None, :], wrh, wrl, b_r, tm_merge)

        wgu = jnp.concatenate([w_ff_gate[l], w_ff_up[l]], axis=-1).astype(BF16)
        wd = w_ff_down[l].astype(BF16).reshape(N_EXPERTS * EXPERT_FF, D)
        x = _moe(x1, h2, gate, wgu, wd, tm_moe).reshape(B, S, D)
    return x
```

```python
import functools
import math

import numpy as np
import jax
import jax.numpy as jnp
from jax import lax
from jax.experimental import pallas as pl
from jax.experimental.pallas import tpu as pltpu

F32 = jnp.float32
BF16 = jnp.bfloat16
I32 = jnp.int32
I16 = jnp.int16

D_MODEL = 1024
DEPTH = 2
CHUNK = 64
HEAD_DIM = 64
A_HEADS = 4
A_V_DIM = 2 * HEAD_DIM
B_HEADS = 8
B_KV_HEADS = 2
B_GROUP = B_HEADS // B_KV_HEADS
W_CHUNKS = 2
C_HEADS = 8
IDX_HEADS = 4
IDX_DIM = 32
TOPK_MAX = 256
NUM_BUCKETS = 32
MAX_DISTANCE = 1024
N_BRANCH = 3
BRANCH_WIDTH = 512
N_GROUPS = 4
EXPERTS_PER_GROUP = 4
N_EXPERTS = N_GROUPS * EXPERTS_PER_GROUP
EXPERT_FF = 256
EPS = 1e-6
NEG = -1e30
I16_MIN = -(2 ** 15)

LANES = 128
SUBLANES = 8
TILE = 256
NEAR_TILES = 4
BF16_ROWS = 16
A_VT_ROWS = A_V_DIM + BF16_ROWS
C_VT_ROWS = HEAD_DIM + BF16_ROWS
LOG2E = 1.4426950408889634
SKEW = 5
VMEM_CAP = 60000 * 1024

_C_AQ, _C_AK, _C_AV, _C_BQ, _C_BK, _C_BV, _C_CQ = 0, 512, 1024, 1536, 2048, 2176, 2304
_C_CKV, _C_IQ, _C_IKW = 2816, 2944, 3072
_IN_COLS = 3108
_W_COLS = 3200


def _dot(a, b):
    return jnp.dot(a, b, preferred_element_type=F32)


def _split_bf16(a):
    hi = a.astype(BF16)
    lo = (a - hi.astype(F32)).astype(BF16)
    return hi, lo


def _sigmoid(x):
    return 1.0 / (1.0 + jnp.exp(-x))


def _rms(x, g):
    return x * lax.rsqrt(jnp.mean(x * x, axis=-1, keepdims=True) + EPS) * g


def _colmax8(s):
    r, c = s.shape
    return jnp.max(s.reshape(r // SUBLANES, SUBLANES, c), axis=0)


def _colsum8(s):
    r, c = s.shape
    return jnp.sum(s.reshape(r // SUBLANES, SUBLANES, c), axis=0)


def _vmem_limit(nbytes):
    return int(min(VMEM_CAP, nbytes))


def _proj_kernel(x_ref, g_ref, w_ref, seg_ref, gaq_ref, gak_ref, gbq_ref, gbk_ref, gcq_ref, gck_ref,
                 aqT_ref, ak_ref, avT_ref, bqT_ref, bk_ref, bvT_ref, cqT_ref, ck_ref, cvT_ref,
                 iqT_ref, ik_ref, iwT_ref, *, tm, iw_scale):
    hb = _rms(x_ref[0], g_ref[...]).astype(BF16)
    seg = seg_ref[...]
    n_sub = tm // TILE

    def grp(a, n):
        return _dot(hb, w_ref[:, a:a + n])

    def segnorm(t, g):
        n = t.shape[1]
        ssq = _dot((t * t).astype(BF16), seg[:n, :n])
        return t * lax.rsqrt(ssq * (1.0 / HEAD_DIM) + EPS) * g

    def segnorm_t(t, gcol):
        n = t.shape[0] // HEAD_DIM
        t3 = t.reshape(n, HEAD_DIM, tm)
        ssq = jnp.sum(t3 * t3, axis=1, keepdims=True)
        return (t3 * lax.rsqrt(ssq * (1.0 / HEAD_DIM) + EPS)).reshape(t.shape) * gcol

    ones_rows = (lax.broadcasted_iota(I32, (BF16_ROWS, tm), 0) == 0).astype(F32)

    def put_slabs(ref, tT):
        for s in range(n_sub):
            ref[0, s] = tT[:, s * TILE:(s + 1) * TILE].astype(BF16)

    aqT_ref[0] = segnorm_t(grp(_C_AQ, BRANCH_WIDTH).T, gaq_ref[...]).astype(BF16)
    ak_ref[0] = segnorm(grp(_C_AK, BRANCH_WIDTH), gak_ref[...]).astype(BF16)
    avT = grp(_C_AV, BRANCH_WIDTH).T
    put_slabs(avT_ref, jnp.concatenate(
        [p for h in range(A_HEADS) for p in (avT[h * A_V_DIM:(h + 1) * A_V_DIM, :], ones_rows)], axis=0))
    bqT_ref[0] = segnorm_t(grp(_C_BQ, BRANCH_WIDTH).T, gbq_ref[...]).astype(BF16)
    cqT_ref[0] = segnorm_t(grp(_C_CQ, BRANCH_WIDTH).T, gcq_ref[...]).astype(BF16)
    bk_ref[0] = segnorm(grp(_C_BK, LANES), gbk_ref[...]).astype(BF16)
    bvT = grp(_C_BV, LANES).T
    put_slabs(bvT_ref, jnp.concatenate(
        [p for g in range(B_KV_HEADS) for p in (bvT[g * HEAD_DIM:(g + 1) * HEAD_DIM, :], ones_rows)], axis=0))
    iqT_ref[0] = grp(_C_IQ, LANES).T.astype(BF16)
    ckv = grp(_C_CKV, LANES)
    ck = ckv[:, :HEAD_DIM]
    ssq = jnp.sum(ck * ck, axis=-1, keepdims=True)
    ck_ref[0] = (ck * lax.rsqrt(ssq * (1.0 / HEAD_DIM) + EPS) * gck_ref[...]).astype(BF16)
    put_slabs(cvT_ref, jnp.concatenate([ckv.T[HEAD_DIM:, :], ones_rows], axis=0))
    ikw = grp(_C_IKW, LANES)
    ik_ref[0] = ikw[:, :IDX_DIM].astype(BF16)
    iwT_ref[0] = ikw.T[IDX_DIM:IDX_DIM + IDX_HEADS, :] * iw_scale


def _proj(x, g, w, seg, gains, tm):
    B, S, D = x.shape
    nt = S // TILE
    n_sub = tm // TILE
    full = lambda shape: pl.BlockSpec(shape, lambda b, t: (0,) * len(shape))
    out_shape = [
        jax.ShapeDtypeStruct((B, BRANCH_WIDTH, S), BF16),
        jax.ShapeDtypeStruct((B, S, BRANCH_WIDTH), BF16),
        jax.ShapeDtypeStruct((B, nt, A_HEADS * A_VT_ROWS, TILE), BF16),
        jax.ShapeDtypeStruct((B, BRANCH_WIDTH, S), BF16),
        jax.ShapeDtypeStruct((B, S, LANES), BF16),
        jax.ShapeDtypeStruct((B, nt, B_KV_HEADS * C_VT_ROWS, TILE), BF16),
        jax.ShapeDtypeStruct((B, BRANCH_WIDTH, S), BF16),
        jax.ShapeDtypeStruct((B, S, HEAD_DIM), BF16),
        jax.ShapeDtypeStruct((B, nt, C_VT_ROWS, TILE), BF16),
        jax.ShapeDtypeStruct((B, LANES, S), BF16),
        jax.ShapeDtypeStruct((B, S, IDX_DIM), BF16),
        jax.ShapeDtypeStruct((B, IDX_HEADS, S), F32),
    ]
    colT = lambda r: pl.BlockSpec((1, r, tm), lambda b, t: (b, 0, t))
    row = lambda c: pl.BlockSpec((1, tm, c), lambda b, t: (b, t, 0))
    slab = lambda r: pl.BlockSpec((1, n_sub, r, TILE), lambda b, t: (b, t, 0, 0))
    out_specs = [colT(BRANCH_WIDTH), row(BRANCH_WIDTH), slab(A_HEADS * A_VT_ROWS), colT(BRANCH_WIDTH),
                 row(LANES),
                 slab(B_KV_HEADS * C_VT_ROWS), colT(BRANCH_WIDTH), row(HEAD_DIM), slab(C_VT_ROWS), colT(LANES), row(IDX_DIM),
                 pl.BlockSpec((1, IDX_HEADS, tm), lambda b, t: (b, 0, t))]
    in_specs = [pl.BlockSpec((1, tm, D), lambda b, t: (b, t, 0)), full((1, D)), full((D, _W_COLS)),
                full((BRANCH_WIDTH, BRANCH_WIDTH))] + [full(gn.shape) for gn in gains]
    vmem = 2 * (tm * D * 4 + D * _W_COLS * 2 + BRANCH_WIDTH * BRANCH_WIDTH * 2 + tm * 3400 * 2) + 24 * tm * BRANCH_WIDTH * 4
    return pl.pallas_call(
        functools.partial(_proj_kernel, tm=tm, iw_scale=IDX_HEADS ** -0.5 * IDX_DIM ** -0.5),
        grid=(B, S // tm), in_specs=in_specs, out_specs=out_specs, out_shape=out_shape,
        compiler_params=pltpu.CompilerParams(dimension_semantics=("arbitrary", "arbitrary"),
                                             vmem_limit_bytes=_vmem_limit(vmem)),
        name="proj",
    )(x, g, w, seg, *gains)


def _online_step(s, m_old, vt, acc_ref, ch):
    if s.dtype == BF16:
        r, c = s.shape
        cm = jnp.max(s.reshape(r // BF16_ROWS, BF16_ROWS, c), axis=0).astype(F32)
        m_new = jnp.maximum(m_old, jnp.max(cm, axis=0, keepdims=True))
        e = jnp.exp2(s - m_new.astype(BF16))
    else:
        m_new = jnp.maximum(m_old, jnp.max(_colmax8(s), axis=0, keepdims=True))
        e = jnp.exp2(s - m_new).astype(BF16)
    alpha = jnp.exp2(m_old - m_new)
    acc_ref[ch] = acc_ref[ch] * alpha + _dot(vt, e)
    return m_new


def _sweep_key_tiles(i, n_chain, scores, vt, acc_ref):
    n_far = jnp.maximum(i - (NEAR_TILES - 1), 0)

    def step(tiles, ms):
        ms = list(ms)
        chains = [(j, d, ch) for (j, d) in tiles for ch in range(n_chain)]
        pending = [scores(*c) for c in chains[:SKEW]]
        for n, (j, d, ch) in enumerate(chains):
            s = pending.pop(0)
            if n + SKEW < len(chains):
                pending.append(scores(*chains[n + SKEW]))
            ms[ch] = _online_step(s, ms[ch], vt(j, ch), acc_ref, ch)
        return tuple(ms)

    def near_pairs(ms):
        ms = step([(i - 3, 3), (i - 2, 2)], ms)
        return step([(i - 1, 1), (i, 0)], ms)

    def near_singles(ms):
        return lax.fori_loop(0, i + 1, lambda t, ms: step([(t, i - t)], ms), ms)

    assert NEAR_TILES == 4
    ms = tuple(jnp.full((1, TILE), -jnp.inf, F32) for _ in range(n_chain))
    ms = lax.fori_loop(0, n_far // 4, lambda p, ms: step([(4 * p + u, None) for u in range(4)], ms), ms)
    rest = n_far - n_far % 4
    ms = lax.cond(n_far % 4 >= 2, lambda ms: step([(rest, None), (rest + 1, None)], ms), lambda ms: ms, ms)
    ms = lax.cond(n_far % 2 == 1, lambda ms: step([(n_far - 1, None)], ms), lambda ms: ms, ms)
    lax.cond(i >= NEAR_TILES - 1, near_pairs, near_singles, ms)


def _attn_a_kernel(qT_ref, k_ref, vT_ref, bias_ref, lam_ref, sub_ref, o_ref, q2_ref, acc_ref, *, lambda_init):
    i = pl.program_id(1)
    lp = lam_ref[...]
    lam = (jnp.exp(jnp.sum(lp[0:1] * lp[1:2], axis=-1, keepdims=True))
           - jnp.exp(jnp.sum(lp[2:3] * lp[3:4], axis=-1, keepdims=True)) + lambda_init)
    row = lax.broadcasted_iota(I32, (2 * HEAD_DIM, TILE), 0)
    n_chain = 2 * A_HEADS

    for h in range(A_HEADS):
        qh = qT_ref[0, h * A_V_DIM:(h + 1) * A_V_DIM, :]
        zero = jnp.zeros_like(qh)
        q2_ref[2 * h] = jnp.where(row < HEAD_DIM, qh, zero)
        q2_ref[2 * h + 1] = jnp.where(row >= HEAD_DIM, qh, zero)
    acc_ref[...] = jnp.zeros_like(acc_ref)

    def scores(j, d, ch):
        h = ch // 2
        rows = pl.ds(pl.multiple_of(j * TILE, TILE), TILE)
        s = _dot(k_ref[0, rows, h * A_V_DIM:(h + 1) * A_V_DIM], q2_ref[ch]).astype(BF16)
        return s if d is None else s + bias_ref[h, d]

    def vt(j, ch):
        h = ch // 2
        return vT_ref[0, j, h * A_VT_ROWS:(h + 1) * A_VT_ROWS, :]

    _sweep_key_tiles(i, n_chain, scores, vt, acc_ref)

    for h in range(A_HEADS):
        a0 = acc_ref[2 * h]
        a1 = acc_ref[2 * h + 1]
        r0 = 1.0 / a0[A_V_DIM:A_V_DIM + 1, :]
        r1 = 1.0 / a1[A_V_DIM:A_V_DIM + 1, :]
        outT = a0[:A_V_DIM, :] * r0 - lam * (a1[:A_V_DIM, :] * r1)
        out = _rms(outT.T, sub_ref[...]) * (1.0 - lambda_init)
        o_ref[0, :, h * A_V_DIM:(h + 1) * A_V_DIM] = out.astype(BF16)


def _attn_b_kernel(sink_ref, qT_ref, k_ref, vT_ref, bias_ref, o_ref, oT_ref):
    i = pl.program_id(1)
    jp = jnp.maximum(i - 1, 0)
    p_idx = jnp.where(i > 0, 1, 2)
    cur = pl.ds(pl.multiple_of(i * TILE, TILE), TILE)
    prev = pl.ds(pl.multiple_of(jp * TILE, TILE), TILE)

    def scores(h):
        qh = qT_ref[0, h * HEAD_DIM:(h + 1) * HEAD_DIM, :]
        zero = jnp.zeros_like(qh)
        q2 = jnp.concatenate([qh, zero] if h < B_GROUP else [zero, qh], axis=0)
        return (_dot(k_ref[0, cur, :], q2) + bias_ref[h, 0].astype(F32),
                _dot(k_ref[0, prev, :], q2) + bias_ref[h, p_idx].astype(F32))

    def finish(h, s):
        sc, sp = s
        gs = slice((h // B_GROUP) * C_VT_ROWS, (h // B_GROUP + 1) * C_VT_ROWS)
        sink = sink_ref[h] * LOG2E
        m = jnp.max(jnp.maximum(_colmax8(sc), _colmax8(sp)), axis=0, keepdims=True)
        m = jnp.maximum(m, sink)
        ec = jnp.exp2(sc - m).astype(BF16)
        ep = jnp.exp2(sp - m).astype(BF16)
        outT = _dot(vT_ref[0, i, gs, :], ec) + _dot(vT_ref[0, jp, gs, :], ep)
        den = outT[HEAD_DIM:HEAD_DIM + 1, :] + jnp.exp2(sink - m)
        oT_ref[h * HEAD_DIM:(h + 1) * HEAD_DIM, :] = outT[:HEAD_DIM, :] * (1.0 / den)

    pending = [scores(h) for h in range(SKEW)]
    for h in range(B_HEADS):
        s = pending.pop(0)
        if h + SKEW < B_HEADS:
            pending.append(scores(h + SKEW))
        finish(h, s)
    o_ref[0] = oT_ref[...].T.astype(BF16)


def _attn_c_kernel(qT_ref, k_ref, vT_ref, iqT_ref, ik_ref, iwT_ref, bias_ref, o_ref,
                   hi_ref, lo_ref, mb_ref, acc_ref, oT_ref, *, top_k):
    i = pl.program_id(1)
    n_t = i + 1
    krow = lax.broadcasted_iota(I32, (TILE, TILE), 0)
    qcol = lax.broadcasted_iota(I32, (TILE, TILE), 1)
    allowed = (krow // CHUNK) <= (qcol // CHUNK)

    def idx_keys(j, diag):
        ikt = ik_ref[0, pl.ds(pl.multiple_of(j * TILE, TILE), TILE), :]
        lgs = [_dot(ikt, iqT_ref[0, hh * IDX_DIM:(hh + 1) * IDX_DIM, :]) for hh in range(IDX_HEADS)]
        sc = jnp.zeros((TILE, TILE), F32)
        for hh in range(IDX_HEADS):
            sc = sc + jnp.maximum(lgs[hh], 0.0) * iwT_ref[0, hh:hh + 1, :]
        if diag:
            sc = jnp.where(allowed, sc, NEG)
        bits = lax.bitcast_convert_type(sc, I32)
        hi_b = (bits >> 16).astype(I16)
        lo_b = bits.astype(I16)
        sign = jnp.where(hi_b < 0, jnp.int16(-1), jnp.int16(0))
        hi_ref[j] = hi_b ^ (sign & 0x7FFF)
        lo_ref[j] = lo_b ^ sign ^ I16_MIN

    def fill_pair(p, carry):
        idx_keys(2 * p, False)
        idx_keys(2 * p + 1, False)
        return carry

    lax.fori_loop(0, i // 2, fill_pair, 0)

    @pl.when(i % 2 == 1)
    def _():
        idx_keys(i - 1, False)

    idx_keys(i, True)

    n_pair = (n_t + 1) // 2

    @pl.when(n_t % 2 == 1)
    def _():
        hi_ref[n_t] = jnp.full((TILE, TILE), I16_MIN, I16)
        lo_ref[n_t] = jnp.full((TILE, TILE), I16_MIN, I16)

    groups = TILE // BF16_ROWS

    def rows16(ref, j):
        return ref[j].reshape(groups, BF16_ROWS, TILE)

    def bcast16(v):
        return jnp.broadcast_to(v, (BF16_ROWS, TILE)).astype(I16)

    def count(pred):
        def body(p, cs):
            cs = list(cs)
            for u in (0, 1):
                j = 2 * p + u
                hi, lo = rows16(hi_ref, j), rows16(lo_ref, j)
                for r in range(groups):
                    cs[r % len(cs)] = cs[r % len(cs)] + pred(hi[r], lo[r], j, r).astype(I16)
            return tuple(cs)
        cs = lax.fori_loop(0, n_pair, body, (jnp.zeros((BF16_ROWS, TILE), I16),) * 4)
        c = (cs[0] + cs[1]) + (cs[2] + cs[3])
        return jnp.sum(c.astype(I32), axis=0, keepdims=True)

    def search(n_bits, accept):
        def step(b, t):
            cand = t + lax.shift_left(jnp.int32(1), n_bits - 1 - b)
            return jnp.where(accept(cand), cand, t)
        return lax.fori_loop(0, n_bits, step, jnp.full((1, TILE), I16_MIN, I32))

    def hi_accept(cand):
        c16 = bcast16(cand)
        return count(lambda hi, lo, j, r: hi >= c16) >= top_k

    t_hi = search(16, hi_accept)
    th16 = bcast16(t_hi)
    cnt_above = count(lambda hi, lo, j, r: hi > th16)
    r_lo = top_k - cnt_above

    def bucket_only(p, carry):
        for u in (0, 1):
            j = 2 * p + u
            hi, lo = rows16(hi_ref, j), rows16(lo_ref, j)
            for r in range(groups):
                lo_ref[j, r * BF16_ROWS:(r + 1) * BF16_ROWS, :] = jnp.where(hi[r] == th16, lo[r], I16_MIN)
        return carry

    lax.fori_loop(0, n_pair, bucket_only, 0)

    def lo_accept(cand):
        c16 = bcast16(cand)
        return count(lambda hi, lo, j, r: lo >= c16) >= r_lo

    t_lo = search(16, lo_accept)
    tl16 = bcast16(t_lo)
    cnt_gt = count(lambda hi, lo, j, r: lo > tl16)
    r_eq = (r_lo - cnt_gt).astype(F32)

    tri = (krow >= qcol).astype(BF16)
    one, zero, neg16 = jnp.ones((), BF16), jnp.zeros((), BF16), jnp.full((), NEG, BF16)

    def mask_tile(j, ties_before, diag):
        hi, lo = rows16(hi_ref, j), rows16(lo_ref, j)
        eqs = [(lo[r] == tl16) & (hi[r] == th16) for r in range(groups)]
        ties = _dot(tri, jnp.concatenate([jnp.where(e, one, zero) for e in eqs], axis=0)) + ties_before
        over = (ties - r_eq).astype(BF16)
        rows = []
        for r in range(groups):
            kept_tie = eqs[r] & (over[r * BF16_ROWS:(r + 1) * BF16_ROWS, :] <= zero)
            rows.append(jnp.where((hi[r] > th16) | (lo[r] > tl16) | kept_tie, zero, neg16))
        mb = jnp.concatenate(rows, axis=0)
        if diag:
            mb = jnp.where(allowed, mb.astype(F32), NEG).astype(BF16)
        mb_ref[j] = mb
        return ties[TILE - 1:TILE, :]

    ties_before = lax.fori_loop(0, i // 2, lambda p, c: mask_tile(2 * p + 1, mask_tile(2 * p, c, False), False),
                                jnp.zeros((1, TILE), F32))
    ties_before = lax.cond(i % 2 == 1, lambda c: mask_tile(i - 1, c, False), lambda c: c, ties_before)
    mask_tile(i, ties_before, True)

    acc_ref[...] = jnp.zeros_like(acc_ref)

    def scores(j, d, h):
        kt = k_ref[0, pl.ds(pl.multiple_of(j * TILE, TILE), TILE), :]
        s = _dot(kt, qT_ref[0, h * HEAD_DIM:(h + 1) * HEAD_DIM, :]).astype(BF16) + mb_ref[j]
        return s if d is None else s + bias_ref[h, d]

    _sweep_key_tiles(i, C_HEADS, scores, lambda j, h: vT_ref[0, j], acc_ref)

    for h in range(C_HEADS):
        a = acc_ref[h]
        oT_ref[h * HEAD_DIM:(h + 1) * HEAD_DIM, :] = a[:HEAD_DIM, :] * (1.0 / a[HEAD_DIM:HEAD_DIM + 1, :])
    o_ref[0] = oT_ref[...].T.astype(BF16)


def _attn_kernel(*refs, lambda_init, top_k):
    a_in, b_in, c_in = refs[0:6], refs[6:11], refs[11:18]
    oa_ref, ob_ref, oc_ref = refs[18:21]
    a_scratch, b_scratch, c_scratch = refs[21:23], refs[23:24], refs[24:29]
    _attn_a_kernel(*a_in, oa_ref, *a_scratch, lambda_init=lambda_init)
    _attn_b_kernel(*b_in, ob_ref, *b_scratch)
    _attn_c_kernel(*c_in, oc_ref, *c_scratch, top_k=top_k)


def _attn(aqT, ak, avT, bias_a, lam_par, subln_g, sinks, bqT, bk, bvT, bias_b, cqT, ck, cvT, iqT, ik, iwT, bias_c,
          lambda_init, top_k):
    B, _, S = aqT.shape
    nt = S // TILE
    const = lambda a: pl.BlockSpec(a.shape, lambda b, i: (0,) * a.ndim, pipeline_mode=pl.Buffered(1))
    qtile = lambda rows: pl.BlockSpec((1, rows, TILE), lambda b, i: (b, 0, i))
    per_batch = lambda a: pl.BlockSpec((1,) + a.shape[1:], lambda b, i: (b,) + (0,) * (a.ndim - 1))
    in_specs = [qtile(BRANCH_WIDTH), per_batch(ak), per_batch(avT), const(bias_a), const(lam_par), const(subln_g),
                pl.BlockSpec(memory_space=pltpu.SMEM), qtile(BRANCH_WIDTH), per_batch(bk), per_batch(bvT),
                const(bias_b),
                qtile(BRANCH_WIDTH), per_batch(ck), per_batch(cvT), qtile(LANES), per_batch(ik), qtile(IDX_HEADS),
                const(bias_c)]
    out_spec = pl.BlockSpec((1, TILE, BRANCH_WIDTH), lambda b, i: (b, i, 0))
    out_shape = jax.ShapeDtypeStruct((B, S, BRANCH_WIDTH), BF16)
    scratch = [pltpu.VMEM((2 * A_HEADS, A_V_DIM, TILE), BF16),
               pltpu.VMEM((2 * A_HEADS, A_VT_ROWS, TILE), F32),
               pltpu.VMEM((BRANCH_WIDTH, TILE), F32),
               pltpu.VMEM((nt + nt % 2, TILE, TILE), I16),
               pltpu.VMEM((nt + nt % 2, TILE, TILE), I16),
               pltpu.VMEM((nt, TILE, TILE), BF16),
               pltpu.VMEM((C_HEADS, C_VT_ROWS, TILE), F32),
               pltpu.VMEM((BRANCH_WIDTH, TILE), F32)]
    per_batch_bytes = sum(a.size // B * a.dtype.itemsize for a in (ak, avT, bk, bvT, ck, cvT, ik))
    const_bytes = sum(a.size * a.dtype.itemsize for a in (bias_a, bias_b, bias_c))
    scratch_bytes = (2 * A_HEADS * (A_V_DIM * 2 + A_VT_ROWS * 4) + 2 * BRANCH_WIDTH * 4 + C_HEADS * C_VT_ROWS * 4) * TILE \
        + (2 * (nt + nt % 2) * 2 + nt * 2) * TILE * TILE
    vmem = 2 * per_batch_bytes + const_bytes + scratch_bytes + 12 * TILE * BRANCH_WIDTH * 2 + 64 * TILE * TILE * 4
    return pl.pallas_call(
        functools.partial(_attn_kernel, lambda_init=lambda_init, top_k=top_k),
        grid=(B, nt), in_specs=in_specs, out_specs=[out_spec] * 3, out_shape=[out_shape] * 3,
        scratch_shapes=scratch,
        compiler_params=pltpu.CompilerParams(dimension_semantics=("arbitrary", "arbitrary"),
                                             vmem_limit_bytes=_vmem_limit(vmem)),
        name="attn",
    )(aqT, ak, avT, bias_a, lam_par, subln_g, sinks, bqT, bk, bvT, bias_b, cqT, ck, cvT, iqT, ik, iwT, bias_c)


def _merge_kernel(x_ref, oa_ref, ob_ref, oc_ref, gmix_ref, wg_ref, bg_ref, wb_ref, wo_ref, gffn_ref,
                  wrh_ref, wrl_ref, br_ref, x1_ref, h2_ref, gate_ref):
    x = x_ref[...]
    hb = _rms(x, gmix_ref[...]).astype(BF16)
    z = None
    for n, o_ref in enumerate((oa_ref, ob_ref, oc_ref)):
        cs = slice(n * D_MODEL, (n + 1) * D_MODEL)
        gate = _sigmoid(_dot(hb, wg_ref[:, cs]) + bg_ref[:, cs])
        y = _dot(o_ref[...], wb_ref[n])
        z = gate * y if z is None else z + gate * y
    x1 = x + _dot(z.astype(BF16), wo_ref[...])
    x1_ref[...] = x1
    h2 = _rms(x1, gffn_ref[...])
    h2_ref[...] = h2.astype(BF16)

    hi, lo = _split_bf16(h2)
    lg = _dot(hi, wrh_ref[...]) + _dot(lo, wrh_ref[...]) + _dot(hi, wrl_ref[...]) + br_ref[...]
    col = lax.broadcasted_iota(I32, lg.shape, 1).astype(F32)
    big = float(4 * LANES)
    is_g = (col >= N_EXPERTS) & (col < N_EXPERTS + N_GROUPS)
    gl = jnp.where(is_g, lg, -jnp.inf)
    gmax = jnp.max(gl, axis=-1, keepdims=True)
    p_group = 1.0 / jnp.sum(jnp.exp(gl - gmax), axis=-1, keepdims=True)
    g_sel = jnp.min(jnp.where(gl == gmax, col, big), axis=-1, keepdims=True) - N_EXPERTS
    in_g = (col >= g_sel * EXPERTS_PER_GROUP) & (col < (g_sel + 1) * EXPERTS_PER_GROUP)
    el = jnp.where(in_g, lg, -jnp.inf)
    e1 = jnp.max(el, axis=-1, keepdims=True)
    i1 = jnp.min(jnp.where(el == e1, col, big), axis=-1, keepdims=True)
    el2 = jnp.where(col == i1, -jnp.inf, el)
    e2 = jnp.max(el2, axis=-1, keepdims=True)
    i2 = jnp.min(jnp.where(el2 == e2, col, big), axis=-1, keepdims=True)
    t2 = jnp.exp(e2 - e1)
    w1 = p_group / (1.0 + t2)
    w2 = w1 * t2
    gates = jnp.where(col == i1, w1, 0.0) + jnp.where(col == i2, w2, 0.0)
    gate_ref[...] = gates[:, :N_EXPERTS]


def _merge(x2, oa, ob, oc, gmix, wg, bg, wb, wo, gffn, wrh, wrl, br, tm):
    T, D = x2.shape
    full = lambda a: pl.BlockSpec(a.shape, lambda t: (0,) * a.ndim)
    rowb = lambda c: pl.BlockSpec((tm, c), lambda t: (t, 0))
    vmem = 2 * (tm * D * 4 * 2 + 3 * tm * BRANCH_WIDTH * 2 + tm * D * 2 + wg.size * 2 + wb.size * 2 + wo.size * 2
                + 2 * D * LANES * 2) + 10 * tm * D * 4
    return pl.pallas_call(
        _merge_kernel,
        grid=(T // tm,),
        in_specs=[rowb(D), rowb(BRANCH_WIDTH), rowb(BRANCH_WIDTH), rowb(BRANCH_WIDTH), full(gmix), full(wg), full(bg), full(wb),
                  full(wo), full(gffn), full(wrh), full(wrl), full(br)],
        out_specs=[rowb(D), rowb(D), rowb(N_EXPERTS)],
        out_shape=[jax.ShapeDtypeStruct((T, D), F32), jax.ShapeDtypeStruct((T, D), BF16),
                   jax.ShapeDtypeStruct((T, N_EXPERTS), F32)],
        compiler_params=pltpu.CompilerParams(dimension_semantics=("arbitrary",),
                                             vmem_limit_bytes=_vmem_limit(vmem)),
        name="merge",
    )(x2, oa, ob, oc, gmix, wg, bg, wb, wo, gffn, wrh, wrl, br)


def _moe_kernel(x1_ref, h2_ref, gate_ref, wgu_ref, wd_ref, o_ref):
    g = pl.program_id(1)
    h2 = h2_ref[...]
    gate = gate_ref[...]
    lane = lax.broadcasted_iota(I32, gate.shape, 1)
    hids = []
    for u in range(EXPERTS_PER_GROUP):
        gu = _dot(h2, wgu_ref[u])
        a = gu[:, :EXPERT_FF]
        w = jnp.sum(jnp.where(lane == g * EXPERTS_PER_GROUP + u, gate, 0.0), axis=-1, keepdims=True)
        hids.append((a * _sigmoid(a) * gu[:, EXPERT_FF:] * w).astype(BF16))
    y = _dot(jnp.concatenate(hids, axis=1), wd_ref[...])

    @pl.when(g == 0)
    def _():
        o_ref[...] = x1_ref[...] + y

    @pl.when(g > 0)
    def _():
        o_ref[...] += y


def _moe(x1, h2, gate, wgu, wd, tm):
    T, D = x1.shape
    n_e, ff = EXPERTS_PER_GROUP, EXPERT_FF
    vmem = 2 * (tm * D * 4 * 2 + tm * D * 2 + tm * LANES * 4 + n_e * D * 2 * ff * 2 + n_e * ff * D * 2) \
        + 3 * n_e * tm * 2 * ff * 4 + 2 * tm * D * 4
    return pl.pallas_call(
        _moe_kernel,
        grid=(T // tm, N_GROUPS),
        in_specs=[pl.BlockSpec((tm, D), lambda t, g: (t, 0)),
                  pl.BlockSpec((tm, D), lambda t, g: (t, 0)),
                  pl.BlockSpec((tm, N_EXPERTS), lambda t, g: (t, 0)),
                  pl.BlockSpec((n_e, D, 2 * ff), lambda t, g: (g, 0, 0)),
                  pl.BlockSpec((n_e * ff, D), lambda t, g: (g, 0))],
        out_specs=pl.BlockSpec((tm, D), lambda t, g: (t, 0)),
        out_shape=jax.ShapeDtypeStruct((T, D), F32),
        compiler_params=pltpu.CompilerParams(dimension_semantics=("arbitrary", "arbitrary"),
                                             vmem_limit_bytes=_vmem_limit(vmem)),
        name="moe",
    )(x1, h2, gate, wgu, wd)


def _t5_bucket_np(rel):
    half = NUM_BUCKETS // 2
    max_exact = half // 2
    n = np.abs(rel)
    n_f = np.maximum(n, 1).astype(np.float32)
    large = max_exact + (np.log(n_f / np.float32(max_exact)) / np.float32(math.log(MAX_DISTANCE / max_exact))
                         * np.float32(half - max_exact)).astype(np.int32)
    large = np.minimum(large, half - 1)
    return np.where(rel > 0, half, 0) + np.where(n < max_exact, n, large)


def _toeplitz_kernel(v_ref, o_ref):
    v = v_ref[0]
    for d in range(NEAR_TILES):
        x = jnp.broadcast_to(v[d:d + 1, :], (TILE, 2 * TILE))
        o_ref[0, d] = pltpu.roll(x, 0, 1, stride=1, stride_axis=0)[:, :TILE]


def _toeplitz(vals):
    n_heads = vals.shape[0]
    return pl.pallas_call(
        _toeplitz_kernel,
        grid=(n_heads,),
        in_specs=[pl.BlockSpec((1, NEAR_TILES, 2 * TILE), lambda h: (h, 0, 0))],
        out_specs=pl.BlockSpec((1, NEAR_TILES, TILE, TILE), lambda h: (h, 0, 0, 0)),
        out_shape=jax.ShapeDtypeStruct((n_heads, NEAR_TILES, TILE, TILE), F32),
        name="bias_tiles",
    )(vals)


def _bias_tables(rel_bias):
    u = np.arange(2 * TILE)
    off = np.where(u < TILE, u, u - 2 * TILE)
    rel = np.stack([-off - TILE * d for d in range(NEAR_TILES)])
    onehot = (_t5_bucket_np(rel)[..., None] == np.arange(NUM_BUCKETS)).astype(np.float32)
    vals = jnp.einsum("dub,bh->hdu", jnp.asarray(onehot), rel_bias, precision=lax.Precision.HIGHEST)
    tab = _toeplitz(vals)

    far = rel_bias[NUM_BUCKETS // 2 - 1]
    kk = np.arange(TILE)[:, None]
    qc = np.arange(TILE)[None, :] // CHUNK
    kc = np.stack([(kk // CHUNK) - (TILE // CHUNK) * d for d in range(NEAR_TILES)]) + 0 * qc
    a_ok = jnp.asarray(kc <= qc)
    ta = jnp.where(a_ok[None], (tab[:A_HEADS] - far[:A_HEADS, None, None, None]) * LOG2E, NEG)
    b_ok = jnp.asarray((qc - kc[:2] >= 0) & (qc - kc[:2] <= W_CHUNKS))
    tb = jnp.where(b_ok[None], tab[A_HEADS:A_HEADS + B_HEADS, :2] * LOG2E, NEG)
    tb = jnp.concatenate([tb, jnp.full((B_HEADS, 1, TILE, TILE), NEG, F32)], axis=1)
    tc = (tab[A_HEADS + B_HEADS:] - far[A_HEADS + B_HEADS:, None, None, None]) * LOG2E
    return ta.astype(BF16), tb.astype(BF16), tc.astype(BF16)


def _proj_weight_kernel(w_ref, o_ref):
    o_ref[...] = jnp.zeros_like(o_ref)
    o_ref[:, :_IN_COLS] = w_ref[0].astype(BF16)


def _proj_weight(w_in, l):
    _, d, n = w_in.shape
    assert n == _IN_COLS
    rows = 128
    return pl.pallas_call(
        _proj_weight_kernel,
        grid=(d // rows,),
        in_specs=[pl.BlockSpec((1, rows, n), lambda r: (l, r, 0))],
        out_specs=pl.BlockSpec((rows, _W_COLS), lambda r: (r, 0)),
        out_shape=jax.ShapeDtypeStruct((d, _W_COLS), BF16),
        name="proj_weight",
    )(w_in)


def kernel(x, rel_bias, norm_mix_g, w_in, qk_norm_g, diff_lambda, diff_subln_g, sinks, w_branch, w_gate, b_gate,
           w_out, norm_ffn_g, w_router_group, b_router_group, w_router_expert, b_router_expert, w_ff_gate,
           w_ff_up, w_ff_down):
    B, S, D = x.shape
    assert D == D_MODEL and S % TILE == 0
    T = B * S
    top_k = min(TOPK_MAX, S // 4)
    tm_proj = 1024 if S % 1024 == 0 else TILE
    tm_merge = 512 if T % 512 == 0 else TILE
    tm_moe = 1024 if T % 1024 == 0 else TILE

    bias_a, bias_b, bias_c = _bias_tables(rel_bias)
    seg = jnp.asarray(np.kron(np.eye(BRANCH_WIDTH // HEAD_DIM), np.ones((HEAD_DIM, HEAD_DIM))), BF16)
    q_scale = HEAD_DIM ** -0.5

    for l in range(DEPTH):
        lambda_init = 0.8 - 0.6 * math.exp(-0.3 * l)
        qg = qk_norm_g[l]
        tile8 = lambda g: jnp.tile(g, BRANCH_WIDTH // HEAD_DIM)
        gains = ((tile8(qg[0, 0]) * (q_scale * LOG2E))[:, None], tile8(qg[0, 1])[None, :],
                 (tile8(qg[1, 0]) * (q_scale * LOG2E))[:, None], jnp.tile(qg[1, 1], B_KV_HEADS)[None, :],
                 (tile8(qg[2, 0]) * (q_scale * LOG2E))[:, None], qg[2, 1][None, :])
        (aqT, ak, avT, bqT, bk, bvT, cqT, ck, cvT, iqT, ik, iwT) = _proj(
            x, norm_mix_g[l][None, :], _proj_weight(w_in, l), seg, gains, tm_proj)

        oa, ob, oc = _attn(aqT, ak, avT, bias_a, diff_lambda[l], diff_subln_g[l][None, :], sinks[l], bqT, bk, bvT,
                           bias_b, cqT, ck, cvT, iqT, ik, iwT, bias_c, lambda_init, top_k)

        w_r = jnp.concatenate([w_router_expert[l], w_router_group[l],
                               jnp.zeros((D, LANES - N_EXPERTS - N_GROUPS), F32)], axis=1)
        b_r = jnp.concatenate([b_router_expert[l], b_router_group[l],
                               jnp.zeros((LANES - N_EXPERTS - N_GROUPS,), F32)])[None, :]
        wrh = w_r.astype(BF16)
        wrl = (w_r - wrh.astype(F32)).astype(BF16)
        x1, h2, gate = _merge(
            x.reshape(T, D), oa.reshape(T, BRANCH_WIDTH), ob.reshape(T, BRANCH_WIDTH), oc.reshape(T, BRANCH_WIDTH),
            norm_mix_g[l][None, :], w_gate[l].astype(BF16), b_gate[l][None, :], w_branch[l].astype(BF16),
            w_out[l].astype(BF16), norm_ffn_g[l][None, :], wrh, wrl, b_r, tm_merge)

        wgu = jnp.concatenate([w_ff_gate[l], w_ff_up[l]], axis=-1).astype(BF16)
        wd = w_ff_down[l].astype(BF16).reshape(N_EXPERTS * EXPERT_FF, D)
        x = _moe(x1, h2, gate, wgu, wd, tm_moe).reshape(B, S, D)
    return x
```

```python
import functools
import math

import numpy as np
import jax
import jax.numpy as jnp
from jax import lax
from jax.experimental import pallas as pl
from jax.experimental.pallas import tpu as pltpu

F32 = jnp.float32
BF16 = jnp.bfloat16
I32 = jnp.int32
I16 = jnp.int16

D_MODEL = 1024
DEPTH = 2
CHUNK = 64
HEAD_DIM = 64
A_HEADS = 4
A_V_DIM = 2 * HEAD_DIM
B_HEADS = 8
B_KV_HEADS = 2
B_GROUP = B_HEADS // B_KV_HEADS
W_CHUNKS = 2
C_HEADS = 8
IDX_HEADS = 4
IDX_DIM = 32
TOPK_MAX = 256
NUM_BUCKETS = 32
MAX_DISTANCE = 1024
N_BRANCH = 3
BRANCH_WIDTH = 512
N_GROUPS = 4
EXPERTS_PER_GROUP = 4
N_EXPERTS = N_GROUPS * EXPERTS_PER_GROUP
EXPERT_FF = 256
MOE_STEP_EXPERTS = 8
EPS = 1e-6
NEG = -1e30
I16_MIN = -(2 ** 15)

LANES = 128
SUBLANES = 8
TILE = 256
NEAR_TILES = 4
BF16_ROWS = 16
A_VT_ROWS = A_V_DIM + BF16_ROWS
C_VT_ROWS = HEAD_DIM + BF16_ROWS
LOG2E = 1.4426950408889634
SKEW = 7
VMEM_CAP = 60000 * 1024

_C_AQ, _C_AK, _C_AV, _C_BQ, _C_BK, _C_BV, _C_CQ = 0, 512, 1024, 1536, 2048, 2176, 2304
_C_CKV, _C_IQ, _C_IKW = 2816, 2944, 3072
_IN_COLS = 3108
_W_COLS = 3200


def _dot(a, b):
    return jnp.dot(a, b, preferred_element_type=F32)


def _split_bf16(a):
    hi = a.astype(BF16)
    lo = (a - hi.astype(F32)).astype(BF16)
    return hi, lo


def _sigmoid(x):
    return 1.0 / (1.0 + jnp.exp(-x))


def _rms(x, g):
    return x * lax.rsqrt(jnp.mean(x * x, axis=-1, keepdims=True) + EPS) * g


def _colmax8(s):
    r, c = s.shape
    return jnp.max(s.reshape(r // SUBLANES, SUBLANES, c), axis=0)


def _colsum8(s):
    r, c = s.shape
    return jnp.sum(s.reshape(r // SUBLANES, SUBLANES, c), axis=0)


def _vmem_limit(nbytes):
    return int(min(VMEM_CAP, nbytes))


def _proj_kernel(x_ref, g_ref, w_ref, seg_ref, gaq_ref, gak_ref, gbq_ref, gbk_ref, gcq_ref, gck_ref,
                 aqT_ref, ak_ref, avT_ref, bqT_ref, bk_ref, bvT_ref, cqT_ref, ck_ref, cvT_ref,
                 iqT_ref, ik_ref, iwT_ref, *, tm, iw_scale):
    hb = _rms(x_ref[0], g_ref[...]).astype(BF16)
    seg = seg_ref[...]
    n_sub = tm // TILE

    def grp(a, n):
        return _dot(hb, w_ref[:, a:a + n])

    def segnorm(t, g):
        n = t.shape[1]
        ssq = _dot((t * t).astype(BF16), seg[:n, :n])
        return t * lax.rsqrt(ssq * (1.0 / HEAD_DIM) + EPS) * g

    def segnorm_t(t, gcol):
        n = t.shape[0] // HEAD_DIM
        t3 = t.reshape(n, HEAD_DIM, tm)
        ssq = jnp.sum(t3 * t3, axis=1, keepdims=True)
        return (t3 * lax.rsqrt(ssq * (1.0 / HEAD_DIM) + EPS)).reshape(t.shape) * gcol

    ones_rows = (lax.broadcasted_iota(I32, (BF16_ROWS, tm), 0) == 0).astype(F32)

    def put_slabs(ref, tT):
        for s in range(n_sub):
            ref[0, s] = tT[:, s * TILE:(s + 1) * TILE].astype(BF16)

    aqT_ref[0] = segnorm_t(grp(_C_AQ, BRANCH_WIDTH).T, gaq_ref[...]).astype(BF16)
    ak_ref[0] = segnorm(grp(_C_AK, BRANCH_WIDTH), gak_ref[...]).astype(BF16)
    avT = grp(_C_AV, BRANCH_WIDTH).T
    put_slabs(avT_ref, jnp.concatenate(
        [p for h in range(A_HEADS) for p in (avT[h * A_V_DIM:(h + 1) * A_V_DIM, :], ones_rows)], axis=0))
    bqT_ref[0] = segnorm_t(grp(_C_BQ, BRANCH_WIDTH).T, gbq_ref[...]).astype(BF16)
    cqT_ref[0] = segnorm_t(grp(_C_CQ, BRANCH_WIDTH).T, gcq_ref[...]).astype(BF16)
    bk_ref[0] = segnorm(grp(_C_BK, LANES), gbk_ref[...]).astype(BF16)
    bvT = grp(_C_BV, LANES).T
    put_slabs(bvT_ref, jnp.concatenate(
        [p for g in range(B_KV_HEADS) for p in (bvT[g * HEAD_DIM:(g + 1) * HEAD_DIM, :], ones_rows)], axis=0))
    iqT_ref[0] = grp(_C_IQ, LANES).T.astype(BF16)
    ckv = grp(_C_CKV, LANES)
    ck = ckv[:, :HEAD_DIM]
    ssq = jnp.sum(ck * ck, axis=-1, keepdims=True)
    ck_ref[0] = (ck * lax.rsqrt(ssq * (1.0 / HEAD_DIM) + EPS) * gck_ref[...]).astype(BF16)
    put_slabs(cvT_ref, jnp.concatenate([ckv.T[HEAD_DIM:, :], ones_rows], axis=0))
    ikw = grp(_C_IKW, LANES)
    ik_ref[0] = ikw[:, :IDX_DIM].astype(BF16)
    iwT_ref[0] = ikw.T[IDX_DIM:IDX_DIM + IDX_HEADS, :] * iw_scale


def _proj(x, g, w, seg, gains, tm):
    B, S, D = x.shape
    nt = S // TILE
    n_sub = tm // TILE
    full = lambda shape: pl.BlockSpec(shape, lambda b, t: (0,) * len(shape))
    out_shape = [
        jax.ShapeDtypeStruct((B, BRANCH_WIDTH, S), BF16),
        jax.ShapeDtypeStruct((B, S, BRANCH_WIDTH), BF16),
        jax.ShapeDtypeStruct((B, nt, A_HEADS * A_VT_ROWS, TILE), BF16),
        jax.ShapeDtypeStruct((B, BRANCH_WIDTH, S), BF16),
        jax.ShapeDtypeStruct((B, S, LANES), BF16),
        jax.ShapeDtypeStruct((B, nt, B_KV_HEADS * C_VT_ROWS, TILE), BF16),
        jax.ShapeDtypeStruct((B, BRANCH_WIDTH, S), BF16),
        jax.ShapeDtypeStruct((B, S, HEAD_DIM), BF16),
        jax.ShapeDtypeStruct((B, nt, C_VT_ROWS, TILE), BF16),
        jax.ShapeDtypeStruct((B, LANES, S), BF16),
        jax.ShapeDtypeStruct((B, S, IDX_DIM), BF16),
        jax.ShapeDtypeStruct((B, IDX_HEADS, S), F32),
    ]
    colT = lambda r: pl.BlockSpec((1, r, tm), lambda b, t: (b, 0, t))
    row = lambda c: pl.BlockSpec((1, tm, c), lambda b, t: (b, t, 0))
    slab = lambda r: pl.BlockSpec((1, n_sub, r, TILE), lambda b, t: (b, t, 0, 0))
    out_specs = [colT(BRANCH_WIDTH), row(BRANCH_WIDTH), slab(A_HEADS * A_VT_ROWS), colT(BRANCH_WIDTH),
                 row(LANES),
                 slab(B_KV_HEADS * C_VT_ROWS), colT(BRANCH_WIDTH), row(HEAD_DIM), slab(C_VT_ROWS), colT(LANES), row(IDX_DIM),
                 pl.BlockSpec((1, IDX_HEADS, tm), lambda b, t: (b, 0, t))]
    in_specs = [pl.BlockSpec((1, tm, D), lambda b, t: (b, t, 0)), full((1, D)), full((D, _W_COLS)),
                full((BRANCH_WIDTH, BRANCH_WIDTH))] + [full(gn.shape) for gn in gains]
    vmem = 2 * (tm * D * 4 + D * _W_COLS * 2 + BRANCH_WIDTH * BRANCH_WIDTH * 2 + tm * 3400 * 2) + 24 * tm * BRANCH_WIDTH * 4
    return pl.pallas_call(
        functools.partial(_proj_kernel, tm=tm, iw_scale=IDX_HEADS ** -0.5 * IDX_DIM ** -0.5),
        grid=(B, S // tm), in_specs=in_specs, out_specs=out_specs, out_shape=out_shape,
        compiler_params=pltpu.CompilerParams(dimension_semantics=("arbitrary", "arbitrary"),
                                             vmem_limit_bytes=_vmem_limit(vmem)),
        name="proj",
    )(x, g, w, seg, *gains)


def _online_step(s, m_old, vt, acc_ref, ch):
    if s.dtype == BF16:
        r, c = s.shape
        cm = jnp.max(s.reshape(r // BF16_ROWS, BF16_ROWS, c), axis=0).astype(F32)
        m_new = jnp.maximum(m_old, jnp.max(cm, axis=0, keepdims=True))
        e = jnp.exp2(s - m_new.astype(BF16))
    else:
        m_new = jnp.maximum(m_old, jnp.max(_colmax8(s), axis=0, keepdims=True))
        e = jnp.exp2(s - m_new).astype(BF16)
    alpha = jnp.exp2(m_old - m_new)
    acc_ref[ch] = acc_ref[ch] * alpha + _dot(vt, e)
    return m_new


def _sweep_key_tiles(i, n_chain, scores, vt, acc_ref):
    n_far = jnp.maximum(i - (NEAR_TILES - 1), 0)

    def step(tiles, ms):
        ms = list(ms)
        chains = [(j, d, ch) for (j, d) in tiles for ch in range(n_chain)]
        pending = [scores(*c) for c in chains[:SKEW]]
        for n, (j, d, ch) in enumerate(chains):
            s = pending.pop(0)
            if n + SKEW < len(chains):
                pending.append(scores(*chains[n + SKEW]))
            ms[ch] = _online_step(s, ms[ch], vt(j, ch), acc_ref, ch)
        return tuple(ms)

    def near_pairs(ms):
        ms = step([(i - 3, 3), (i - 2, 2)], ms)
        return step([(i - 1, 1), (i, 0)], ms)

    def near_singles(ms):
        return lax.fori_loop(0, i + 1, lambda t, ms: step([(t, i - t)], ms), ms)

    assert NEAR_TILES == 4
    ms = tuple(jnp.full((1, TILE), -jnp.inf, F32) for _ in range(n_chain))
    ms = lax.fori_loop(0, n_far // 4, lambda p, ms: step([(4 * p + u, None) for u in range(4)], ms), ms)
    rest = n_far - n_far % 4
    ms = lax.cond(n_far % 4 >= 2, lambda ms: step([(rest, None), (rest + 1, None)], ms), lambda ms: ms, ms)
    ms = lax.cond(n_far % 2 == 1, lambda ms: step([(n_far - 1, None)], ms), lambda ms: ms, ms)
    lax.cond(i >= NEAR_TILES - 1, near_pairs, near_singles, ms)


def _attn_a_kernel(qT_ref, k_ref, vT_ref, bias_ref, lam_ref, sub_ref, o_ref, q2_ref, acc_ref, *, lambda_init):
    i = pl.program_id(1)
    lp = lam_ref[...]
    lam = (jnp.exp(jnp.sum(lp[0:1] * lp[1:2], axis=-1, keepdims=True))
           - jnp.exp(jnp.sum(lp[2:3] * lp[3:4], axis=-1, keepdims=True)) + lambda_init)
    row = lax.broadcasted_iota(I32, (2 * HEAD_DIM, TILE), 0)
    n_chain = 2 * A_HEADS

    for h in range(A_HEADS):
        qh = qT_ref[0, h * A_V_DIM:(h + 1) * A_V_DIM, :]
        zero = jnp.zeros_like(qh)
        q2_ref[2 * h] = jnp.where(row < HEAD_DIM, qh, zero)
        q2_ref[2 * h + 1] = jnp.where(row >= HEAD_DIM, qh, zero)
    acc_ref[...] = jnp.zeros_like(acc_ref)

    def scores(j, d, ch):
        h = ch // 2
        rows = pl.ds(pl.multiple_of(j * TILE, TILE), TILE)
        s = _dot(k_ref[0, rows, h * A_V_DIM:(h + 1) * A_V_DIM], q2_ref[ch]).astype(BF16)
        return s if d is None else s + bias_ref[h, d]

    def vt(j, ch):
        h = ch // 2
        return vT_ref[0, j, h * A_VT_ROWS:(h + 1) * A_VT_ROWS, :]

    _sweep_key_tiles(i, n_chain, scores, vt, acc_ref)

    for h in range(A_HEADS):
        a0 = acc_ref[2 * h]
        a1 = acc_ref[2 * h + 1]
        r0 = 1.0 / a0[A_V_DIM:A_V_DIM + 1, :]
        r1 = 1.0 / a1[A_V_DIM:A_V_DIM + 1, :]
        outT = a0[:A_V_DIM, :] * r0 - lam * (a1[:A_V_DIM, :] * r1)
        out = _rms(outT.T, sub_ref[...]) * (1.0 - lambda_init)
        o_ref[0, :, h * A_V_DIM:(h + 1) * A_V_DIM] = out.astype(BF16)


def _attn_b_kernel(sink_ref, qT_ref, k_ref, vT_ref, bias_ref, o_ref, oT_ref):
    i = pl.program_id(1)
    jp = jnp.maximum(i - 1, 0)
    p_idx = jnp.where(i > 0, 1, 2)
    cur = pl.ds(pl.multiple_of(i * TILE, TILE), TILE)
    prev = pl.ds(pl.multiple_of(jp * TILE, TILE), TILE)

    def scores(h):
        qh = qT_ref[0, h * HEAD_DIM:(h + 1) * HEAD_DIM, :]
        zero = jnp.zeros_like(qh)
        q2 = jnp.concatenate([qh, zero] if h < B_GROUP else [zero, qh], axis=0)
        return (_dot(k_ref[0, cur, :], q2) + bias_ref[h, 0].astype(F32),
                _dot(k_ref[0, prev, :], q2) + bias_ref[h, p_idx].astype(F32))

    def finish(h, s):
        sc, sp = s
        gs = slice((h // B_GROUP) * C_VT_ROWS, (h // B_GROUP + 1) * C_VT_ROWS)
        sink = sink_ref[h] * LOG2E
        m = jnp.max(jnp.maximum(_colmax8(sc), _colmax8(sp)), axis=0, keepdims=True)
        m = jnp.maximum(m, sink)
        ec = jnp.exp2(sc - m).astype(BF16)
        ep = jnp.exp2(sp - m).astype(BF16)
        outT = _dot(vT_ref[0, i, gs, :], ec) + _dot(vT_ref[0, jp, gs, :], ep)
        den = outT[HEAD_DIM:HEAD_DIM + 1, :] + jnp.exp2(sink - m)
        oT_ref[h * HEAD_DIM:(h + 1) * HEAD_DIM, :] = outT[:HEAD_DIM, :] * (1.0 / den)

    pending = [scores(h) for h in range(SKEW)]
    for h in range(B_HEADS):
        s = pending.pop(0)
        if h + SKEW < B_HEADS:
            pending.append(scores(h + SKEW))
        finish(h, s)
    o_ref[0] = oT_ref[...].T.astype(BF16)


def _attn_c_kernel(qT_ref, k_ref, vT_ref, iqT_ref, ik_ref, iwT_ref, bias_ref, o_ref,
                   hi_ref, lo_ref, mb_ref, acc_ref, oT_ref, *, top_k):
    i = pl.program_id(1)
    n_t = i + 1
    krow = lax.broadcasted_iota(I32, (TILE, TILE), 0)
    qcol = lax.broadcasted_iota(I32, (TILE, TILE), 1)
    allowed = (krow // CHUNK) <= (qcol // CHUNK)

    def idx_keys(j, diag):
        ikt = ik_ref[0, pl.ds(pl.multiple_of(j * TILE, TILE), TILE), :]
        lgs = [_dot(ikt, iqT_ref[0, hh * IDX_DIM:(hh + 1) * IDX_DIM, :]) for hh in range(IDX_HEADS)]
        sc = jnp.zeros((TILE, TILE), F32)
        for hh in range(IDX_HEADS):
            sc = sc + jnp.maximum(lgs[hh], 0.0) * iwT_ref[0, hh:hh + 1, :]
        if diag:
            sc = jnp.where(allowed, sc, NEG)
        bits = lax.bitcast_convert_type(sc, I32)
        hi_b = (bits >> 16).astype(I16)
        lo_b = bits.astype(I16)
        sign = jnp.where(hi_b < 0, jnp.int16(-1), jnp.int16(0))
        hi_ref[j] = hi_b ^ (sign & 0x7FFF)
        lo_ref[j] = lo_b ^ sign ^ I16_MIN

    def fill_pair(p, carry):
        idx_keys(2 * p, False)
        idx_keys(2 * p + 1, False)
        return carry

    lax.fori_loop(0, i // 2, fill_pair, 0)

    @pl.when(i % 2 == 1)
    def _():
        idx_keys(i - 1, False)

    idx_keys(i, True)

    n_pair = (n_t + 1) // 2

    @pl.when(n_t % 2 == 1)
    def _():
        hi_ref[n_t] = jnp.full((TILE, TILE), I16_MIN, I16)
        lo_ref[n_t] = jnp.full((TILE, TILE), I16_MIN, I16)

    groups = TILE // BF16_ROWS

    def rows16(ref, j):
        return ref[j].reshape(groups, BF16_ROWS, TILE)

    def bcast16(v):
        return jnp.broadcast_to(v, (BF16_ROWS, TILE)).astype(I16)

    def count(pred):
        def body(p, cs):
            cs = list(cs)
            for u in (0, 1):
                j = 2 * p + u
                hi, lo = rows16(hi_ref, j), rows16(lo_ref, j)
                for r in range(groups):
                    cs[r % len(cs)] = cs[r % len(cs)] + pred(hi[r], lo[r], j, r).astype(I16)
            return tuple(cs)
        cs = lax.fori_loop(0, n_pair, body, (jnp.zeros((BF16_ROWS, TILE), I16),) * 4)
        c = (cs[0] + cs[1]) + (cs[2] + cs[3])
        return jnp.sum(c.astype(I32), axis=0, keepdims=True)

    def search(n_bits, accept):
        def step(b, t):
            cand = t + lax.shift_left(jnp.int32(1), n_bits - 1 - b)
            return jnp.where(accept(cand), cand, t)
        return lax.fori_loop(0, n_bits, step, jnp.full((1, TILE), I16_MIN, I32))

    def hi_accept(cand):
        c16 = bcast16(cand)
        return count(lambda hi, lo, j, r: hi >= c16) >= top_k

    t_hi = search(16, hi_accept)
    th16 = bcast16(t_hi)
    cnt_above = count(lambda hi, lo, j, r: hi > th16)
    r_lo = top_k - cnt_above

    def bucket_only(p, carry):
        for u in (0, 1):
            j = 2 * p + u
            hi, lo = rows16(hi_ref, j), rows16(lo_ref, j)
            for r in range(groups):
                lo_ref[j, r * BF16_ROWS:(r + 1) * BF16_ROWS, :] = jnp.where(hi[r] == th16, lo[r], I16_MIN)
        return carry

    lax.fori_loop(0, n_pair, bucket_only, 0)

    def lo_accept(cand):
        c16 = bcast16(cand)
        return count(lambda hi, lo, j, r: lo >= c16) >= r_lo

    t_lo = search(16, lo_accept)
    tl16 = bcast16(t_lo)
    cnt_gt = count(lambda hi, lo, j, r: lo > tl16)
    r_eq = (r_lo - cnt_gt).astype(F32)

    tri = (krow >= qcol).astype(BF16)
    one, zero, neg16 = jnp.ones((), BF16), jnp.zeros((), BF16), jnp.full((), NEG, BF16)

    def mask_tile(j, ties_before, diag):
        hi, lo = rows16(hi_ref, j), rows16(lo_ref, j)
        eqs = [(lo[r] == tl16) & (hi[r] == th16) for r in range(groups)]
        ties = _dot(tri, jnp.concatenate([jnp.where(e, one, zero) for e in eqs], axis=0)) + ties_before
        over = (ties - r_eq).astype(BF16)
        rows = []
        for r in range(groups):
            kept_tie = eqs[r] & (over[r * BF16_ROWS:(r + 1) * BF16_ROWS, :] <= zero)
            rows.append(jnp.where((hi[r] > th16) | (lo[r] > tl16) | kept_tie, zero, neg16))
        mb = jnp.concatenate(rows, axis=0)
        if diag:
            mb = jnp.where(allowed, mb.astype(F32), NEG).astype(BF16)
        mb_ref[j] = mb
        return ties[TILE - 1:TILE, :]

    ties_before = lax.fori_loop(0, i // 2, lambda p, c: mask_tile(2 * p + 1, mask_tile(2 * p, c, False), False),
                                jnp.zeros((1, TILE), F32))
    ties_before = lax.cond(i % 2 == 1, lambda c: mask_tile(i - 1, c, False), lambda c: c, ties_before)
    mask_tile(i, ties_before, True)

    acc_ref[...] = jnp.zeros_like(acc_ref)

    def scores(j, d, h):
        kt = k_ref[0, pl.ds(pl.multiple_of(j * TILE, TILE), TILE), :]
        s = _dot(kt, qT_ref[0, h * HEAD_DIM:(h + 1) * HEAD_DIM, :]).astype(BF16) + mb_ref[j]
        return s if d is None else s + bias_ref[h, d]

    _sweep_key_tiles(i, C_HEADS, scores, lambda j, h: vT_ref[0, j], acc_ref)

    for h in range(C_HEADS):
        a = acc_ref[h]
        oT_ref[h * HEAD_DIM:(h + 1) * HEAD_DIM, :] = a[:HEAD_DIM, :] * (1.0 / a[HEAD_DIM:HEAD_DIM + 1, :])
    o_ref[0] = oT_ref[...].T.astype(BF16)


def _attn_kernel(*refs, lambda_init, top_k):
    a_in, b_in, c_in = refs[0:6], refs[6:11], refs[11:18]
    oa_ref, ob_ref, oc_ref = refs[18:21]
    a_scratch, b_scratch, c_scratch = refs[21:23], refs[23:24], refs[24:29]
    _attn_a_kernel(*a_in, oa_ref, *a_scratch, lambda_init=lambda_init)
    _attn_b_kernel(*b_in, ob_ref, *b_scratch)
    _attn_c_kernel(*c_in, oc_ref, *c_scratch, top_k=top_k)


def _attn(aqT, ak, avT, bias_a, lam_par, subln_g, sinks, bqT, bk, bvT, bias_b, cqT, ck, cvT, iqT, ik, iwT, bias_c,
          lambda_init, top_k):
    B, _, S = aqT.shape
    nt = S // TILE
    const = lambda a: pl.BlockSpec(a.shape, lambda b, i: (0,) * a.ndim, pipeline_mode=pl.Buffered(1))
    qtile = lambda rows: pl.BlockSpec((1, rows, TILE), lambda b, i: (b, 0, i))
    per_batch = lambda a: pl.BlockSpec((1,) + a.shape[1:], lambda b, i: (b,) + (0,) * (a.ndim - 1))
    in_specs = [qtile(BRANCH_WIDTH), per_batch(ak), per_batch(avT), const(bias_a), const(lam_par), const(subln_g),
                pl.BlockSpec(memory_space=pltpu.SMEM), qtile(BRANCH_WIDTH), per_batch(bk), per_batch(bvT),
                const(bias_b),
                qtile(BRANCH_WIDTH), per_batch(ck), per_batch(cvT), qtile(LANES), per_batch(ik), qtile(IDX_HEADS),
                const(bias_c)]
    out_spec = pl.BlockSpec((1, TILE, BRANCH_WIDTH), lambda b, i: (b, i, 0))
    out_shape = jax.ShapeDtypeStruct((B, S, BRANCH_WIDTH), BF16)
    scratch = [pltpu.VMEM((2 * A_HEADS, A_V_DIM, TILE), BF16),
               pltpu.VMEM((2 * A_HEADS, A_VT_ROWS, TILE), F32),
               pltpu.VMEM((BRANCH_WIDTH, TILE), F32),
               pltpu.VMEM((nt + nt % 2, TILE, TILE), I16),
               pltpu.VMEM((nt + nt % 2, TILE, TILE), I16),
               pltpu.VMEM((nt, TILE, TILE), BF16),
               pltpu.VMEM((C_HEADS, C_VT_ROWS, TILE), F32),
               pltpu.VMEM((BRANCH_WIDTH, TILE), F32)]
    per_batch_bytes = sum(a.size // B * a.dtype.itemsize for a in (ak, avT, bk, bvT, ck, cvT, ik))
    const_bytes = sum(a.size * a.dtype.itemsize for a in (bias_a, bias_b, bias_c))
    scratch_bytes = (2 * A_HEADS * (A_V_DIM * 2 + A_VT_ROWS * 4) + 2 * BRANCH_WIDTH * 4 + C_HEADS * C_VT_ROWS * 4) * TILE \
        + (2 * (nt + nt % 2) * 2 + nt * 2) * TILE * TILE
    vmem = 2 * per_batch_bytes + const_bytes + scratch_bytes + 12 * TILE * BRANCH_WIDTH * 2 + 64 * TILE * TILE * 4
    return pl.pallas_call(
        functools.partial(_attn_kernel, lambda_init=lambda_init, top_k=top_k),
        grid=(B, nt), in_specs=in_specs, out_specs=[out_spec] * 3, out_shape=[out_shape] * 3,
        scratch_shapes=scratch,
        compiler_params=pltpu.CompilerParams(dimension_semantics=("arbitrary", "arbitrary"),
                                             vmem_limit_bytes=_vmem_limit(vmem)),
        name="attn",
    )(aqT, ak, avT, bias_a, lam_par, subln_g, sinks, bqT, bk, bvT, bias_b, cqT, ck, cvT, iqT, ik, iwT, bias_c)


def _merge_kernel(x_ref, oa_ref, ob_ref, oc_ref, gmix_ref, wg_ref, bg_ref, wb_ref, wo_ref, gffn_ref,
                  wrh_ref, wrl_ref, br_ref, x1_ref, h2_ref, gate_ref):
    x = x_ref[...]
    hb = _rms(x, gmix_ref[...]).astype(BF16)
    z = None
    for n, o_ref in enumerate((oa_ref, ob_ref, oc_ref)):
        cs = slice(n * D_MODEL, (n + 1) * D_MODEL)
        gate = _sigmoid(_dot(hb, wg_ref[:, cs]) + bg_ref[:, cs])
        y = _dot(o_ref[...], wb_ref[n])
        z = gate * y if z is None else z + gate * y
    x1 = x + _dot(z.astype(BF16), wo_ref[...])
    x1_ref[...] = x1
    h2 = _rms(x1, gffn_ref[...])
    h2_ref[...] = h2.astype(BF16)

    hi, lo = _split_bf16(h2)
    lg = _dot(hi, wrh_ref[...]) + _dot(lo, wrh_ref[...]) + _dot(hi, wrl_ref[...]) + br_ref[...]
    col = lax.broadcasted_iota(I32, lg.shape, 1).astype(F32)
    big = float(4 * LANES)
    is_g = (col >= N_EXPERTS) & (col < N_EXPERTS + N_GROUPS)
    gl = jnp.where(is_g, lg, -jnp.inf)
    gmax = jnp.max(gl, axis=-1, keepdims=True)
    p_group = 1.0 / jnp.sum(jnp.exp(gl - gmax), axis=-1, keepdims=True)
    g_sel = jnp.min(jnp.where(gl == gmax, col, big), axis=-1, keepdims=True) - N_EXPERTS
    in_g = (col >= g_sel * EXPERTS_PER_GROUP) & (col < (g_sel + 1) * EXPERTS_PER_GROUP)
    el = jnp.where(in_g, lg, -jnp.inf)
    e1 = jnp.max(el, axis=-1, keepdims=True)
    i1 = jnp.min(jnp.where(el == e1, col, big), axis=-1, keepdims=True)
    el2 = jnp.where(col == i1, -jnp.inf, el)
    e2 = jnp.max(el2, axis=-1, keepdims=True)
    i2 = jnp.min(jnp.where(el2 == e2, col, big), axis=-1, keepdims=True)
    t2 = jnp.exp(e2 - e1)
    w1 = p_group / (1.0 + t2)
    w2 = w1 * t2
    gates = jnp.where(col == i1, w1, 0.0) + jnp.where(col == i2, w2, 0.0)
    gate_ref[...] = gates[:, :N_EXPERTS]


def _merge(x2, oa, ob, oc, gmix, wg, bg, wb, wo, gffn, wrh, wrl, br, tm):
    T, D = x2.shape
    full = lambda a: pl.BlockSpec(a.shape, lambda t: (0,) * a.ndim)
    rowb = lambda c: pl.BlockSpec((tm, c), lambda t: (t, 0))
    vmem = 2 * (tm * D * 4 * 2 + 3 * tm * BRANCH_WIDTH * 2 + tm * D * 2 + wg.size * 2 + wb.size * 2 + wo.size * 2
                + 2 * D * LANES * 2) + 10 * tm * D * 4
    return pl.pallas_call(
        _merge_kernel,
        grid=(T // tm,),
        in_specs=[rowb(D), rowb(BRANCH_WIDTH), rowb(BRANCH_WIDTH), rowb(BRANCH_WIDTH), full(gmix), full(wg), full(bg), full(wb),
                  full(wo), full(gffn), full(wrh), full(wrl), full(br)],
        out_specs=[rowb(D), rowb(D), rowb(N_EXPERTS)],
        out_shape=[jax.ShapeDtypeStruct((T, D), F32), jax.ShapeDtypeStruct((T, D), BF16),
                   jax.ShapeDtypeStruct((T, N_EXPERTS), F32)],
        compiler_params=pltpu.CompilerParams(dimension_semantics=("arbitrary",),
                                             vmem_limit_bytes=_vmem_limit(vmem)),
        name="merge",
    )(x2, oa, ob, oc, gmix, wg, bg, wb, wo, gffn, wrh, wrl, br)


def _moe_kernel(x1_ref, h2_ref, gate_ref, wgu_ref, wd_ref, o_ref):
    g = pl.program_id(1)
    h2 = h2_ref[...]
    gate = gate_ref[...]
    lane = lax.broadcasted_iota(I32, gate.shape, 1)
    hids = []
    for u in range(MOE_STEP_EXPERTS):
        gu = _dot(h2, wgu_ref[u])
        a = gu[:, :EXPERT_FF]
        w = jnp.sum(jnp.where(lane == g * MOE_STEP_EXPERTS + u, gate, 0.0), axis=-1, keepdims=True)
        hids.append((a * _sigmoid(a) * gu[:, EXPERT_FF:] * w).astype(BF16))
    y = _dot(jnp.concatenate(hids, axis=1), wd_ref[...])

    @pl.when(g == 0)
    def _():
        o_ref[...] = x1_ref[...] + y

    @pl.when(g > 0)
    def _():
        o_ref[...] += y


def _moe(x1, h2, gate, wgu, wd, tm):
    T, D = x1.shape
    n_e, ff = MOE_STEP_EXPERTS, EXPERT_FF
    vmem = 2 * (tm * D * 4 * 2 + tm * D * 2 + tm * LANES * 4 + n_e * D * 2 * ff * 2 + n_e * ff * D * 2) \
        + 3 * n_e * tm * 2 * ff * 4 + 2 * tm * D * 4
    return pl.pallas_call(
        _moe_kernel,
        grid=(T // tm, N_EXPERTS // n_e),
        in_specs=[pl.BlockSpec((tm, D), lambda t, g: (t, 0)),
                  pl.BlockSpec((tm, D), lambda t, g: (t, 0)),
                  pl.BlockSpec((tm, N_EXPERTS), lambda t, g: (t, 0)),
                  pl.BlockSpec((n_e, D, 2 * ff), lambda t, g: (g, 0, 0)),
                  pl.BlockSpec((n_e * ff, D), lambda t, g: (g, 0))],
        out_specs=pl.BlockSpec((tm, D), lambda t, g: (t, 0)),
        out_shape=jax.ShapeDtypeStruct((T, D), F32),
        compiler_params=pltpu.CompilerParams(dimension_semantics=("arbitrary", "arbitrary"),
                                             vmem_limit_bytes=_vmem_limit(vmem)),
        name="moe",
    )(x1, h2, gate, wgu, wd)


def _t5_bucket_np(rel):
    half = NUM_BUCKETS // 2
    max_exact = half // 2
    n = np.abs(rel)
    n_f = np.maximum(n, 1).astype(np.float32)
    large = max_exact + (np.log(n_f / np.float32(max_exact)) / np.float32(math.log(MAX_DISTANCE / max_exact))
                         * np.float32(half - max_exact)).astype(np.int32)
    large = np.minimum(large, half - 1)
    return np.where(rel > 0, half, 0) + np.where(n < max_exact, n, large)


def _toeplitz_kernel(v_ref, o_ref):
    v = v_ref[0]
    for d in range(NEAR_TILES):
        x = jnp.broadcast_to(v[d:d + 1, :], (TILE, 2 * TILE))
        o_ref[0, d] = pltpu.roll(x, 0, 1, stride=1, stride_axis=0)[:, :TILE]


def _toeplitz(vals):
    n_heads = vals.shape[0]
    return pl.pallas_call(
        _toeplitz_kernel,
        grid=(n_heads,),
        in_specs=[pl.BlockSpec((1, NEAR_TILES, 2 * TILE), lambda h: (h, 0, 0))],
        out_specs=pl.BlockSpec((1, NEAR_TILES, TILE, TILE), lambda h: (h, 0, 0, 0)),
        out_shape=jax.ShapeDtypeStruct((n_heads, NEAR_TILES, TILE, TILE), F32),
        name="bias_tiles",
    )(vals)


def _bias_tables(rel_bias):
    u = np.arange(2 * TILE)
    off = np.where(u < TILE, u, u - 2 * TILE)
    rel = np.stack([-off - TILE * d for d in range(NEAR_TILES)])
    onehot = (_t5_bucket_np(rel)[..., None] == np.arange(NUM_BUCKETS)).astype(np.float32)
    vals = jnp.einsum("dub,bh->hdu", jnp.asarray(onehot), rel_bias, precision=lax.Precision.HIGHEST)
    tab = _toeplitz(vals)

    far = rel_bias[NUM_BUCKETS // 2 - 1]
    kk = np.arange(TILE)[:, None]
    qc = np.arange(TILE)[None, :] // CHUNK
    kc = np.stack([(kk // CHUNK) - (TILE // CHUNK) * d for d in range(NEAR_TILES)]) + 0 * qc
    a_ok = jnp.asarray(kc <= qc)
    ta = jnp.where(a_ok[None], (tab[:A_HEADS] - far[:A_HEADS, None, None, None]) * LOG2E, NEG)
    b_ok = jnp.asarray((qc - kc[:2] >= 0) & (qc - kc[:2] <= W_CHUNKS))
    tb = jnp.where(b_ok[None], tab[A_HEADS:A_HEADS + B_HEADS, :2] * LOG2E, NEG)
    tb = jnp.concatenate([tb, jnp.full((B_HEADS, 1, TILE, TILE), NEG, F32)], axis=1)
    tc = (tab[A_HEADS + B_HEADS:] - far[A_HEADS + B_HEADS:, None, None, None]) * LOG2E
    return ta.astype(BF16), tb.astype(BF16), tc.astype(BF16)


def _proj_weight_kernel(w_ref, o_ref):
    o_ref[...] = jnp.zeros_like(o_ref)
    o_ref[:, :_IN_COLS] = w_ref[0].astype(BF16)


def _proj_weight(w_in, l):
    _, d, n = w_in.shape
    assert n == _IN_COLS
    rows = 128
    return pl.pallas_call(
        _proj_weight_kernel,
        grid=(d // rows,),
        in_specs=[pl.BlockSpec((1, rows, n), lambda r: (l, r, 0))],
        out_specs=pl.BlockSpec((rows, _W_COLS), lambda r: (r, 0)),
        out_shape=jax.ShapeDtypeStruct((d, _W_COLS), BF16),
        name="proj_weight",
    )(w_in)


def kernel(x, rel_bias, norm_mix_g, w_in, qk_norm_g, diff_lambda, diff_subln_g, sinks, w_branch, w_gate, b_gate,
           w_out, norm_ffn_g, w_router_group, b_router_group, w_router_expert, b_router_expert, w_ff_gate,
           w_ff_up, w_ff_down):
    B, S, D = x.shape
    assert D == D_MODEL and S % TILE == 0
    T = B * S
    top_k = min(TOPK_MAX, S // 4)
    tm_proj = 1024 if S % 1024 == 0 else TILE
    tm_merge = 512 if T % 512 == 0 else TILE
    tm_moe = 1024 if T % 1024 == 0 else TILE

    bias_a, bias_b, bias_c = _bias_tables(rel_bias)
    seg = jnp.asarray(np.kron(np.eye(BRANCH_WIDTH // HEAD_DIM), np.ones((HEAD_DIM, HEAD_DIM))), BF16)
    q_scale = HEAD_DIM ** -0.5

    for l in range(DEPTH):
        lambda_init = 0.8 - 0.6 * math.exp(-0.3 * l)
        qg = qk_norm_g[l]
        tile8 = lambda g: jnp.tile(g, BRANCH_WIDTH // HEAD_DIM)
        gains = ((tile8(qg[0, 0]) * (q_scale * LOG2E))[:, None], tile8(qg[0, 1])[None, :],
                 (tile8(qg[1, 0]) * (q_scale * LOG2E))[:, None], jnp.tile(qg[1, 1], B_KV_HEADS)[None, :],
                 (tile8(qg[2, 0]) * (q_scale * LOG2E))[:, None], qg[2, 1][None, :])
        (aqT, ak, avT, bqT, bk, bvT, cqT, ck, cvT, iqT, ik, iwT) = _proj(
            x, norm_mix_g[l][None, :], _proj_weight(w_in, l), seg, gains, tm_proj)

        oa, ob, oc = _attn(aqT, ak, avT, bias_a, diff_lambda[l], diff_subln_g[l][None, :], sinks[l], bqT, bk, bvT,
                           bias_b, cqT, ck, cvT, iqT, ik, iwT, bias_c, lambda_init, top_k)

        w_r = jnp.concatenate([w_router_expert[l], w_router_group[l],
                               jnp.zeros((D, LANES - N_EXPERTS - N_GROUPS), F32)], axis=1)
        b_r = jnp.concatenate([b_router_expert[l], b_router_group[l],
                               jnp.zeros((LANES - N_EXPERTS - N_GROUPS,), F32)])[None, :]
        wrh = w_r.astype(BF16)
        wrl = (w_r - wrh.astype(F32)).astype(BF16)
        x1, h2, gate = _merge(
            x.reshape(T, D), oa.reshape(T, BRANCH_WIDTH), ob.reshape(T, BRANCH_WIDTH), oc.reshape(T, BRANCH_WIDTH),
            norm_mix_g[l][None, :], w_gate[l].astype(BF16), b_gate[l][None, :], w_branch[l].astype(BF16),
            w_out[l].astype(BF16), norm_ffn_g[l][None, :], wrh, wrl, b_r, tm_merge)

        wgu = jnp.concatenate([w_ff_gate[l], w_ff_up[l]], axis=-1).astype(BF16)
        wd = w_ff_down[l].astype(BF16).reshape(N_EXPERTS * EXPERT_FF, D)
        x = _moe(x1, h2, gate, wgu, wd, tm_moe).reshape(B, S, D)
    return x
```

```python
import functools
import math

import numpy as np
import jax
import jax.numpy as jnp
from jax import lax
from jax.experimental import pallas as pl
from jax.experimental.pallas import tpu as pltpu

F32 = jnp.float32
BF16 = jnp.bfloat16
I32 = jnp.int32
I16 = jnp.int16

D_MODEL = 1024
DEPTH = 2
CHUNK = 64
HEAD_DIM = 64
A_HEADS = 4
A_V_DIM = 2 * HEAD_DIM
B_HEADS = 8
B_KV_HEADS = 2
B_GROUP = B_HEADS // B_KV_HEADS
W_CHUNKS = 2
C_HEADS = 8
IDX_HEADS = 4
IDX_DIM = 32
TOPK_MAX = 256
NUM_BUCKETS = 32
MAX_DISTANCE = 1024
BRANCH_WIDTH = 512
N_GROUPS = 4
EXPERTS_PER_GROUP = 4
N_EXPERTS = N_GROUPS * EXPERTS_PER_GROUP
EXPERT_FF = 256
MOE_STEP_EXPERTS = 8
EPS = 1e-6
NEG = -1e30
I16_MIN = -(2 ** 15)

LANES = 128
SUBLANES = 8
TILE = 256
NEAR_TILES = 4
BF16_ROWS = 16
A_VT_ROWS = A_V_DIM + BF16_ROWS
C_VT_ROWS = HEAD_DIM + BF16_ROWS
LOG2E = 1.4426950408889634
SKEW = 7
VMEM_CAP = 60000 * 1024

_C_AQ, _C_AK, _C_AV, _C_BQ, _C_BK, _C_BV, _C_CQ = 0, 512, 1024, 1536, 2048, 2176, 2304
_C_CKV, _C_IQ, _C_IKW = 2816, 2944, 3072
_IN_COLS = 3108
_W_COLS = 3200


def _dot(a, b):
    return jnp.dot(a, b, preferred_element_type=F32)


def _split_bf16(a):
    hi = a.astype(BF16)
    lo = (a - hi.astype(F32)).astype(BF16)
    return hi, lo


def _sigmoid(x):
    return 1.0 / (1.0 + jnp.exp(-x))


def _rms(x, g):
    return x * lax.rsqrt(jnp.mean(x * x, axis=-1, keepdims=True) + EPS) * g


def _colmax8(s):
    r, c = s.shape
    return jnp.max(s.reshape(r // SUBLANES, SUBLANES, c), axis=0)


def _colmax16(s):
    r, c = s.shape
    return jnp.max(s.reshape(r // BF16_ROWS, BF16_ROWS, c), axis=0)


def _vmem_limit(nbytes):
    return int(min(VMEM_CAP, nbytes))


def _proj_kernel(x_ref, g_ref, w_ref, seg_ref, gaq_ref, gak_ref, gbq_ref, gbk_ref, gcq_ref, gck_ref,
                 aqT_ref, ak_ref, avT_ref, bqT_ref, bk_ref, bvT_ref, cqT_ref, ck_ref, cvT_ref,
                 iqT_ref, ik_ref, iwT_ref, *, tm, iw_scale):
    hb = _rms(x_ref[0], g_ref[...]).astype(BF16)
    seg = seg_ref[...]
    n_sub = tm // TILE

    def grp(a, n):
        return _dot(hb, w_ref[:, a:a + n])

    def segnorm(t, g):
        n = t.shape[1]
        ssq = _dot((t * t).astype(BF16), seg[:n, :n])
        return t * lax.rsqrt(ssq * (1.0 / HEAD_DIM) + EPS) * g

    def segnorm_t(t, gcol):
        n = t.shape[0] // HEAD_DIM
        t3 = t.reshape(n, HEAD_DIM, tm)
        ssq = jnp.sum(t3 * t3, axis=1, keepdims=True)
        return (t3 * lax.rsqrt(ssq * (1.0 / HEAD_DIM) + EPS)).reshape(t.shape) * gcol

    ones_rows = (lax.broadcasted_iota(I32, (BF16_ROWS, tm), 0) == 0).astype(F32)

    def put_slabs(ref, tT):
        for s in range(n_sub):
            ref[0, s] = tT[:, s * TILE:(s + 1) * TILE].astype(BF16)

    aqT_ref[0] = segnorm_t(grp(_C_AQ, BRANCH_WIDTH).T, gaq_ref[...]).astype(BF16)
    ak_ref[0] = segnorm(grp(_C_AK, BRANCH_WIDTH), gak_ref[...]).astype(BF16)
    avT = grp(_C_AV, BRANCH_WIDTH).T
    put_slabs(avT_ref, jnp.concatenate(
        [p for h in range(A_HEADS) for p in (avT[h * A_V_DIM:(h + 1) * A_V_DIM, :], ones_rows)], axis=0))
    bqT_ref[0] = segnorm_t(grp(_C_BQ, BRANCH_WIDTH).T, gbq_ref[...]).astype(BF16)
    cqT_ref[0] = segnorm_t(grp(_C_CQ, BRANCH_WIDTH).T, gcq_ref[...]).astype(BF16)
    bk_ref[0] = segnorm(grp(_C_BK, LANES), gbk_ref[...]).astype(BF16)
    bvT = grp(_C_BV, LANES).T
    put_slabs(bvT_ref, jnp.concatenate(
        [p for g in range(B_KV_HEADS) for p in (bvT[g * HEAD_DIM:(g + 1) * HEAD_DIM, :], ones_rows)], axis=0))
    iqT_ref[0] = grp(_C_IQ, LANES).T.astype(BF16)
    ckv = grp(_C_CKV, LANES)
    ck = ckv[:, :HEAD_DIM]
    ssq = jnp.sum(ck * ck, axis=-1, keepdims=True)
    ck_ref[0] = (ck * lax.rsqrt(ssq * (1.0 / HEAD_DIM) + EPS) * gck_ref[...]).astype(BF16)
    put_slabs(cvT_ref, jnp.concatenate([ckv.T[HEAD_DIM:, :], ones_rows], axis=0))
    ikw = grp(_C_IKW, LANES)
    ik_ref[0] = ikw[:, :IDX_DIM].astype(BF16)
    iwT_ref[0] = ikw.T[IDX_DIM:IDX_DIM + IDX_HEADS, :] * iw_scale


def _proj(x, g, w, seg, gains, tm):
    B, S, D = x.shape
    nt = S // TILE
    n_sub = tm // TILE
    full = lambda shape: pl.BlockSpec(shape, lambda b, t: (0,) * len(shape))
    out_shape = [
        jax.ShapeDtypeStruct((B, BRANCH_WIDTH, S), BF16),
        jax.ShapeDtypeStruct((B, S, BRANCH_WIDTH), BF16),
        jax.ShapeDtypeStruct((B, nt, A_HEADS * A_VT_ROWS, TILE), BF16),
        jax.ShapeDtypeStruct((B, BRANCH_WIDTH, S), BF16),
        jax.ShapeDtypeStruct((B, S, LANES), BF16),
        jax.ShapeDtypeStruct((B, nt, B_KV_HEADS * C_VT_ROWS, TILE), BF16),
        jax.ShapeDtypeStruct((B, BRANCH_WIDTH, S), BF16),
        jax.ShapeDtypeStruct((B, S, HEAD_DIM), BF16),
        jax.ShapeDtypeStruct((B, nt, C_VT_ROWS, TILE), BF16),
        jax.ShapeDtypeStruct((B, LANES, S), BF16),
        jax.ShapeDtypeStruct((B, S, IDX_DIM), BF16),
        jax.ShapeDtypeStruct((B, IDX_HEADS, S), F32),
    ]
    colT = lambda r: pl.BlockSpec((1, r, tm), lambda b, t: (b, 0, t))
    row = lambda c: pl.BlockSpec((1, tm, c), lambda b, t: (b, t, 0))
    slab = lambda r: pl.BlockSpec((1, n_sub, r, TILE), lambda b, t: (b, t, 0, 0))
    out_specs = [colT(BRANCH_WIDTH), row(BRANCH_WIDTH), slab(A_HEADS * A_VT_ROWS), colT(BRANCH_WIDTH),
                 row(LANES),
                 slab(B_KV_HEADS * C_VT_ROWS), colT(BRANCH_WIDTH), row(HEAD_DIM), slab(C_VT_ROWS), colT(LANES), row(IDX_DIM),
                 pl.BlockSpec((1, IDX_HEADS, tm), lambda b, t: (b, 0, t))]
    in_specs = [pl.BlockSpec((1, tm, D), lambda b, t: (b, t, 0)), full((1, D)), full((D, _W_COLS)),
                full((BRANCH_WIDTH, BRANCH_WIDTH))] + [full(gn.shape) for gn in gains]
    vmem = 2 * (tm * D * 4 + D * _W_COLS * 2 + BRANCH_WIDTH * BRANCH_WIDTH * 2 + tm * 3400 * 2) + 24 * tm * BRANCH_WIDTH * 4
    return pl.pallas_call(
        functools.partial(_proj_kernel, tm=tm, iw_scale=IDX_HEADS ** -0.5 * IDX_DIM ** -0.5),
        grid=(B, S // tm), in_specs=in_specs, out_specs=out_specs, out_shape=out_shape,
        compiler_params=pltpu.CompilerParams(dimension_semantics=("arbitrary", "arbitrary"),
                                             vmem_limit_bytes=_vmem_limit(vmem)),
        name="proj",
    )(x, g, w, seg, *gains)


def _online_step(s, m_old, vt, acc_ref, ch):
    if s.dtype == BF16:
        m_new = jnp.maximum(m_old, jnp.max(_colmax16(s).astype(F32), axis=0, keepdims=True))
        e = jnp.exp2(s - m_new.astype(BF16))
    else:
        m_new = jnp.maximum(m_old, jnp.max(_colmax8(s), axis=0, keepdims=True))
        e = jnp.exp2(s - m_new).astype(BF16)
    alpha = jnp.exp2(m_old - m_new)
    acc_ref[ch] = acc_ref[ch] * alpha + _dot(vt, e)
    return m_new


def _sweep_key_tiles(i, n_chain, scores, vt, acc_ref):
    n_far = jnp.maximum(i - (NEAR_TILES - 1), 0)

    def step(tiles, ms):
        ms = list(ms)
        chains = [(j, d, ch) for (j, d) in tiles for ch in range(n_chain)]
        pending = [scores(*c) for c in chains[:SKEW]]
        for n, (j, d, ch) in enumerate(chains):
            s = pending.pop(0)
            if n + SKEW < len(chains):
                pending.append(scores(*chains[n + SKEW]))
            ms[ch] = _online_step(s, ms[ch], vt(j, ch), acc_ref, ch)
        return tuple(ms)

    def near_pairs(ms):
        ms = step([(i - 3, 3), (i - 2, 2)], ms)
        return step([(i - 1, 1), (i, 0)], ms)

    def near_singles(ms):
        return lax.fori_loop(0, i + 1, lambda t, ms: step([(t, i - t)], ms), ms)

    assert NEAR_TILES == 4
    ms = tuple(jnp.full((1, TILE), -jnp.inf, F32) for _ in range(n_chain))
    ms = lax.fori_loop(0, n_far // 4, lambda p, ms: step([(4 * p + u, None) for u in range(4)], ms), ms)
    rest = n_far - n_far % 4
    ms = lax.cond(n_far % 4 >= 2, lambda ms: step([(rest, None), (rest + 1, None)], ms), lambda ms: ms, ms)
    ms = lax.cond(n_far % 2 == 1, lambda ms: step([(n_far - 1, None)], ms), lambda ms: ms, ms)
    lax.cond(i >= NEAR_TILES - 1, near_pairs, near_singles, ms)


def _attn_a_kernel(qT_ref, k_ref, vT_ref, bias_ref, lam_ref, sub_ref, o_ref, q2_ref, acc_ref, *, lambda_init):
    i = pl.program_id(1)
    lp = lam_ref[...]
    lam = (jnp.exp(jnp.sum(lp[0:1] * lp[1:2], axis=-1, keepdims=True))
           - jnp.exp(jnp.sum(lp[2:3] * lp[3:4], axis=-1, keepdims=True)) + lambda_init)
    row = lax.broadcasted_iota(I32, (2 * HEAD_DIM, TILE), 0)
    n_chain = 2 * A_HEADS

    for h in range(A_HEADS):
        qh = qT_ref[0, h * A_V_DIM:(h + 1) * A_V_DIM, :]
        zero = jnp.zeros_like(qh)
        q2_ref[2 * h] = jnp.where(row < HEAD_DIM, qh, zero)
        q2_ref[2 * h + 1] = jnp.where(row >= HEAD_DIM, qh, zero)
    acc_ref[...] = jnp.zeros_like(acc_ref)

    def scores(j, d, ch):
        h = ch // 2
        rows = pl.ds(pl.multiple_of(j * TILE, TILE), TILE)
        s = _dot(k_ref[0, rows, h * A_V_DIM:(h + 1) * A_V_DIM], q2_ref[ch]).astype(BF16)
        return s if d is None else s + bias_ref[h, d]

    def vt(j, ch):
        h = ch // 2
        return vT_ref[0, j, h * A_VT_ROWS:(h + 1) * A_VT_ROWS, :]

    _sweep_key_tiles(i, n_chain, scores, vt, acc_ref)

    for h in range(A_HEADS):
        a0 = acc_ref[2 * h]
        a1 = acc_ref[2 * h + 1]
        r0 = 1.0 / a0[A_V_DIM:A_V_DIM + 1, :]
        r1 = 1.0 / a1[A_V_DIM:A_V_DIM + 1, :]
        outT = a0[:A_V_DIM, :] * r0 - lam * (a1[:A_V_DIM, :] * r1)
        out = _rms(outT.T, sub_ref[...]) * (1.0 - lambda_init)
        o_ref[0, :, h * A_V_DIM:(h + 1) * A_V_DIM] = out.astype(BF16)


def _attn_b_kernel(sink_ref, qT_ref, k_ref, vT_ref, bias_ref, o_ref, oT_ref):
    i = pl.program_id(1)
    jp = jnp.maximum(i - 1, 0)
    p_idx = jnp.where(i > 0, 1, 2)
    cur = pl.ds(pl.multiple_of(i * TILE, TILE), TILE)
    prev = pl.ds(pl.multiple_of(jp * TILE, TILE), TILE)

    def scores(h):
        qh = qT_ref[0, h * HEAD_DIM:(h + 1) * HEAD_DIM, :]
        zero = jnp.zeros_like(qh)
        q2 = jnp.concatenate([qh, zero] if h < B_GROUP else [zero, qh], axis=0)
        return (_dot(k_ref[0, cur, :], q2).astype(BF16) + bias_ref[h, 0],
                _dot(k_ref[0, prev, :], q2).astype(BF16) + bias_ref[h, p_idx])

    def finish(h, s):
        sc, sp = s
        gs = slice((h // B_GROUP) * C_VT_ROWS, (h // B_GROUP + 1) * C_VT_ROWS)
        sink = sink_ref[h] * LOG2E
        m = jnp.max(jnp.maximum(_colmax16(sc), _colmax16(sp)).astype(F32), axis=0, keepdims=True)
        m = jnp.maximum(m, sink).astype(BF16)
        ec = jnp.exp2(sc - m)
        ep = jnp.exp2(sp - m)
        outT = _dot(vT_ref[0, i, gs, :], ec) + _dot(vT_ref[0, jp, gs, :], ep)
        den = outT[HEAD_DIM:HEAD_DIM + 1, :] + jnp.exp2(sink - m.astype(F32))
        oT_ref[h * HEAD_DIM:(h + 1) * HEAD_DIM, :] = outT[:HEAD_DIM, :] * (1.0 / den)

    pending = [scores(h) for h in range(SKEW)]
    for h in range(B_HEADS):
        s = pending.pop(0)
        if h + SKEW < B_HEADS:
            pending.append(scores(h + SKEW))
        finish(h, s)
    o_ref[0] = oT_ref[...].T.astype(BF16)


def _attn_c_kernel(qT_ref, k_ref, vT_ref, iqT_ref, ik_ref, iwT_ref, bias_ref, o_ref,
                   hi_ref, lo_ref, mb_ref, acc_ref, oT_ref, *, top_k):
    i = pl.program_id(1)
    n_t = i + 1
    krow = lax.broadcasted_iota(I32, (TILE, TILE), 0)
    qcol = lax.broadcasted_iota(I32, (TILE, TILE), 1)
    allowed = (krow // CHUNK) <= (qcol // CHUNK)

    def idx_keys(j, diag):
        ikt = ik_ref[0, pl.ds(pl.multiple_of(j * TILE, TILE), TILE), :]
        lgs = [_dot(ikt, iqT_ref[0, hh * IDX_DIM:(hh + 1) * IDX_DIM, :]) for hh in range(IDX_HEADS)]
        sc = jnp.zeros((TILE, TILE), F32)
        for hh in range(IDX_HEADS):
            sc = sc + jnp.maximum(lgs[hh], 0.0) * iwT_ref[0, hh:hh + 1, :]
        if diag:
            sc = jnp.where(allowed, sc, NEG)
        bits = lax.bitcast_convert_type(sc, I32)
        hi_b = (bits >> 16).astype(I16)
        lo_b = bits.astype(I16)
        sign = jnp.where(hi_b < 0, jnp.int16(-1), jnp.int16(0))
        hi_ref[j] = hi_b ^ (sign & 0x7FFF)
        lo_ref[j] = lo_b ^ sign ^ I16_MIN

    def fill_pair(p, carry):
        idx_keys(2 * p, False)
        idx_keys(2 * p + 1, False)
        return carry

    lax.fori_loop(0, i // 2, fill_pair, 0)

    @pl.when(i % 2 == 1)
    def _():
        idx_keys(i - 1, False)

    idx_keys(i, True)

    n_pair = (n_t + 1) // 2

    @pl.when(n_t % 2 == 1)
    def _():
        hi_ref[n_t] = jnp.full((TILE, TILE), I16_MIN, I16)
        lo_ref[n_t] = jnp.full((TILE, TILE), I16_MIN, I16)

    groups = TILE // BF16_ROWS

    def rows16(ref, j):
        return ref[j].reshape(groups, BF16_ROWS, TILE)

    def bcast16(v):
        return jnp.broadcast_to(v, (BF16_ROWS, TILE)).astype(I16)

    def count(pred):
        def body(p, cs):
            cs = list(cs)
            for u in (0, 1):
                j = 2 * p + u
                hi, lo = rows16(hi_ref, j), rows16(lo_ref, j)
                for r in range(groups):
                    cs[r % len(cs)] = cs[r % len(cs)] + pred(hi[r], lo[r], j, r).astype(I16)
            return tuple(cs)
        cs = lax.fori_loop(0, n_pair, body, (jnp.zeros((BF16_ROWS, TILE), I16),) * 4)
        c = (cs[0] + cs[1]) + (cs[2] + cs[3])
        return jnp.sum(c.astype(I32), axis=0, keepdims=True)

    def search(n_bits, accept):
        def step(b, t):
            cand = t + lax.shift_left(jnp.int32(1), n_bits - 1 - b)
            return jnp.where(accept(cand), cand, t)
        return lax.fori_loop(0, n_bits, step, jnp.full((1, TILE), I16_MIN, I32))

    def hi_accept(cand):
        c16 = bcast16(cand)
        return count(lambda hi, lo, j, r: hi >= c16) >= top_k

    t_hi = search(16, hi_accept)
    th16 = bcast16(t_hi)
    cnt_above = count(lambda hi, lo, j, r: hi > th16)
    r_lo = top_k - cnt_above

    def bucket_only(p, carry):
        for u in (0, 1):
            j = 2 * p + u
            hi, lo = rows16(hi_ref, j), rows16(lo_ref, j)
            for r in range(groups):
                lo_ref[j, r * BF16_ROWS:(r + 1) * BF16_ROWS, :] = jnp.where(hi[r] == th16, lo[r], I16_MIN)
        return carry

    lax.fori_loop(0, n_pair, bucket_only, 0)

    def lo_accept(cand):
        c16 = bcast16(cand)
        return count(lambda hi, lo, j, r: lo >= c16) >= r_lo

    t_lo = search(16, lo_accept)
    tl16 = bcast16(t_lo)
    cnt_gt = count(lambda hi, lo, j, r: lo > tl16)
    r_eq = (r_lo - cnt_gt).astype(F32)

    tri = (krow >= qcol).astype(BF16)
    one, zero, neg16 = jnp.ones((), BF16), jnp.zeros((), BF16), jnp.full((), NEG, BF16)

    def mask_tile(j, ties_before, diag):
        hi, lo = rows16(hi_ref, j), rows16(lo_ref, j)
        eqs = [(lo[r] == tl16) & (hi[r] == th16) for r in range(groups)]
        ties = _dot(tri, jnp.concatenate([jnp.where(e, one, zero) for e in eqs], axis=0)) + ties_before
        over = (ties - r_eq).astype(BF16)
        rows = []
        for r in range(groups):
            kept_tie = eqs[r] & (over[r * BF16_ROWS:(r + 1) * BF16_ROWS, :] <= zero)
            rows.append(jnp.where((hi[r] > th16) | (lo[r] > tl16) | kept_tie, zero, neg16))
        mb = jnp.concatenate(rows, axis=0)
        if diag:
            mb = jnp.where(allowed, mb.astype(F32), NEG).astype(BF16)
        mb_ref[j] = mb
        return ties[TILE - 1:TILE, :]

    ties_before = lax.fori_loop(0, i // 2, lambda p, c: mask_tile(2 * p + 1, mask_tile(2 * p, c, False), False),
                                jnp.zeros((1, TILE), F32))
    ties_before = lax.cond(i % 2 == 1, lambda c: mask_tile(i - 1, c, False), lambda c: c, ties_before)
    mask_tile(i, ties_before, True)

    acc_ref[...] = jnp.zeros_like(acc_ref)

    def scores(j, d, h):
        kt = k_ref[0, pl.ds(pl.multiple_of(j * TILE, TILE), TILE), :]
        s = _dot(kt, qT_ref[0, h * HEAD_DIM:(h + 1) * HEAD_DIM, :]).astype(BF16) + mb_ref[j]
        return s if d is None else s + bias_ref[h, d]

    _sweep_key_tiles(i, C_HEADS, scores, lambda j, h: vT_ref[0, j], acc_ref)

    for h in range(C_HEADS):
        a = acc_ref[h]
        oT_ref[h * HEAD_DIM:(h + 1) * HEAD_DIM, :] = a[:HEAD_DIM, :] * (1.0 / a[HEAD_DIM:HEAD_DIM + 1, :])
    o_ref[0] = oT_ref[...].T.astype(BF16)


def _attn_kernel(*refs, lambda_init, top_k):
    a_in, b_in, c_in = refs[0:6], refs[6:11], refs[11:18]
    oa_ref, ob_ref, oc_ref = refs[18:21]
    a_scratch, b_scratch, c_scratch = refs[21:23], refs[23:24], refs[24:29]
    _attn_a_kernel(*a_in, oa_ref, *a_scratch, lambda_init=lambda_init)
    _attn_b_kernel(*b_in, ob_ref, *b_scratch)
    _attn_c_kernel(*c_in, oc_ref, *c_scratch, top_k=top_k)


def _attn(aqT, ak, avT, bias_a, lam_par, subln_g, sinks, bqT, bk, bvT, bias_b, cqT, ck, cvT, iqT, ik, iwT, bias_c,
          lambda_init, top_k):
    B, _, S = aqT.shape
    nt = S // TILE
    const = lambda a: pl.BlockSpec(a.shape, lambda b, i: (0,) * a.ndim, pipeline_mode=pl.Buffered(1))
    qtile = lambda rows: pl.BlockSpec((1, rows, TILE), lambda b, i: (b, 0, i))
    per_batch = lambda a: pl.BlockSpec((1,) + a.shape[1:], lambda b, i: (b,) + (0,) * (a.ndim - 1))
    in_specs = [qtile(BRANCH_WIDTH), per_batch(ak), per_batch(avT), const(bias_a), const(lam_par), const(subln_g),
                pl.BlockSpec(memory_space=pltpu.SMEM), qtile(BRANCH_WIDTH), per_batch(bk), per_batch(bvT),
                const(bias_b),
                qtile(BRANCH_WIDTH), per_batch(ck), per_batch(cvT), qtile(LANES), per_batch(ik), qtile(IDX_HEADS),
                const(bias_c)]
    out_spec = pl.BlockSpec((1, TILE, BRANCH_WIDTH), lambda b, i: (b, i, 0))
    out_shape = jax.ShapeDtypeStruct((B, S, BRANCH_WIDTH), BF16)
    scratch = [pltpu.VMEM((2 * A_HEADS, A_V_DIM, TILE), BF16),
               pltpu.VMEM((2 * A_HEADS, A_VT_ROWS, TILE), F32),
               pltpu.VMEM((BRANCH_WIDTH, TILE), F32),
               pltpu.VMEM((nt + nt % 2, TILE, TILE), I16),
               pltpu.VMEM((nt + nt % 2, TILE, TILE), I16),
               pltpu.VMEM((nt, TILE, TILE), BF16),
               pltpu.VMEM((C_HEADS, C_VT_ROWS, TILE), F32),
               pltpu.VMEM((BRANCH_WIDTH, TILE), F32)]
    per_batch_bytes = sum(a.size // B * a.dtype.itemsize for a in (ak, avT, bk, bvT, ck, cvT, ik))
    const_bytes = sum(a.size * a.dtype.itemsize for a in (bias_a, bias_b, bias_c))
    scratch_bytes = (2 * A_HEADS * (A_V_DIM * 2 + A_VT_ROWS * 4) + 2 * BRANCH_WIDTH * 4 + C_HEADS * C_VT_ROWS * 4) * TILE \
        + (2 * (nt + nt % 2) * 2 + nt * 2) * TILE * TILE
    vmem = 2 * per_batch_bytes + const_bytes + scratch_bytes + 12 * TILE * BRANCH_WIDTH * 2 + 64 * TILE * TILE * 4
    return pl.pallas_call(
        functools.partial(_attn_kernel, lambda_init=lambda_init, top_k=top_k),
        grid=(B, nt), in_specs=in_specs, out_specs=[out_spec] * 3, out_shape=[out_shape] * 3,
        scratch_shapes=scratch,
        compiler_params=pltpu.CompilerParams(dimension_semantics=("arbitrary", "arbitrary"),
                                             vmem_limit_bytes=_vmem_limit(vmem)),
        name="attn",
    )(aqT, ak, avT, bias_a, lam_par, subln_g, sinks, bqT, bk, bvT, bias_b, cqT, ck, cvT, iqT, ik, iwT, bias_c)


def _merge_kernel(x_ref, oa_ref, ob_ref, oc_ref, gmix_ref, wg_ref, bg_ref, wb_ref, wo_ref, gffn_ref,
                  wrh_ref, wrl_ref, br_ref, x1_ref, h2_ref, gate_ref):
    x = x_ref[...]
    hb = _rms(x, gmix_ref[...]).astype(BF16)
    z = None
    for n, o_ref in enumerate((oa_ref, ob_ref, oc_ref)):
        cs = slice(n * D_MODEL, (n + 1) * D_MODEL)
        gate = _sigmoid(_dot(hb, wg_ref[:, cs]) + bg_ref[:, cs])
        y = _dot(o_ref[...], wb_ref[n])
        z = gate * y if z is None else z + gate * y
    x1 = x + _dot(z.astype(BF16), wo_ref[...])
    x1_ref[...] = x1
    h2 = _rms(x1, gffn_ref[...])
    h2_ref[...] = h2.astype(BF16)

    hi, lo = _split_bf16(h2)
    lg = _dot(hi, wrh_ref[...]) + _dot(lo, wrh_ref[...]) + _dot(hi, wrl_ref[...]) + br_ref[...]
    col = lax.broadcasted_iota(I32, lg.shape, 1).astype(F32)
    big = float(4 * LANES)
    is_g = (col >= N_EXPERTS) & (col < N_EXPERTS + N_GROUPS)
    gl = jnp.where(is_g, lg, -jnp.inf)
    gmax = jnp.max(gl, axis=-1, keepdims=True)
    p_group = 1.0 / jnp.sum(jnp.exp(gl - gmax), axis=-1, keepdims=True)
    g_sel = jnp.min(jnp.where(gl == gmax, col, big), axis=-1, keepdims=True) - N_EXPERTS
    in_g = (col >= g_sel * EXPERTS_PER_GROUP) & (col < (g_sel + 1) * EXPERTS_PER_GROUP)
    el = jnp.where(in_g, lg, -jnp.inf)
    e1 = jnp.max(el, axis=-1, keepdims=True)
    i1 = jnp.min(jnp.where(el == e1, col, big), axis=-1, keepdims=True)
    el2 = jnp.where(col == i1, -jnp.inf, el)
    e2 = jnp.max(el2, axis=-1, keepdims=True)
    i2 = jnp.min(jnp.where(el2 == e2, col, big), axis=-1, keepdims=True)
    t2 = jnp.exp(e2 - e1)
    w1 = p_group / (1.0 + t2)
    w2 = w1 * t2
    gates = jnp.where(col == i1, w1, 0.0) + jnp.where(col == i2, w2, 0.0)
    gate_ref[...] = gates[:, :N_EXPERTS]


def _merge(x2, oa, ob, oc, gmix, wg, bg, wb, wo, gffn, wrh, wrl, br, tm):
    T, D = x2.shape
    full = lambda a: pl.BlockSpec(a.shape, lambda t: (0,) * a.ndim, pipeline_mode=pl.Buffered(1))
    rowb = lambda c: pl.BlockSpec((tm, c), lambda t: (t, 0))
    vmem = 2 * (tm * D * 4 * 2 + 3 * tm * BRANCH_WIDTH * 2 + tm * D * 2) + wg.size * 2 + wb.size * 2 + wo.size * 2 \
        + 2 * D * LANES * 2 + 10 * tm * D * 4
    return pl.pallas_call(
        _merge_kernel,
        grid=(T // tm,),
        in_specs=[rowb(D), rowb(BRANCH_WIDTH), rowb(BRANCH_WIDTH), rowb(BRANCH_WIDTH), full(gmix), full(wg), full(bg), full(wb),
                  full(wo), full(gffn), full(wrh), full(wrl), full(br)],
        out_specs=[rowb(D), rowb(D), rowb(N_EXPERTS)],
        out_shape=[jax.ShapeDtypeStruct((T, D), F32), jax.ShapeDtypeStruct((T, D), BF16),
                   jax.ShapeDtypeStruct((T, N_EXPERTS), F32)],
        compiler_params=pltpu.CompilerParams(dimension_semantics=("arbitrary",),
                                             vmem_limit_bytes=_vmem_limit(vmem)),
        name="merge",
    )(x2, oa, ob, oc, gmix, wg, bg, wb, wo, gffn, wrh, wrl, br)


def _moe_kernel(x1_ref, h2_ref, gate_ref, wgu_ref, wd_ref, o_ref):
    g = pl.program_id(1)
    h2 = h2_ref[...]
    gate = gate_ref[...]
    lane = lax.broadcasted_iota(I32, gate.shape, 1)
    hids = []
    for u in range(MOE_STEP_EXPERTS):
        gu = _dot(h2, wgu_ref[u])
        a = gu[:, :EXPERT_FF]
        w = jnp.sum(jnp.where(lane == g * MOE_STEP_EXPERTS + u, gate, 0.0), axis=-1, keepdims=True)
        hids.append((a * _sigmoid(a) * gu[:, EXPERT_FF:] * w).astype(BF16))
    y = _dot(jnp.concatenate(hids, axis=1), wd_ref[...])

    @pl.when(g == 0)
    def _():
        o_ref[...] = x1_ref[...] + y

    @pl.when(g > 0)
    def _():
        o_ref[...] += y


def _moe(x1, h2, gate, wgu, wd, tm):
    T, D = x1.shape
    n_e, ff = MOE_STEP_EXPERTS, EXPERT_FF
    vmem = 2 * (tm * D * 4 * 2 + tm * D * 2 + tm * LANES * 4 + n_e * D * 2 * ff * 2 + n_e * ff * D * 2) \
        + 3 * n_e * tm * 2 * ff * 4 + 2 * tm * D * 4
    return pl.pallas_call(
        _moe_kernel,
        grid=(T // tm, N_EXPERTS // n_e),
        in_specs=[pl.BlockSpec((tm, D), lambda t, g: (t, 0)),
                  pl.BlockSpec((tm, D), lambda t, g: (t, 0)),
                  pl.BlockSpec((tm, N_EXPERTS), lambda t, g: (t, 0)),
                  pl.BlockSpec((n_e, D, 2 * ff), lambda t, g: (g, 0, 0)),
                  pl.BlockSpec((n_e * ff, D), lambda t, g: (g, 0))],
        out_specs=pl.BlockSpec((tm, D), lambda t, g: (t, 0)),
        out_shape=jax.ShapeDtypeStruct((T, D), F32),
        compiler_params=pltpu.CompilerParams(dimension_semantics=("arbitrary", "arbitrary"),
                                             vmem_limit_bytes=_vmem_limit(vmem)),
        name="moe",
    )(x1, h2, gate, wgu, wd)


def _t5_bucket_np(rel):
    half = NUM_BUCKETS // 2
    max_exact = half // 2
    n = np.abs(rel)
    n_f = np.maximum(n, 1).astype(np.float32)
    large = max_exact + (np.log(n_f / np.float32(max_exact)) / np.float32(math.log(MAX_DISTANCE / max_exact))
                         * np.float32(half - max_exact)).astype(np.int32)
    large = np.minimum(large, half - 1)
    return np.where(rel > 0, half, 0) + np.where(n < max_exact, n, large)


def _toeplitz_kernel(v_ref, o_ref):
    v = v_ref[0]
    for d in range(NEAR_TILES):
        x = jnp.broadcast_to(v[d:d + 1, :], (TILE, 2 * TILE))
        o_ref[0, d] = pltpu.roll(x, 0, 1, stride=1, stride_axis=0)[:, :TILE]


def _toeplitz(vals):
    n_heads = vals.shape[0]
    return pl.pallas_call(
        _toeplitz_kernel,
        grid=(n_heads,),
        in_specs=[pl.BlockSpec((1, NEAR_TILES, 2 * TILE), lambda h: (h, 0, 0))],
        out_specs=pl.BlockSpec((1, NEAR_TILES, TILE, TILE), lambda h: (h, 0, 0, 0)),
        out_shape=jax.ShapeDtypeStruct((n_heads, NEAR_TILES, TILE, TILE), F32),
        name="bias_tiles",
    )(vals)


def _bias_tables(rel_bias):
    u = np.arange(2 * TILE)
    off = np.where(u < TILE, u, u - 2 * TILE)
    rel = np.stack([-off - TILE * d for d in range(NEAR_TILES)])
    onehot = (_t5_bucket_np(rel)[..., None] == np.arange(NUM_BUCKETS)).astype(np.float32)
    vals = jnp.einsum("dub,bh->hdu", jnp.asarray(onehot), rel_bias, precision=lax.Precision.HIGHEST)
    tab = _toeplitz(vals)

    far = rel_bias[NUM_BUCKETS // 2 - 1]
    kk = np.arange(TILE)[:, None]
    qc = np.arange(TILE)[None, :] // CHUNK
    kc = np.stack([(kk // CHUNK) - (TILE // CHUNK) * d for d in range(NEAR_TILES)]) + 0 * qc
    a_ok = jnp.asarray(kc <= qc)
    ta = jnp.where(a_ok[None], (tab[:A_HEADS] - far[:A_HEADS, None, None, None]) * LOG2E, NEG)
    b_ok = jnp.asarray((qc - kc[:2] >= 0) & (qc - kc[:2] <= W_CHUNKS))
    tb = jnp.where(b_ok[None], tab[A_HEADS:A_HEADS + B_HEADS, :2] * LOG2E, NEG)
    tb = jnp.concatenate([tb, jnp.full((B_HEADS, 1, TILE, TILE), NEG, F32)], axis=1)
    tc = (tab[A_HEADS + B_HEADS:] - far[A_HEADS + B_HEADS:, None, None, None]) * LOG2E
    return ta.astype(BF16), tb.astype(BF16), tc.astype(BF16)


def _proj_weight_kernel(w_ref, o_ref):
    o_ref[...] = jnp.zeros_like(o_ref)
    o_ref[:, :_IN_COLS] = w_ref[0].astype(BF16)


def _proj_weight(w_in, l):
    _, d, n = w_in.shape
    assert n == _IN_COLS
    rows = 128
    return pl.pallas_call(
        _proj_weight_kernel,
        grid=(d // rows,),
        in_specs=[pl.BlockSpec((1, rows, n), lambda r: (l, r, 0))],
        out_specs=pl.BlockSpec((rows, _W_COLS), lambda r: (r, 0)),
        out_shape=jax.ShapeDtypeStruct((d, _W_COLS), BF16),
        name="proj_weight",
    )(w_in)


def kernel(x, rel_bias, norm_mix_g, w_in, qk_norm_g, diff_lambda, diff_subln_g, sinks, w_branch, w_gate, b_gate,
           w_out, norm_ffn_g, w_router_group, b_router_group, w_router_expert, b_router_expert, w_ff_gate,
           w_ff_up, w_ff_down):
    B, S, D = x.shape
    assert D == D_MODEL and S % TILE == 0
    T = B * S
    top_k = min(TOPK_MAX, S // 4)
    tm_proj = 1024 if S % 1024 == 0 else TILE
    tm_merge = 1024 if T % 1024 == 0 else TILE
    tm_moe = 1024 if T % 1024 == 0 else TILE

    bias_a, bias_b, bias_c = _bias_tables(rel_bias)
    seg = jnp.asarray(np.kron(np.eye(BRANCH_WIDTH // HEAD_DIM), np.ones((HEAD_DIM, HEAD_DIM))), BF16)
    q_scale = HEAD_DIM ** -0.5

    for l in range(DEPTH):
        lambda_init = 0.8 - 0.6 * math.exp(-0.3 * l)
        qg = qk_norm_g[l]
        tile8 = lambda g: jnp.tile(g, BRANCH_WIDTH // HEAD_DIM)
        gains = ((tile8(qg[0, 0]) * (q_scale * LOG2E))[:, None], tile8(qg[0, 1])[None, :],
                 (tile8(qg[1, 0]) * (q_scale * LOG2E))[:, None], jnp.tile(qg[1, 1], B_KV_HEADS)[None, :],
                 (tile8(qg[2, 0]) * (q_scale * LOG2E))[:, None], qg[2, 1][None, :])
        (aqT, ak, avT, bqT, bk, bvT, cqT, ck, cvT, iqT, ik, iwT) = _proj(
            x, norm_mix_g[l][None, :], _proj_weight(w_in, l), seg, gains, tm_proj)

        oa, ob, oc = _attn(aqT, ak, avT, bias_a, diff_lambda[l], diff_subln_g[l][None, :], sinks[l], bqT, bk, bvT,
                           bias_b, cqT, ck, cvT, iqT, ik, iwT, bias_c, lambda_init, top_k)

        w_r = jnp.concatenate([w_router_expert[l], w_router_group[l],
                               jnp.zeros((D, LANES - N_EXPERTS - N_GROUPS), F32)], axis=1)
        b_r = jnp.concatenate([b_router_expert[l], b_router_group[l],
                               jnp.zeros((LANES - N_EXPERTS - N_GROUPS,), F32)])[None, :]
        wrh = w_r.astype(BF16)
        wrl = (w_r - wrh.astype(F32)).astype(BF16)
        x1, h2, gate = _merge(
            x.reshape(T, D), oa.reshape(T, BRANCH_WIDTH), ob.reshape(T, BRANCH_WIDTH), oc.reshape(T, BRANCH_WIDTH),
            norm_mix_g[l][None, :], w_gate[l].astype(BF16), b_gate[l][None, :], w_branch[l].astype(BF16),
            w_out[l].astype(BF16), norm_ffn_g[l][None, :], wrh, wrl, b_r, tm_merge)

        wgu = jnp.concatenate([w_ff_gate[l], w_ff_up[l]], axis=-1).astype(BF16)
        wd = w_ff_down[l].astype(BF16).reshape(N_EXPERTS * EXPERT_FF, D)
        x = _moe(x1, h2, gate, wgu, wd, tm_moe).reshape(B, S, D)
    return x
```

```python
import functools
import math

import numpy as np
import jax
import jax.numpy as jnp
from jax import lax
from jax.experimental import pallas as pl
from jax.experimental.pallas import tpu as pltpu

F32 = jnp.float32
BF16 = jnp.bfloat16
I32 = jnp.int32
I16 = jnp.int16

D_MODEL = 1024
DEPTH = 2
CHUNK = 64
HEAD_DIM = 64
A_HEADS = 4
A_V_DIM = 2 * HEAD_DIM
B_HEADS = 8
B_KV_HEADS = 2
B_GROUP = B_HEADS // B_KV_HEADS
W_CHUNKS = 2
C_HEADS = 8
IDX_HEADS = 4
IDX_DIM = 32
TOPK_MAX = 256
NUM_BUCKETS = 32
MAX_DISTANCE = 1024
BRANCH_WIDTH = 512
N_GROUPS = 4
EXPERTS_PER_GROUP = 4
N_EXPERTS = N_GROUPS * EXPERTS_PER_GROUP
EXPERT_FF = 256
MOE_STEP_EXPERTS = 16
EPS = 1e-6
NEG = -1e30
I16_MIN = -(2 ** 15)

LANES = 128
SUBLANES = 8
TILE = 256
NEAR_TILES = 4
BF16_ROWS = 16
A_VT_ROWS = A_V_DIM + BF16_ROWS
C_VT_ROWS = HEAD_DIM + BF16_ROWS
LOG2E = 1.4426950408889634
SKEW = 7
VMEM_CAP = 60000 * 1024

_C_AQ, _C_AK, _C_AV, _C_BQ, _C_BK, _C_BV, _C_CQ = 0, 512, 1024, 1536, 2048, 2176, 2304
_C_CKV, _C_IQ, _C_IKW = 2816, 2944, 3072
_IN_COLS = 3108
_W_COLS = 3200


def _dot(a, b):
    return jnp.dot(a, b, preferred_element_type=F32)


def _split_bf16(a):
    hi = a.astype(BF16)
    lo = (a - hi.astype(F32)).astype(BF16)
    return hi, lo


def _sigmoid(x):
    return 1.0 / (1.0 + jnp.exp(-x))


def _rms(x, g):
    return x * lax.rsqrt(jnp.mean(x * x, axis=-1, keepdims=True) + EPS) * g


def _colmax8(s):
    r, c = s.shape
    return jnp.max(s.reshape(r // SUBLANES, SUBLANES, c), axis=0)


def _colmax16(s):
    r, c = s.shape
    return jnp.max(s.reshape(r // BF16_ROWS, BF16_ROWS, c), axis=0)


def _vmem_limit(nbytes):
    return int(min(VMEM_CAP, nbytes))


def _proj_kernel(x_ref, g_ref, w_ref, seg_ref, gaq_ref, gak_ref, gbq_ref, gbk_ref, gcq_ref, gck_ref,
                 aqT_ref, ak_ref, avT_ref, bqT_ref, bk_ref, bvT_ref, cqT_ref, ck_ref, cvT_ref,
                 iqT_ref, ik_ref, iwT_ref, *, tm, iw_scale):
    hb = _rms(x_ref[0], g_ref[...]).astype(BF16)
    seg = seg_ref[...]
    n_sub = tm // TILE

    def grp(a, n):
        return _dot(hb, w_ref[:, a:a + n])

    def segnorm(t, g):
        n = t.shape[1]
        ssq = _dot((t * t).astype(BF16), seg[:n, :n])
        return t * lax.rsqrt(ssq * (1.0 / HEAD_DIM) + EPS) * g

    def segnorm_t(t, gcol):
        n = t.shape[0] // HEAD_DIM
        t3 = t.reshape(n, HEAD_DIM, tm)
        ssq = jnp.sum(t3 * t3, axis=1, keepdims=True)
        return (t3 * lax.rsqrt(ssq * (1.0 / HEAD_DIM) + EPS)).reshape(t.shape) * gcol

    ones_rows = (lax.broadcasted_iota(I32, (BF16_ROWS, tm), 0) == 0).astype(F32)

    def put_slabs(ref, tT):
        for s in range(n_sub):
            ref[0, s] = tT[:, s * TILE:(s + 1) * TILE].astype(BF16)

    aqT_ref[0] = segnorm_t(grp(_C_AQ, BRANCH_WIDTH).T, gaq_ref[...]).astype(BF16)
    ak_ref[0] = segnorm(grp(_C_AK, BRANCH_WIDTH), gak_ref[...]).astype(BF16)
    avT = grp(_C_AV, BRANCH_WIDTH).T
    put_slabs(avT_ref, jnp.concatenate(
        [p for h in range(A_HEADS) for p in (avT[h * A_V_DIM:(h + 1) * A_V_DIM, :], ones_rows)], axis=0))
    bqT_ref[0] = segnorm_t(grp(_C_BQ, BRANCH_WIDTH).T, gbq_ref[...]).astype(BF16)
    cqT_ref[0] = segnorm_t(grp(_C_CQ, BRANCH_WIDTH).T, gcq_ref[...]).astype(BF16)
    bk_ref[0] = segnorm(grp(_C_BK, LANES), gbk_ref[...]).astype(BF16)
    bvT = grp(_C_BV, LANES).T
    put_slabs(bvT_ref, jnp.concatenate(
        [p for g in range(B_KV_HEADS) for p in (bvT[g * HEAD_DIM:(g + 1) * HEAD_DIM, :], ones_rows)], axis=0))
    iqT_ref[0] = grp(_C_IQ, LANES).T.astype(BF16)
    ckv = grp(_C_CKV, LANES)
    ck = ckv[:, :HEAD_DIM]
    ssq = jnp.sum(ck * ck, axis=-1, keepdims=True)
    ck_ref[0] = (ck * lax.rsqrt(ssq * (1.0 / HEAD_DIM) + EPS) * gck_ref[...]).astype(BF16)
    put_slabs(cvT_ref, jnp.concatenate([ckv.T[HEAD_DIM:, :], ones_rows], axis=0))
    ikw = grp(_C_IKW, LANES)
    ik_ref[0] = ikw[:, :IDX_DIM].astype(BF16)
    iwT_ref[0] = ikw.T[IDX_DIM:IDX_DIM + IDX_HEADS, :] * iw_scale


def _proj(x, g, w, seg, gains, tm):
    B, S, D = x.shape
    nt = S // TILE
    n_sub = tm // TILE
    full = lambda shape: pl.BlockSpec(shape, lambda b, t: (0,) * len(shape))
    out_shape = [
        jax.ShapeDtypeStruct((B, BRANCH_WIDTH, S), BF16),
        jax.ShapeDtypeStruct((B, S, BRANCH_WIDTH), BF16),
        jax.ShapeDtypeStruct((B, nt, A_HEADS * A_VT_ROWS, TILE), BF16),
        jax.ShapeDtypeStruct((B, BRANCH_WIDTH, S), BF16),
        jax.ShapeDtypeStruct((B, S, LANES), BF16),
        jax.ShapeDtypeStruct((B, nt, B_KV_HEADS * C_VT_ROWS, TILE), BF16),
        jax.ShapeDtypeStruct((B, BRANCH_WIDTH, S), BF16),
        jax.ShapeDtypeStruct((B, S, HEAD_DIM), BF16),
        jax.ShapeDtypeStruct((B, nt, C_VT_ROWS, TILE), BF16),
        jax.ShapeDtypeStruct((B, LANES, S), BF16),
        jax.ShapeDtypeStruct((B, S, IDX_DIM), BF16),
        jax.ShapeDtypeStruct((B, IDX_HEADS, S), F32),
    ]
    colT = lambda r: pl.BlockSpec((1, r, tm), lambda b, t: (b, 0, t))
    row = lambda c: pl.BlockSpec((1, tm, c), lambda b, t: (b, t, 0))
    slab = lambda r: pl.BlockSpec((1, n_sub, r, TILE), lambda b, t: (b, t, 0, 0))
    out_specs = [colT(BRANCH_WIDTH), row(BRANCH_WIDTH), slab(A_HEADS * A_VT_ROWS), colT(BRANCH_WIDTH),
                 row(LANES),
                 slab(B_KV_HEADS * C_VT_ROWS), colT(BRANCH_WIDTH), row(HEAD_DIM), slab(C_VT_ROWS), colT(LANES), row(IDX_DIM),
                 pl.BlockSpec((1, IDX_HEADS, tm), lambda b, t: (b, 0, t))]
    in_specs = [pl.BlockSpec((1, tm, D), lambda b, t: (b, t, 0)), full((1, D)), full((D, _W_COLS)),
                full((BRANCH_WIDTH, BRANCH_WIDTH))] + [full(gn.shape) for gn in gains]
    vmem = 2 * (tm * D * 4 + D * _W_COLS * 2 + BRANCH_WIDTH * BRANCH_WIDTH * 2 + tm * 3400 * 2) + 24 * tm * BRANCH_WIDTH * 4
    return pl.pallas_call(
        functools.partial(_proj_kernel, tm=tm, iw_scale=IDX_HEADS ** -0.5 * IDX_DIM ** -0.5),
        grid=(B, S // tm), in_specs=in_specs, out_specs=out_specs, out_shape=out_shape,
        compiler_params=pltpu.CompilerParams(dimension_semantics=("arbitrary", "arbitrary"),
                                             vmem_limit_bytes=_vmem_limit(vmem)),
        name="proj",
    )(x, g, w, seg, *gains)


def _online_step(s, m_old, vt, acc_ref, ch):
    if s.dtype == BF16:
        m_new = jnp.maximum(m_old, jnp.max(_colmax16(s).astype(F32), axis=0, keepdims=True))
        e = jnp.exp2(s - m_new.astype(BF16))
    else:
        m_new = jnp.maximum(m_old, jnp.max(_colmax8(s), axis=0, keepdims=True))
        e = jnp.exp2(s - m_new).astype(BF16)
    alpha = jnp.exp2(m_old - m_new)
    acc_ref[ch] = acc_ref[ch] * alpha + _dot(vt, e)
    return m_new


def _sweep_key_tiles(i, n_chain, scores, vt, acc_ref):
    n_far = jnp.maximum(i - (NEAR_TILES - 1), 0)

    def step(tiles, ms):
        ms = list(ms)
        chains = [(j, d, ch) for (j, d) in tiles for ch in range(n_chain)]
        pending = [scores(*c) for c in chains[:SKEW]]
        for n, (j, d, ch) in enumerate(chains):
            s = pending.pop(0)
            if n + SKEW < len(chains):
                pending.append(scores(*chains[n + SKEW]))
            ms[ch] = _online_step(s, ms[ch], vt(j, ch), acc_ref, ch)
        return tuple(ms)

    def near_pairs(ms):
        ms = step([(i - 3, 3), (i - 2, 2)], ms)
        return step([(i - 1, 1), (i, 0)], ms)

    def near_singles(ms):
        return lax.fori_loop(0, i + 1, lambda t, ms: step([(t, i - t)], ms), ms)

    assert NEAR_TILES == 4
    ms = tuple(jnp.full((1, TILE), -jnp.inf, F32) for _ in range(n_chain))
    ms = lax.fori_loop(0, n_far // 4, lambda p, ms: step([(4 * p + u, None) for u in range(4)], ms), ms)
    rest = n_far - n_far % 4
    ms = lax.cond(n_far % 4 >= 2, lambda ms: step([(rest, None), (rest + 1, None)], ms), lambda ms: ms, ms)
    ms = lax.cond(n_far % 2 == 1, lambda ms: step([(n_far - 1, None)], ms), lambda ms: ms, ms)
    lax.cond(i >= NEAR_TILES - 1, near_pairs, near_singles, ms)


def _attn_a_kernel(qT_ref, k_ref, vT_ref, bias_ref, lam_ref, sub_ref, o_ref, q2_ref, acc_ref, *, lambda_init):
    i = pl.program_id(1)
    lp = lam_ref[...]
    lam = (jnp.exp(jnp.sum(lp[0:1] * lp[1:2], axis=-1, keepdims=True))
           - jnp.exp(jnp.sum(lp[2:3] * lp[3:4], axis=-1, keepdims=True)) + lambda_init)
    row = lax.broadcasted_iota(I32, (2 * HEAD_DIM, TILE), 0)
    n_chain = 2 * A_HEADS

    for h in range(A_HEADS):
        qh = qT_ref[0, h * A_V_DIM:(h + 1) * A_V_DIM, :]
        zero = jnp.zeros_like(qh)
        q2_ref[2 * h] = jnp.where(row < HEAD_DIM, qh, zero)
        q2_ref[2 * h + 1] = jnp.where(row >= HEAD_DIM, qh, zero)
    acc_ref[...] = jnp.zeros_like(acc_ref)

    def scores(j, d, ch):
        h = ch // 2
        rows = pl.ds(pl.multiple_of(j * TILE, TILE), TILE)
        s = _dot(k_ref[0, rows, h * A_V_DIM:(h + 1) * A_V_DIM], q2_ref[ch]).astype(BF16)
        return s if d is None else s + bias_ref[h, d]

    def vt(j, ch):
        h = ch // 2
        return vT_ref[0, j, h * A_VT_ROWS:(h + 1) * A_VT_ROWS, :]

    _sweep_key_tiles(i, n_chain, scores, vt, acc_ref)

    for h in range(A_HEADS):
        a0 = acc_ref[2 * h]
        a1 = acc_ref[2 * h + 1]
        r0 = 1.0 / a0[A_V_DIM:A_V_DIM + 1, :]
        r1 = 1.0 / a1[A_V_DIM:A_V_DIM + 1, :]
        outT = a0[:A_V_DIM, :] * r0 - lam * (a1[:A_V_DIM, :] * r1)
        out = _rms(outT.T, sub_ref[...]) * (1.0 - lambda_init)
        o_ref[0, :, h * A_V_DIM:(h + 1) * A_V_DIM] = out.astype(BF16)


def _attn_b_kernel(sink_ref, qT_ref, k_ref, vT_ref, bias_ref, o_ref, oT_ref):
    i = pl.program_id(1)
    jp = jnp.maximum(i - 1, 0)
    p_idx = jnp.where(i > 0, 1, 2)
    cur = pl.ds(pl.multiple_of(i * TILE, TILE), TILE)
    prev = pl.ds(pl.multiple_of(jp * TILE, TILE), TILE)

    def scores(h):
        qh = qT_ref[0, h * HEAD_DIM:(h + 1) * HEAD_DIM, :]
        zero = jnp.zeros_like(qh)
        q2 = jnp.concatenate([qh, zero] if h < B_GROUP else [zero, qh], axis=0)
        return (_dot(k_ref[0, cur, :], q2).astype(BF16) + bias_ref[h, 0],
                _dot(k_ref[0, prev, :], q2).astype(BF16) + bias_ref[h, p_idx])

    def finish(h, s):
        sc, sp = s
        gs = slice((h // B_GROUP) * C_VT_ROWS, (h // B_GROUP + 1) * C_VT_ROWS)
        sink = sink_ref[h] * LOG2E
        m = jnp.max(jnp.maximum(_colmax16(sc), _colmax16(sp)).astype(F32), axis=0, keepdims=True)
        m = jnp.maximum(m, sink).astype(BF16)
        ec = jnp.exp2(sc - m)
        ep = jnp.exp2(sp - m)
        outT = _dot(vT_ref[0, i, gs, :], ec) + _dot(vT_ref[0, jp, gs, :], ep)
        den = outT[HEAD_DIM:HEAD_DIM + 1, :] + jnp.exp2(sink - m.astype(F32))
        oT_ref[h * HEAD_DIM:(h + 1) * HEAD_DIM, :] = outT[:HEAD_DIM, :] * (1.0 / den)

    pending = [scores(h) for h in range(SKEW)]
    for h in range(B_HEADS):
        s = pending.pop(0)
        if h + SKEW < B_HEADS:
            pending.append(scores(h + SKEW))
        finish(h, s)
    o_ref[0] = oT_ref[...].T.astype(BF16)


def _attn_c_kernel(qT_ref, k_ref, vT_ref, iqT_ref, ik_ref, iwT_ref, bias_ref, o_ref,
                   hi_ref, lo_ref, mb_ref, acc_ref, oT_ref, *, top_k):
    i = pl.program_id(1)
    n_t = i + 1
    krow = lax.broadcasted_iota(I32, (TILE, TILE), 0)
    qcol = lax.broadcasted_iota(I32, (TILE, TILE), 1)
    allowed = (krow // CHUNK) <= (qcol // CHUNK)

    def idx_keys(j, diag):
        ikt = ik_ref[0, pl.ds(pl.multiple_of(j * TILE, TILE), TILE), :]
        lgs = [_dot(ikt, iqT_ref[0, hh * IDX_DIM:(hh + 1) * IDX_DIM, :]) for hh in range(IDX_HEADS)]
        sc = jnp.zeros((TILE, TILE), F32)
        for hh in range(IDX_HEADS):
            sc = sc + jnp.maximum(lgs[hh], 0.0) * iwT_ref[0, hh:hh + 1, :]
        if diag:
            sc = jnp.where(allowed, sc, NEG)
        bits = lax.bitcast_convert_type(sc, I32)
        hi_b = (bits >> 16).astype(I16)
        lo_b = bits.astype(I16)
        sign = jnp.where(hi_b < 0, jnp.int16(-1), jnp.int16(0))
        hi_ref[j] = hi_b ^ (sign & 0x7FFF)
        lo_ref[j] = lo_b ^ sign ^ I16_MIN

    def fill_pair(p, carry):
        idx_keys(2 * p, False)
        idx_keys(2 * p + 1, False)
        return carry

    lax.fori_loop(0, i // 2, fill_pair, 0)

    @pl.when(i % 2 == 1)
    def _():
        idx_keys(i - 1, False)

    idx_keys(i, True)

    n_pair = (n_t + 1) // 2

    @pl.when(n_t % 2 == 1)
    def _():
        hi_ref[n_t] = jnp.full((TILE, TILE), I16_MIN, I16)
        lo_ref[n_t] = jnp.full((TILE, TILE), I16_MIN, I16)

    groups = TILE // BF16_ROWS

    def rows16(ref, j):
        return ref[j].reshape(groups, BF16_ROWS, TILE)

    def bcast16(v):
        return jnp.broadcast_to(v, (BF16_ROWS, TILE)).astype(I16)

    def count(pred):
        def body(p, cs):
            cs = list(cs)
            for u in (0, 1):
                j = 2 * p + u
                hi, lo = rows16(hi_ref, j), rows16(lo_ref, j)
                for r in range(groups):
                    cs[r % len(cs)] = cs[r % len(cs)] + pred(hi[r], lo[r], j, r).astype(I16)
            return tuple(cs)
        cs = lax.fori_loop(0, n_pair, body, (jnp.zeros((BF16_ROWS, TILE), I16),) * 4)
        c = (cs[0] + cs[1]) + (cs[2] + cs[3])
        return jnp.sum(c.astype(I32), axis=0, keepdims=True)

    def search(n_bits, accept):
        def step(b, t):
            cand = t + lax.shift_left(jnp.int32(1), n_bits - 1 - b)
            return jnp.where(accept(cand), cand, t)
        return lax.fori_loop(0, n_bits, step, jnp.full((1, TILE), I16_MIN, I32))

    def hi_accept(cand):
        c16 = bcast16(cand)
        return count(lambda hi, lo, j, r: hi >= c16) >= top_k

    t_hi = search(16, hi_accept)
    th16 = bcast16(t_hi)
    cnt_above = count(lambda hi, lo, j, r: hi > th16)
    r_lo = top_k - cnt_above

    def bucket_only(p, carry):
        for u in (0, 1):
            j = 2 * p + u
            hi, lo = rows16(hi_ref, j), rows16(lo_ref, j)
            for r in range(groups):
                lo_ref[j, r * BF16_ROWS:(r + 1) * BF16_ROWS, :] = jnp.where(hi[r] == th16, lo[r], I16_MIN)
        return carry

    lax.fori_loop(0, n_pair, bucket_only, 0)

    def lo_accept(cand):
        c16 = bcast16(cand)
        return count(lambda hi, lo, j, r: lo >= c16) >= r_lo

    t_lo = search(16, lo_accept)
    tl16 = bcast16(t_lo)
    cnt_gt = count(lambda hi, lo, j, r: lo > tl16)
    r_eq = (r_lo - cnt_gt).astype(F32)

    tri = (krow >= qcol).astype(BF16)
    one, zero, neg16 = jnp.ones((), BF16), jnp.zeros((), BF16), jnp.full((), NEG, BF16)

    def mask_tile(j, ties_before, diag):
        hi, lo = rows16(hi_ref, j), rows16(lo_ref, j)
        eqs = [(lo[r] == tl16) & (hi[r] == th16) for r in range(groups)]
        ties = _dot(tri, jnp.concatenate([jnp.where(e, one, zero) for e in eqs], axis=0)) + ties_before
        over = (ties - r_eq).astype(BF16)
        rows = []
        for r in range(groups):
            kept_tie = eqs[r] & (over[r * BF16_ROWS:(r + 1) * BF16_ROWS, :] <= zero)
            rows.append(jnp.where((hi[r] > th16) | (lo[r] > tl16) | kept_tie, zero, neg16))
        mb = jnp.concatenate(rows, axis=0)
        if diag:
            mb = jnp.where(allowed, mb.astype(F32), NEG).astype(BF16)
        mb_ref[j] = mb
        return ties[TILE - 1:TILE, :]

    ties_before = lax.fori_loop(0, i // 2, lambda p, c: mask_tile(2 * p + 1, mask_tile(2 * p, c, False), False),
                                jnp.zeros((1, TILE), F32))
    ties_before = lax.cond(i % 2 == 1, lambda c: mask_tile(i - 1, c, False), lambda c: c, ties_before)
    mask_tile(i, ties_before, True)

    acc_ref[...] = jnp.zeros_like(acc_ref)

    def scores(j, d, h):
        kt = k_ref[0, pl.ds(pl.multiple_of(j * TILE, TILE), TILE), :]
        s = _dot(kt, qT_ref[0, h * HEAD_DIM:(h + 1) * HEAD_DIM, :]).astype(BF16) + mb_ref[j]
        return s if d is None else s + bias_ref[h, d]

    _sweep_key_tiles(i, C_HEADS, scores, lambda j, h: vT_ref[0, j], acc_ref)

    for h in range(C_HEADS):
        a = acc_ref[h]
        oT_ref[h * HEAD_DIM:(h + 1) * HEAD_DIM, :] = a[:HEAD_DIM, :] * (1.0 / a[HEAD_DIM:HEAD_DIM + 1, :])
    o_ref[0] = oT_ref[...].T.astype(BF16)


def _attn_kernel(*refs, lambda_init, top_k):
    a_in, b_in, c_in = refs[0:6], refs[6:11], refs[11:18]
    oa_ref, ob_ref, oc_ref = refs[18:21]
    a_scratch, b_scratch, c_scratch = refs[21:23], refs[23:24], refs[24:29]
    _attn_a_kernel(*a_in, oa_ref, *a_scratch, lambda_init=lambda_init)
    _attn_b_kernel(*b_in, ob_ref, *b_scratch)
    _attn_c_kernel(*c_in, oc_ref, *c_scratch, top_k=top_k)


def _attn(aqT, ak, avT, bias_a, lam_par, subln_g, sinks, bqT, bk, bvT, bias_b, cqT, ck, cvT, iqT, ik, iwT, bias_c,
          lambda_init, top_k):
    B, _, S = aqT.shape
    nt = S // TILE
    const = lambda a: pl.BlockSpec(a.shape, lambda b, i: (0,) * a.ndim, pipeline_mode=pl.Buffered(1))
    qtile = lambda rows: pl.BlockSpec((1, rows, TILE), lambda b, i: (b, 0, i))
    per_batch = lambda a: pl.BlockSpec((1,) + a.shape[1:], lambda b, i: (b,) + (0,) * (a.ndim - 1))
    in_specs = [qtile(BRANCH_WIDTH), per_batch(ak), per_batch(avT), const(bias_a), const(lam_par), const(subln_g),
                pl.BlockSpec(memory_space=pltpu.SMEM), qtile(BRANCH_WIDTH), per_batch(bk), per_batch(bvT),
                const(bias_b),
                qtile(BRANCH_WIDTH), per_batch(ck), per_batch(cvT), qtile(LANES), per_batch(ik), qtile(IDX_HEADS),
                const(bias_c)]
    out_spec = pl.BlockSpec((1, TILE, BRANCH_WIDTH), lambda b, i: (b, i, 0))
    out_shape = jax.ShapeDtypeStruct((B, S, BRANCH_WIDTH), BF16)
    scratch = [pltpu.VMEM((2 * A_HEADS, A_V_DIM, TILE), BF16),
               pltpu.VMEM((2 * A_HEADS, A_VT_ROWS, TILE), F32),
               pltpu.VMEM((BRANCH_WIDTH, TILE), F32),
               pltpu.VMEM((nt + nt % 2, TILE, TILE), I16),
               pltpu.VMEM((nt + nt % 2, TILE, TILE), I16),
               pltpu.VMEM((nt, TILE, TILE), BF16),
               pltpu.VMEM((C_HEADS, C_VT_ROWS, TILE), F32),
               pltpu.VMEM((BRANCH_WIDTH, TILE), F32)]
    per_batch_bytes = sum(a.size // B * a.dtype.itemsize for a in (ak, avT, bk, bvT, ck, cvT, ik))
    const_bytes = sum(a.size * a.dtype.itemsize for a in (bias_a, bias_b, bias_c))
    scratch_bytes = (2 * A_HEADS * (A_V_DIM * 2 + A_VT_ROWS * 4) + 2 * BRANCH_WIDTH * 4 + C_HEADS * C_VT_ROWS * 4) * TILE \
        + (2 * (nt + nt % 2) * 2 + nt * 2) * TILE * TILE
    vmem = 2 * per_batch_bytes + const_bytes + scratch_bytes + 12 * TILE * BRANCH_WIDTH * 2 + 64 * TILE * TILE * 4
    return pl.pallas_call(
        functools.partial(_attn_kernel, lambda_init=lambda_init, top_k=top_k),
        grid=(B, nt), in_specs=in_specs, out_specs=[out_spec] * 3, out_shape=[out_shape] * 3,
        scratch_shapes=scratch,
        compiler_params=pltpu.CompilerParams(dimension_semantics=("arbitrary", "arbitrary"),
                                             vmem_limit_bytes=_vmem_limit(vmem)),
        name="attn",
    )(aqT, ak, avT, bias_a, lam_par, subln_g, sinks, bqT, bk, bvT, bias_b, cqT, ck, cvT, iqT, ik, iwT, bias_c)


def _merge_kernel(x_ref, oa_ref, ob_ref, oc_ref, gmix_ref, wg_ref, bg_ref, wb_ref, wo_ref, gffn_ref,
                  wrh_ref, wrl_ref, br_ref, x1_ref, h2_ref, gate_ref):
    x = x_ref[...]
    hb = _rms(x, gmix_ref[...]).astype(BF16)
    z = None
    for n, o_ref in enumerate((oa_ref, ob_ref, oc_ref)):
        cs = slice(n * D_MODEL, (n + 1) * D_MODEL)
        gate = _sigmoid(_dot(hb, wg_ref[:, cs]) + bg_ref[:, cs])
        y = _dot(o_ref[...], wb_ref[n])
        z = gate * y if z is None else z + gate * y
    x1 = x + _dot(z.astype(BF16), wo_ref[...])
    x1_ref[...] = x1
    h2 = _rms(x1, gffn_ref[...])
    h2_ref[...] = h2.astype(BF16)

    hi, lo = _split_bf16(h2)
    lg = _dot(hi, wrh_ref[...]) + _dot(lo, wrh_ref[...]) + _dot(hi, wrl_ref[...]) + br_ref[...]
    col = lax.broadcasted_iota(I32, lg.shape, 1).astype(F32)
    big = float(4 * LANES)
    is_g = (col >= N_EXPERTS) & (col < N_EXPERTS + N_GROUPS)
    gl = jnp.where(is_g, lg, -jnp.inf)
    gmax = jnp.max(gl, axis=-1, keepdims=True)
    p_group = 1.0 / jnp.sum(jnp.exp(gl - gmax), axis=-1, keepdims=True)
    g_sel = jnp.min(jnp.where(gl == gmax, col, big), axis=-1, keepdims=True) - N_EXPERTS
    in_g = (col >= g_sel * EXPERTS_PER_GROUP) & (col < (g_sel + 1) * EXPERTS_PER_GROUP)
    el = jnp.where(in_g, lg, -jnp.inf)
    e1 = jnp.max(el, axis=-1, keepdims=True)
    i1 = jnp.min(jnp.where(el == e1, col, big), axis=-1, keepdims=True)
    el2 = jnp.where(col == i1, -jnp.inf, el)
    e2 = jnp.max(el2, axis=-1, keepdims=True)
    i2 = jnp.min(jnp.where(el2 == e2, col, big), axis=-1, keepdims=True)
    t2 = jnp.exp(e2 - e1)
    w1 = p_group / (1.0 + t2)
    w2 = w1 * t2
    gates = jnp.where(col == i1, w1, 0.0) + jnp.where(col == i2, w2, 0.0)
    gate_ref[...] = gates[:, :N_EXPERTS]


def _merge(x2, oa, ob, oc, gmix, wg, bg, wb, wo, gffn, wrh, wrl, br, tm):
    T, D = x2.shape
    full = lambda a: pl.BlockSpec(a.shape, lambda t: (0,) * a.ndim, pipeline_mode=pl.Buffered(1))
    rowb = lambda c: pl.BlockSpec((tm, c), lambda t: (t, 0))
    vmem = 2 * (tm * D * 4 * 2 + 3 * tm * BRANCH_WIDTH * 2 + tm * D * 2) + wg.size * 2 + wb.size * 2 + wo.size * 2 \
        + 2 * D * LANES * 2 + 10 * tm * D * 4
    return pl.pallas_call(
        _merge_kernel,
        grid=(T // tm,),
        in_specs=[rowb(D), rowb(BRANCH_WIDTH), rowb(BRANCH_WIDTH), rowb(BRANCH_WIDTH), full(gmix), full(wg), full(bg), full(wb),
                  full(wo), full(gffn), full(wrh), full(wrl), full(br)],
        out_specs=[rowb(D), rowb(D), rowb(N_EXPERTS)],
        out_shape=[jax.ShapeDtypeStruct((T, D), F32), jax.ShapeDtypeStruct((T, D), BF16),
                   jax.ShapeDtypeStruct((T, N_EXPERTS), F32)],
        compiler_params=pltpu.CompilerParams(dimension_semantics=("arbitrary",),
                                             vmem_limit_bytes=_vmem_limit(vmem)),
        name="merge",
    )(x2, oa, ob, oc, gmix, wg, bg, wb, wo, gffn, wrh, wrl, br)


def _moe_kernel(x1_ref, h2_ref, gate_ref, wgu_ref, wd_ref, o_ref):
    g = pl.program_id(1)
    h2 = h2_ref[...]
    gate = gate_ref[...]
    lane = lax.broadcasted_iota(I32, gate.shape, 1)
    hids = []
    for u in range(MOE_STEP_EXPERTS):
        gu = _dot(h2, wgu_ref[u])
        a = gu[:, :EXPERT_FF]
        w = jnp.sum(jnp.where(lane == g * MOE_STEP_EXPERTS + u, gate, 0.0), axis=-1, keepdims=True)
        hids.append((a * _sigmoid(a) * gu[:, EXPERT_FF:] * w).astype(BF16))
    y = _dot(jnp.concatenate(hids, axis=1), wd_ref[...])

    @pl.when(g == 0)
    def _():
        o_ref[...] = x1_ref[...] + y

    @pl.when(g > 0)
    def _():
        o_ref[...] += y


def _moe(x1, h2, gate, wgu, wd, tm):
    T, D = x1.shape
    n_e, ff = MOE_STEP_EXPERTS, EXPERT_FF
    vmem = 2 * (tm * D * 4 * 2 + tm * D * 2 + tm * LANES * 4) + n_e * D * 2 * ff * 2 + n_e * ff * D * 2 \
        + tm * n_e * ff * 2 + 4 * tm * 2 * ff * 4 + 2 * tm * D * 4
    return pl.pallas_call(
        _moe_kernel,
        grid=(T // tm, N_EXPERTS // n_e),
        in_specs=[pl.BlockSpec((tm, D), lambda t, g: (t, 0)),
                  pl.BlockSpec((tm, D), lambda t, g: (t, 0)),
                  pl.BlockSpec((tm, N_EXPERTS), lambda t, g: (t, 0)),
                  pl.BlockSpec((n_e, D, 2 * ff), lambda t, g: (g, 0, 0), pipeline_mode=pl.Buffered(1)),
                  pl.BlockSpec((n_e * ff, D), lambda t, g: (g, 0), pipeline_mode=pl.Buffered(1))],
        out_specs=pl.BlockSpec((tm, D), lambda t, g: (t, 0)),
        out_shape=jax.ShapeDtypeStruct((T, D), F32),
        compiler_params=pltpu.CompilerParams(dimension_semantics=("arbitrary", "arbitrary"),
                                             vmem_limit_bytes=_vmem_limit(vmem)),
        name="moe",
    )(x1, h2, gate, wgu, wd)


def _t5_bucket_np(rel):
    half = NUM_BUCKETS // 2
    max_exact = half // 2
    n = np.abs(rel)
    n_f = np.maximum(n, 1).astype(np.float32)
    large = max_exact + (np.log(n_f / np.float32(max_exact)) / np.float32(math.log(MAX_DISTANCE / max_exact))
                         * np.float32(half - max_exact)).astype(np.int32)
    large = np.minimum(large, half - 1)
    return np.where(rel > 0, half, 0) + np.where(n < max_exact, n, large)


def _toeplitz_kernel(v_ref, o_ref):
    v = v_ref[0]
    for d in range(NEAR_TILES):
        x = jnp.broadcast_to(v[d:d + 1, :], (TILE, 2 * TILE))
        o_ref[0, d] = pltpu.roll(x, 0, 1, stride=1, stride_axis=0)[:, :TILE]


def _toeplitz(vals):
    n_heads = vals.shape[0]
    return pl.pallas_call(
        _toeplitz_kernel,
        grid=(n_heads,),
        in_specs=[pl.BlockSpec((1, NEAR_TILES, 2 * TILE), lambda h: (h, 0, 0))],
        out_specs=pl.BlockSpec((1, NEAR_TILES, TILE, TILE), lambda h: (h, 0, 0, 0)),
        out_shape=jax.ShapeDtypeStruct((n_heads, NEAR_TILES, TILE, TILE), F32),
        name="bias_tiles",
    )(vals)


def _bias_tables(rel_bias):
    u = np.arange(2 * TILE)
    off = np.where(u < TILE, u, u - 2 * TILE)
    rel = np.stack([-off - TILE * d for d in range(NEAR_TILES)])
    onehot = (_t5_bucket_np(rel)[..., None] == np.arange(NUM_BUCKETS)).astype(np.float32)
    vals = jnp.einsum("dub,bh->hdu", jnp.asarray(onehot), rel_bias, precision=lax.Precision.HIGHEST)
    tab = _toeplitz(vals)

    far = rel_bias[NUM_BUCKETS // 2 - 1]
    kk = np.arange(TILE)[:, None]
    qc = np.arange(TILE)[None, :] // CHUNK
    kc = np.stack([(kk // CHUNK) - (TILE // CHUNK) * d for d in range(NEAR_TILES)]) + 0 * qc
    a_ok = jnp.asarray(kc <= qc)
    ta = jnp.where(a_ok[None], (tab[:A_HEADS] - far[:A_HEADS, None, None, None]) * LOG2E, NEG)
    b_ok = jnp.asarray((qc - kc[:2] >= 0) & (qc - kc[:2] <= W_CHUNKS))
    tb = jnp.where(b_ok[None], tab[A_HEADS:A_HEADS + B_HEADS, :2] * LOG2E, NEG)
    tb = jnp.concatenate([tb, jnp.full((B_HEADS, 1, TILE, TILE), NEG, F32)], axis=1)
    tc = (tab[A_HEADS + B_HEADS:] - far[A_HEADS + B_HEADS:, None, None, None]) * LOG2E
    return ta.astype(BF16), tb.astype(BF16), tc.astype(BF16)


def _proj_weight_kernel(w_ref, o_ref):
    o_ref[...] = jnp.zeros_like(o_ref)
    o_ref[:, :_IN_COLS] = w_ref[0].astype(BF16)


def _proj_weight(w_in, l):
    _, d, n = w_in.shape
    assert n == _IN_COLS
    rows = 128
    return pl.pallas_call(
        _proj_weight_kernel,
        grid=(d // rows,),
        in_specs=[pl.BlockSpec((1, rows, n), lambda r: (l, r, 0))],
        out_specs=pl.BlockSpec((rows, _W_COLS), lambda r: (r, 0)),
        out_shape=jax.ShapeDtypeStruct((d, _W_COLS), BF16),
        name="proj_weight",
    )(w_in)


def kernel(x, rel_bias, norm_mix_g, w_in, qk_norm_g, diff_lambda, diff_subln_g, sinks, w_branch, w_gate, b_gate,
           w_out, norm_ffn_g, w_router_group, b_router_group, w_router_expert, b_router_expert, w_ff_gate,
           w_ff_up, w_ff_down):
    B, S, D = x.shape
    assert D == D_MODEL and S % TILE == 0
    T = B * S
    top_k = min(TOPK_MAX, S // 4)
    tm_proj = 1024 if S % 1024 == 0 else TILE
    tm_merge = 1024 if T % 1024 == 0 else TILE
    tm_moe = 1024 if T % 1024 == 0 else TILE

    bias_a, bias_b, bias_c = _bias_tables(rel_bias)
    seg = jnp.asarray(np.kron(np.eye(BRANCH_WIDTH // HEAD_DIM), np.ones((HEAD_DIM, HEAD_DIM))), BF16)
    q_scale = HEAD_DIM ** -0.5

    for l in range(DEPTH):
        lambda_init = 0.8 - 0.6 * math.exp(-0.3 * l)
        qg = qk_norm_g[l]
        tile8 = lambda g: jnp.tile(g, BRANCH_WIDTH // HEAD_DIM)
        gains = ((tile8(qg[0, 0]) * (q_scale * LOG2E))[:, None], tile8(qg[0, 1])[None, :],
                 (tile8(qg[1, 0]) * (q_scale * LOG2E))[:, None], jnp.tile(qg[1, 1], B_KV_HEADS)[None, :],
                 (tile8(qg[2, 0]) * (q_scale * LOG2E))[:, None], qg[2, 1][None, :])
        (aqT, ak, avT, bqT, bk, bvT, cqT, ck, cvT, iqT, ik, iwT) = _proj(
            x, norm_mix_g[l][None, :], _proj_weight(w_in, l), seg, gains, tm_proj)

        oa, ob, oc = _attn(aqT, ak, avT, bias_a, diff_lambda[l], diff_subln_g[l][None, :], sinks[l], bqT, bk, bvT,
                           bias_b, cqT, ck, cvT, iqT, ik, iwT, bias_c, lambda_init, top_k)

        w_r = jnp.concatenate([w_router_expert[l], w_router_group[l],
                               jnp.zeros((D, LANES - N_EXPERTS - N_GROUPS), F32)], axis=1)
        b_r = jnp.concatenate([b_router_expert[l], b_router_group[l],
                               jnp.zeros((LANES - N_EXPERTS - N_GROUPS,), F32)])[None, :]
        wrh = w_r.astype(BF16)
        wrl = (w_r - wrh.astype(F32)).astype(BF16)
        x1, h2, gate = _merge(
            x.reshape(T, D), oa.reshape(T, BRANCH_WIDTH), ob.reshape(T, BRANCH_WIDTH), oc.reshape(T, BRANCH_WIDTH),
            norm_mix_g[l][None, :], w_gate[l].astype(BF16), b_gate[l][None, :], w_branch[l].astype(BF16),
            w_out[l].astype(BF16), norm_ffn_g[l][None, :], wrh, wrl, b_r, tm_merge)

        wgu = jnp.concatenate([w_ff_gate[l], w_ff_up[l]], axis=-1).astype(BF16)
        wd = w_ff_down[l].astype(BF16).reshape(N_EXPERTS * EXPERT_FF, D)
        x = _moe(x1, h2, gate, wgu, wd, tm_moe).reshape(B, S, D)
    return x
```
